```python
import math
import jax, jax.numpy as jnp
from jax import lax
import numpy as np

D_MODEL = 1024
BATCH = 2
SEQ = 16384
DEPTH = 4

N_MIXERS = 3
N_CONF_LAYERS = (DEPTH + 2) // 3
N_SC_LAYERS = (DEPTH + 1) // 3
N_NSA_LAYERS = DEPTH // 3
N_DENSE_LAYERS = (DEPTH + 1) // 2
N_MOE_LAYERS = DEPTH // 2

RMS_EPS = 1e-6
LN_EPS = 1e-5
NEG_INF = -1e30
FORCE_SCORE = 1e30

CONF_KERNEL = 31
SHORT_KERNEL = 3
N_HEADS = 16
HEAD_DIM = D_MODEL // N_HEADS
N_KV_GROUPS = 4
HEADS_PER_GROUP = N_HEADS // N_KV_GROUPS
KV_WIDTH = N_KV_GROUPS * HEAD_DIM
CMP_LEN = 32
CMP_STRIDE = 16
CMP_HIDDEN = 2 * HEAD_DIM
SEL_BLOCK = 64
SEL_TOPK = 16
N_LOCAL_BLOCKS = 2
WINDOW = 512
Q_BLOCK = 128
NSA_IN_WIDTH = N_HEADS * HEAD_DIM + 6 * KV_WIDTH + 3 * N_HEADS
REL_BUCKETS = 32
REL_MAX_DIST = 2048
D_FF_DENSE = 2816
N_EXPERTS = 8
TOP_K = 2
D_FF_EXPERT = 3584
MOE_BLOCK = 512

kernel_name = "hybrid_conv_nsa_moe_trunk"


def rms_norm(x, g):
    xf = x.astype(jnp.float32)
    y = xf * lax.rsqrt(jnp.mean(xf * xf, axis=-1, keepdims=True) + RMS_EPS)
    return (y * g.astype(jnp.float32)).astype(x.dtype)


def layer_norm(x, g, b):
    xf = x.astype(jnp.float32)
    mu = jnp.mean(xf, axis=-1, keepdims=True)
    var = jnp.mean(jnp.square(xf - mu), axis=-1, keepdims=True)
    y = (xf - mu) * lax.rsqrt(var + LN_EPS) * g.astype(jnp.float32) + b.astype(jnp.float32)
    return y.astype(x.dtype)


def adaln_params(c, w, b):
    m = (jax.nn.silu(c) @ w + b)[:, None, :]
    shift, scale, gate = jnp.split(m, 3, axis=-1)
    return shift, scale, gate


def causal_depthwise_conv(x, w):
    k, ch = w.shape
    return lax.conv_general_dilated(
        x, w.astype(x.dtype)[:, None, :], window_strides=(1,), padding=((k - 1, 0),),
        dimension_numbers=("NWC", "WIO", "NWC"), feature_group_count=ch)


def conformer_conv_module(h, w_pw1, b_pw1, w_dw, b_dw, ln_g, ln_b, w_pw2, b_pw2):
    a, gl = jnp.split(h @ w_pw1 + b_pw1, 2, axis=-1)
    u = a * jax.nn.sigmoid(gl)
    u = causal_depthwise_conv(u, w_dw) + b_dw
    u = jax.nn.silu(layer_norm(u, ln_g, ln_b))
    return u @ w_pw2 + b_pw2


def short_conv_mixer(h, w_in, w_conv, w_out):
    v, gate_b, gate_c = jnp.split(h @ w_in, 3, axis=-1)
    u = causal_depthwise_conv(gate_c * v, w_conv)
    return (gate_b * u) @ w_out


def rel_bucket(dist):
    n = jnp.maximum(dist, 0)
    max_exact = REL_BUCKETS // 2
    nf = jnp.maximum(n, 1).astype(jnp.float32)
    large = max_exact + (jnp.log(nf / max_exact) / math.log(REL_MAX_DIST / max_exact)
                         * (REL_BUCKETS - max_exact)).astype(jnp.int32)
    large = jnp.minimum(large, REL_BUCKETS - 1)
    return jnp.where(n < max_exact, n, large)


def masked_softmax(s, mask):
    p = jax.nn.softmax(jnp.where(mask, s, NEG_INF), axis=-1)
    return jnp.where(mask, p, 0.0)


def compress_blocks(blocks, pe, w1, w2):
    hid = jax.nn.gelu(jnp.einsum("bnlgd,lde->bnge", blocks + pe[None, None, :, None, :], w1))
    return jnp.einsum("bnge,ed->bgnd", hid, w2)


def nsa_attention(h, w_in, k_pe, k_w1, k_w2, v_pe, v_w1, v_w2, rel_table, w_out):
    B, S, D = h.shape
    G, HPG, DH = N_KV_GROUPS, HEADS_PER_GROUP, HEAD_DIM
    n_qb = S // Q_BLOCK
    n_cmp = (S - CMP_LEN) // CMP_STRIDE + 1
    n_sel = S // SEL_BLOCK
    k_eff = min(SEL_TOPK, n_sel)
    scale = HEAD_DIM ** -0.5

    z = h @ w_in
    cuts = [N_HEADS * DH + i * KV_WIDTH for i in range(7)]
    q, k_c, v_c, k_s, v_s, k_w, v_w, g = jnp.split(z, cuts, axis=-1)

    cmp_idx = np.arange(n_cmp)[:, None] * CMP_STRIDE + np.arange(CMP_LEN)[None, :]
    cmp_end_np = np.arange(n_cmp) * CMP_STRIDE + CMP_LEN - 1
    cmp_end = jnp.asarray(cmp_end_np, jnp.int32)
    kc = compress_blocks(k_c.reshape(B, S, G, DH)[:, cmp_idx], k_pe, k_w1, k_w2)
    vc = compress_blocks(v_c.reshape(B, S, G, DH)[:, cmp_idx], v_pe, v_w1, v_w2)

    sel_start_np = np.arange(n_sel) * SEL_BLOCK
    cmp_start_np = np.arange(n_cmp) * CMP_STRIDE
    overlap = jnp.asarray(((cmp_start_np[:, None] <= sel_start_np[None, :] + SEL_BLOCK - 1)
                           & (cmp_end_np[:, None] >= sel_start_np[None, :])).astype(np.float32))

    ks_blk = k_s.reshape(B, n_sel, SEL_BLOCK, G, DH).transpose(0, 3, 1, 2, 4)
    vs_blk = v_s.reshape(B, n_sel, SEL_BLOCK, G, DH).transpose(0, 3, 1, 2, 4)
    pad = ((0, 0), (0, 0), (WINDOW, 0), (0, 0))
    kw = jnp.pad(k_w.reshape(B, S, G, DH).transpose(0, 2, 1, 3), pad)
    vw = jnp.pad(v_w.reshape(B, S, G, DH).transpose(0, 2, 1, 3), pad)

    qb_all = q.reshape(B, n_qb, Q_BLOCK, G, HPG, DH).transpose(1, 0, 3, 4, 2, 5)
    gb_all = jax.nn.sigmoid(g).reshape(B, n_qb, Q_BLOCK, G, HPG, 3).transpose(1, 0, 3, 4, 2, 5)

    bi = jnp.arange(B)[:, None, None, None]
    gi = jnp.arange(G)[None, :, None, None]
    tbl_group = rel_table.reshape(REL_BUCKETS, G, HPG).transpose(1, 0, 2)
    jsel = jnp.arange(n_sel)

    def head_bias(bucket):
        qn, kn = bucket.shape
        return rel_table[bucket].transpose(2, 0, 1).reshape(G, HPG, qn, kn)

    def block_fn(args):
        qb, gb, blk = args
        q0 = blk * Q_BLOCK
        t = q0 + jnp.arange(Q_BLOCK)

        dist_c = t[:, None] - cmp_end[None, :]
        s_c = jnp.einsum("bghqd,bgnd->bghqn", qb, kc, preferred_element_type=jnp.float32) * scale
        p_c = masked_softmax(s_c + head_bias(rel_bucket(dist_c)), dist_c >= 0)
        o_c = jnp.einsum("bghqn,bgnd->bghqd", p_c.astype(vc.dtype), vc)

        p_sel = jnp.einsum("bghqn,ns->bgqs", p_c, overlap)
        cur = t // SEL_BLOCK
        back = cur[:, None] - jsel[None, :]
        valid = back >= 0
        forced = (jsel[None, :] == 0) | (valid & (back < N_LOCAL_BLOCKS))
        score = jnp.where(forced, FORCE_SCORE, jnp.where(valid, p_sel, -1.0))
        _, sel_idx = lax.top_k(score, k_eff)
        k_sel = ks_blk[bi, gi, sel_idx].reshape(B, G, Q_BLOCK, k_eff * SEL_BLOCK, DH)
        v_sel = vs_blk[bi, gi, sel_idx].reshape(B, G, Q_BLOCK, k_eff * SEL_BLOCK, DH)
        pos_s = (sel_idx[..., None] * SEL_BLOCK + jnp.arange(SEL_BLOCK)).reshape(
            B, G, Q_BLOCK, k_eff * SEL_BLOCK)
        dist_s = t[None, None, :, None] - pos_s
        bias_s = tbl_group[gi, rel_bucket(dist_s)].transpose(0, 1, 4, 2, 3)
        s_s = jnp.einsum("bghqd,bgqkd->bghqk", qb, k_sel, preferred_element_type=jnp.float32) * scale
        p_s = masked_softmax(s_s + bias_s, (dist_s >= 0)[:, :, None])
        o_s = jnp.einsum("bghqk,bgqkd->bghqd", p_s.astype(v_sel.dtype), v_sel)

        kwb = lax.dynamic_slice_in_dim(kw, q0, WINDOW + Q_BLOCK, axis=2)
        vwb = lax.dynamic_slice_in_dim(vw, q0, WINDOW + Q_BLOCK, axis=2)
        pos_w = q0 - WINDOW + jnp.arange(WINDOW + Q_BLOCK)
        dist_w = t[:, None] - pos_w[None, :]
        mask_w = (dist_w >= 0) & (dist_w < WINDOW) & (pos_w[None, :] >= 0)
        s_w = jnp.einsum("bghqd,bgkd->bghqk", qb, kwb, preferred_element_type=jnp.float32) * scale
        p_w = masked_softmax(s_w + head_bias(rel_bucket(dist_w)), mask_w)
        o_w = jnp.einsum("bghqk,bgkd->bghqd", p_w.astype(vwb.dtype), vwb)

        return gb[..., 0:1] * o_c + gb[..., 1:2] * o_s + gb[..., 2:3] * o_w

    o = lax.map(block_fn, (qb_all, gb_all, jnp.arange(n_qb)))
    o = o.transpose(1, 0, 4, 2, 3, 5).reshape(B, S, D)
    return o @ w_out


def swiglu(h, w_gate, w_up, w_down):
    return (jax.nn.silu(h @ w_gate) * (h @ w_up)) @ w_down


def moe_swiglu(h, w_router, b_router, w_gate, w_up, w_down):
    B, S, D = h.shape
    xt = h.reshape(B * S, D)
    n_tok = B * S
    n_asg = n_tok * TOP_K
    logits = (xt @ w_router).astype(jnp.float32) + b_router.astype(jnp.float32)
    top_logit, top_e = lax.top_k(logits, TOP_K)
    top_w = jax.nn.softmax(top_logit, axis=-1)
    flat_e = top_e.reshape(-1)
    flat_tok = jnp.repeat(jnp.arange(n_tok), TOP_K)
    flat_w = top_w.reshape(-1)
    order = jnp.argsort(flat_e)
    se, stok, sw = flat_e[order], flat_tok[order], flat_w[order]
    counts = jnp.bincount(flat_e, length=N_EXPERTS)
    start = jnp.cumsum(counts) - counts
    padded = (counts + MOE_BLOCK - 1) // MOE_BLOCK * MOE_BLOCK
    pend = jnp.cumsum(padded)
    pstart = pend - padded
    dest = pstart[se] + (jnp.arange(n_asg) - start[se])
    n_blocks = -(-(n_asg + N_EXPERTS * (MOE_BLOCK - 1)) // MOE_BLOCK)
    rows = n_blocks * MOE_BLOCK
    buf = jnp.zeros((rows, D), h.dtype).at[dest].set(xt[stok])
    blk_e = jnp.minimum(jnp.sum((jnp.arange(n_blocks) * MOE_BLOCK)[:, None] >= pend[None, :], axis=1),
                        N_EXPERTS - 1)

    def expert_block(args):
        xb, e = args
        return (jax.nn.silu(xb @ w_gate[e]) * (xb @ w_up[e])) @ w_down[e]

    yb = lax.map(expert_block, (buf.reshape(n_blocks, MOE_BLOCK, D), blk_e)).reshape(rows, D)
    y_sorted = yb[dest] * sw[:, None].astype(h.dtype)
    out = jnp.zeros((n_tok, D), h.dtype).at[stok].add(y_sorted)
    return out.reshape(B, S, D)


def setup_inputs(seed: int = 0) -> dict:
    key = jax.random.key(seed)
    keys = iter(jax.random.split(key, 64))
    D = D_MODEL

    def nrm(shape, s):
        return jax.random.normal(next(keys), shape, jnp.float32) * s

    def gain(shape):
        return 1.0 + nrm(shape, 0.05)

    return {
        "x": nrm((BATCH, SEQ, D), 1.0),
        "c": nrm((BATCH, D), 1.0),
        "mix_pre_g": gain((DEPTH, D)),
        "mix_post_g": gain((DEPTH, D)),
        "mix_ada_w": nrm((DEPTH, D, 3 * D), 0.5 * D ** -0.5),
        "mix_ada_b": nrm((DEPTH, 3 * D), 0.02),
        "ffn_pre_g": gain((DEPTH, D)),
        "ffn_post_g": gain((DEPTH, D)),
        "ffn_ada_w": nrm((DEPTH, D, 3 * D), 0.5 * D ** -0.5),
        "ffn_ada_b": nrm((DEPTH, 3 * D), 0.02),
        "conf_w_pw1": nrm((N_CONF_LAYERS, D, 2 * D), D ** -0.5),
        "conf_b_pw1": nrm((N_CONF_LAYERS, 2 * D), 0.02),
        "conf_w_dw": nrm((N_CONF_LAYERS, CONF_KERNEL, D), CONF_KERNEL ** -0.5),
        "conf_b_dw": nrm((N_CONF_LAYERS, D), 0.02),
        "conf_ln_g": gain((N_CONF_LAYERS, D)),
        "conf_ln_b": nrm((N_CONF_LAYERS, D), 0.02),
        "conf_w_pw2": nrm((N_CONF_LAYERS, D, D), D ** -0.5),
        "conf_b_pw2": nrm((N_CONF_LAYERS, D), 0.02),
        "sc_w_in": nrm((N_SC_LAYERS, D, 3 * D), D ** -0.5),
        "sc_w_conv": nrm((N_SC_LAYERS, SHORT_KERNEL, D), SHORT_KERNEL ** -0.5),
        "sc_w_out": nrm((N_SC_LAYERS, D, D), D ** -0.5),
        "nsa_w_in": nrm((N_NSA_LAYERS, D, NSA_IN_WIDTH), D ** -0.5),
        "nsa_k_pe": nrm((N_NSA_LAYERS, CMP_LEN, HEAD_DIM), 0.2),
        "nsa_k_w1": nrm((N_NSA_LAYERS, CMP_LEN, HEAD_DIM, CMP_HIDDEN), (CMP_LEN * HEAD_DIM) ** -0.5),
        "nsa_k_w2": nrm((N_NSA_LAYERS, CMP_HIDDEN, HEAD_DIM), CMP_HIDDEN ** -0.5),
        "nsa_v_pe": nrm((N_NSA_LAYERS, CMP_LEN, HEAD_DIM), 0.2),
        "nsa_v_w1": nrm((N_NSA_LAYERS, CMP_LEN, HEAD_DIM, CMP_HIDDEN), (CMP_LEN * HEAD_DIM) ** -0.5),
        "nsa_v_w2": nrm((N_NSA_LAYERS, CMP_HIDDEN, HEAD_DIM), CMP_HIDDEN ** -0.5),
        "nsa_w_out": nrm((N_NSA_LAYERS, D, D), D ** -0.5),
        "rel_table": nrm((REL_BUCKETS, N_HEADS), 0.5),
        "dense_w_gate": nrm((N_DENSE_LAYERS, D, D_FF_DENSE), D ** -0.5),
        "dense_w_up": nrm((N_DENSE_LAYERS, D, D_FF_DENSE), D ** -0.5),
        "dense_w_down": nrm((N_DENSE_LAYERS, D_FF_DENSE, D), D_FF_DENSE ** -0.5),
        "moe_w_router": nrm((N_MOE_LAYERS, D, N_EXPERTS), D ** -0.5),
        "moe_b_router": nrm((N_MOE_LAYERS, N_EXPERTS), 0.01),
        "moe_w_gate": nrm((N_MOE_LAYERS, N_EXPERTS, D, D_FF_EXPERT), D ** -0.5),
        "moe_w_up": nrm((N_MOE_LAYERS, N_EXPERTS, D, D_FF_EXPERT), D ** -0.5),
        "moe_w_down": nrm((N_MOE_LAYERS, N_EXPERTS, D_FF_EXPERT, D), D_FF_EXPERT ** -0.5),
    }


def reference(x, c, mix_pre_g, mix_post_g, mix_ada_w, mix_ada_b,
              ffn_pre_g, ffn_post_g, ffn_ada_w, ffn_ada_b,
              conf_w_pw1, conf_b_pw1, conf_w_dw, conf_b_dw, conf_ln_g, conf_ln_b,
              conf_w_pw2, conf_b_pw2,
              sc_w_in, sc_w_conv, sc_w_out,
              nsa_w_in, nsa_k_pe, nsa_k_w1, nsa_k_w2, nsa_v_pe, nsa_v_w1, nsa_v_w2, nsa_w_out,
              rel_table,
              dense_w_gate, dense_w_up, dense_w_down,
              moe_w_router, moe_b_router, moe_w_gate, moe_w_up, moe_w_down):
    for i in range(DEPTH):
        shift, scale, gate = adaln_params(c, mix_ada_w[i], mix_ada_b[i])
        h = rms_norm(x, mix_pre_g[i]) * (1 + scale) + shift
        j = i // N_MIXERS
        kind = i % N_MIXERS
        if kind == 0:
            y = conformer_conv_module(h, conf_w_pw1[j], conf_b_pw1[j], conf_w_dw[j], conf_b_dw[j],
                                      conf_ln_g[j], conf_ln_b[j], conf_w_pw2[j], conf_b_pw2[j])
        elif kind == 1:
            y = short_conv_mixer(h, sc_w_in[j], sc_w_conv[j], sc_w_out[j])
        else:
            y = nsa_attention(h, nsa_w_in[j], nsa_k_pe[j], nsa_k_w1[j], nsa_k_w2[j],
                              nsa_v_pe[j], nsa_v_w1[j], nsa_v_w2[j], rel_table, nsa_w_out[j])
        x = x + gate * rms_norm(y, mix_post_g[i])

        shift, scale, gate = adaln_params(c, ffn_ada_w[i], ffn_ada_b[i])
        h = rms_norm(x, ffn_pre_g[i]) * (1 + scale) + shift
        f = i // 2
        if i % 2 == 0:
            y = swiglu(h, dense_w_gate[f], dense_w_up[f], dense_w_down[f])
        else:
            y = moe_swiglu(h, moe_w_router[f], moe_b_router[f], moe_w_gate[f], moe_w_up[f], moe_w_down[f])
        x = x + gate * rms_norm(y, ffn_post_g[i])
    return x
```

```python
import functools
import math

import numpy as np
import jax
import jax.numpy as jnp
from jax import lax
from jax.experimental import pallas as pl
from jax.experimental.pallas import tpu as pltpu

F32 = jnp.float32
BF16 = jnp.bfloat16
I32 = jnp.int32

D_MODEL = 1024
DEPTH = 4
RMS_EPS = 1e-6
LN_EPS = 1e-5
NEG = -1e30

CONF_KERNEL = 31
SHORT_KERNEL = 3
N_HEADS = 16
HEAD_DIM = 64
N_GROUPS = 4
HPG = 4
CMP_LEN = 32
CMP_STRIDE = 16
CMP_HIDDEN = 128
SEL_BLOCK = 64
SEL_TOPK = 16
N_LOCAL_BLOCKS = 2
WINDOW = 512
REL_BUCKETS = 32
REL_MAX_DIST = 2048
D_FF_DENSE = 2816
N_EXPERTS = 8
D_FF_EXPERT = 3584

LANES = 128
SUBLANES = 8
VMEM_LIMIT = 56 * 1024 * 1024

ADA_TN = 1024
TOK_TILE = 512
CONV_TILE = 256
CONV_HALO = 32
FFN_TILE = 1024
FFN_TF = 256
MOE_ROWS = 1024
ROUTE_TILE = 512
ROW_DMA_TILE = 256
QT = 128
QL = HPG * QT
NEAR_SLOTS = 13
SLOT_FAR = 13
SLOT_WIN = 14
N_SLOTS = 15
CMP_NEAR = 112
CMP_TAB = 224


def _bucket_thresholds():
    n = np.arange(0, 4 * REL_MAX_DIST)
    max_exact = REL_BUCKETS // 2
    nf = np.maximum(n, 1).astype(np.float32)
    large = max_exact + (np.log(nf / np.float32(max_exact))
                         / np.float32(math.log(REL_MAX_DIST / max_exact))
                         * np.float32(REL_BUCKETS - max_exact)).astype(np.int32)
    large = np.minimum(large, REL_BUCKETS - 1)
    b = np.where(n < max_exact, n, large)
    return [int(np.argmax(b >= k)) for k in range(REL_BUCKETS)]


BUCKET_THR = _bucket_thresholds()
assert BUCKET_THR[REL_BUCKETS - 1] <= QT * NEAR_SLOTS - (QT - 1)
assert BUCKET_THR[REL_BUCKETS - 1] <= CMP_STRIDE * (CMP_NEAR - 8 + 1) - CMP_LEN + 1


def _cparams(sem):
    return pltpu.CompilerParams(dimension_semantics=sem, vmem_limit_bytes=VMEM_LIMIT)


def _bdot(a, b):
    return jnp.dot(a.astype(BF16), b.astype(BF16), preferred_element_type=F32)


def _sigmoid(x):
    return 1.0 / (1.0 + jnp.exp(-x))


def _rms(x, g):
    return x * lax.rsqrt(jnp.mean(x * x, axis=-1, keepdims=True) + RMS_EPS) * g


def _norm_mod(x, g, mod):
    return _rms(x, g) * (1.0 + mod[1:2]) + mod[0:1]


def _adaln_kernel(c_ref, wm_ref, bm_ref, wf_ref, bf_ref, om_ref, of_ref):
    c = c_ref[...]
    s = c * _sigmoid(c)
    om_ref[0] = jnp.dot(s, wm_ref[0], precision=lax.Precision.HIGHEST,
                        preferred_element_type=F32) + bm_ref[0]
    of_ref[0] = jnp.dot(s, wf_ref[0], precision=lax.Precision.HIGHEST,
                        preferred_element_type=F32) + bf_ref[0]


def _adaln(c, mix_w, mix_b, ffn_w, ffn_b):
    B, D = c.shape
    depth = mix_w.shape[0]
    cp = jnp.zeros((SUBLANES, D), F32).at[:B].set(c)
    w_spec = pl.BlockSpec((1, D, ADA_TN), lambda l, j: (l, 0, j))
    b_spec = pl.BlockSpec((1, 1, ADA_TN), lambda l, j: (l, 0, j))
    o_spec = pl.BlockSpec((1, SUBLANES, ADA_TN), lambda l, j: (l, 0, j))
    om, of = pl.pallas_call(
        _adaln_kernel,
        grid=(depth, 3 * D // ADA_TN),
        in_specs=[pl.BlockSpec((SUBLANES, D), lambda l, j: (0, 0)), w_spec, b_spec, w_spec, b_spec],
        out_specs=[o_spec, o_spec],
        out_shape=[jax.ShapeDtypeStruct((depth, SUBLANES, 3 * D), F32)] * 2,
        compiler_params=_cparams(("arbitrary", "arbitrary")),
        name="adaln",
    )(cp, mix_w, mix_b.reshape(depth, 1, 3 * D), ffn_w, ffn_b.reshape(depth, 1, 3 * D))
    return (om[:, :B].reshape(depth, B, 3, D), of[:, :B].reshape(depth, B, 3, D))


def _mod_spec(layer):
    return pl.BlockSpec((1, 1, 3, D_MODEL), lambda b, *_: (layer, b, 0, 0))


def _vec_spec(layer, width=D_MODEL):
    return pl.BlockSpec((1, 1, width), lambda *_: (layer, 0, 0))


def _post_residual(x, y, post_g, mod):
    return x + mod[2:3] * _rms(y, post_g)


def _conf_a_kernel(x_ref, mod_ref, g_ref, w_ref, b_ref, u_ref):
    D = D_MODEL
    h = _norm_mod(x_ref[0], g_ref[0], mod_ref[0, 0])
    z = _bdot(h, w_ref[...]) + b_ref[0]
    u_ref[0] = z[:, :D] * _sigmoid(z[:, D:])


def _conf_b_kernel(u_ref, up_ref, x_ref, mod_ref, wdw_ref, bdw_ref, lng_ref, lnb_ref,
                   w2_ref, b2_ref, pg_ref, o_ref, ext_ref, conv_ref):
    T, D, K = CONV_TILE, D_MODEL, CONF_KERNEL
    i = pl.program_id(1)
    keep = (i > 0).astype(F32)
    ext_ref[0:CONV_HALO, :] = up_ref[0] * keep
    ext_ref[CONV_HALO:, :] = u_ref[0]
    rb, cb = 64, 256
    off = CONV_HALO - (K - 1)

    def col_body(c, carry):
        c0 = pl.multiple_of(c * cb, cb)
        w = wdw_ref[0, :, pl.ds(c0, cb)]
        bias = bdw_ref[0, :, pl.ds(c0, cb)]
        for r in range(T // rb):
            acc = jnp.zeros((rb, cb), F32) + bias
            for k in range(K):
                acc = acc + w[k:k + 1, :] * ext_ref[pl.ds(r * rb + off + k, rb), pl.ds(c0, cb)]
            conv_ref[pl.ds(r * rb, rb), pl.ds(c0, cb)] = acc
        return carry

    lax.fori_loop(0, D // cb, col_body, 0)
    v = conv_ref[...]
    mu = jnp.mean(v, axis=-1, keepdims=True)
    var = jnp.mean(jnp.square(v - mu), axis=-1, keepdims=True)
    ln = (v - mu) * lax.rsqrt(var + LN_EPS) * lng_ref[0] + lnb_ref[0]
    act = ln * _sigmoid(ln)
    y = _bdot(act, w2_ref[...]) + b2_ref[0]
    o_ref[0] = _post_residual(x_ref[0], y, pg_ref[0], mod_ref[0, 0])


def _conformer_layer(x, mod, layer, j, p):
    B, S, D = x.shape
    T = TOK_TILE
    u = pl.pallas_call(
        _conf_a_kernel,
        grid=(B, S // T),
        in_specs=[pl.BlockSpec((1, T, D), lambda b, i: (b, i, 0)), _mod_spec(layer),
                  _vec_spec(layer), pl.BlockSpec((D, 2 * D), lambda b, i: (0, 0)),
                  _vec_spec(j, 2 * D)],
        out_specs=pl.BlockSpec((1, T, D), lambda b, i: (b, i, 0)),
        out_shape=jax.ShapeDtypeStruct((B, S, D), F32),
        compiler_params=_cparams(("arbitrary", "arbitrary")),
        name="conf_pw1_glu",
    )(x, mod, p["mix_pre_g"], p["conf_w_pw1"][j].astype(BF16), p["conf_b_pw1"])
    T = CONV_TILE
    hb = T // CONV_HALO
    return pl.pallas_call(
        _conf_b_kernel,
        grid=(B, S // T),
        in_specs=[pl.BlockSpec((1, T, D), lambda b, i: (b, i, 0)),
                  pl.BlockSpec((1, CONV_HALO, D), lambda b, i: (b, jnp.maximum(i * hb - 1, 0), 0)),
                  pl.BlockSpec((1, T, D), lambda b, i: (b, i, 0)), _mod_spec(layer),
                  pl.BlockSpec((1, CONF_KERNEL, D), lambda b, i: (j, 0, 0)),
                  _vec_spec(j), _vec_spec(j), _vec_spec(j),
                  pl.BlockSpec((D, D), lambda b, i: (0, 0)), _vec_spec(j), _vec_spec(layer)],
        out_specs=pl.BlockSpec((1, T, D), lambda b, i: (b, i, 0)),
        out_shape=jax.ShapeDtypeStruct((B, S, D), F32),
        scratch_shapes=[pltpu.VMEM((T + CONV_HALO, D), F32), pltpu.VMEM((T, D), F32)],
        compiler_params=_cparams(("arbitrary", "arbitrary")),
        name="conf_conv_pw2",
    )(u, u, x, mod, p["conf_w_dw"], p["conf_b_dw"], p["conf_ln_g"], p["conf_ln_b"],
      p["conf_w_pw2"][j].astype(BF16), p["conf_b_pw2"], p["mix_post_g"])


def _short_conv_kernel(x_ref, mod_ref, g_ref, win_ref, wc_ref, wout_ref, pg_ref, o_ref, ext_ref):
    T, D = TOK_TILE, D_MODEL
    i = pl.program_id(1)

    @pl.when(i == 0)
    def _():
        ext_ref[0:SUBLANES, :] = jnp.zeros((SUBLANES, D), F32)

    x = x_ref[0]
    mod = mod_ref[0, 0]
    h = _norm_mod(x, g_ref[0], mod)
    z = _bdot(h, win_ref[...])
    v, gate_b, gate_c = z[:, :D], z[:, D:2 * D], z[:, 2 * D:]
    ext_ref[SUBLANES:, :] = gate_c * v
    wc = wc_ref[0]
    u = (wc[0:1] * ext_ref[pl.ds(SUBLANES - 2, T), :] + wc[1:2] * ext_ref[pl.ds(SUBLANES - 1, T), :]
         + wc[2:3] * ext_ref[pl.ds(SUBLANES, T), :])
    ext_ref[0:SUBLANES, :] = ext_ref[pl.ds(T, SUBLANES), :]
    y = _bdot(gate_b * u, wout_ref[...])
    o_ref[0] = _post_residual(x, y, pg_ref[0], mod)


def _short_conv_layer(x, mod, layer, j, p):
    B, S, D = x.shape
    T = TOK_TILE
    return pl.pallas_call(
        _short_conv_kernel,
        grid=(B, S // T),
        in_specs=[pl.BlockSpec((1, T, D), lambda b, i: (b, i, 0)), _mod_spec(layer), _vec_spec(layer),
                  pl.BlockSpec((D, 3 * D), lambda b, i: (0, 0)),
                  pl.BlockSpec((1, SHORT_KERNEL, D), lambda b, i: (j, 0, 0)),
                  pl.BlockSpec((D, D), lambda b, i: (0, 0)), _vec_spec(layer)],
        out_specs=pl.BlockSpec((1, T, D), lambda b, i: (b, i, 0)),
        out_shape=jax.ShapeDtypeStruct((B, S, D), F32),
        scratch_shapes=[pltpu.VMEM((T + SUBLANES, D), F32)],
        compiler_params=_cparams(("arbitrary", "arbitrary")),
        name="short_conv",
    )(x, mod, p["mix_pre_g"], p["sc_w_in"][j].astype(BF16), p["sc_w_conv"],
      p["sc_w_out"][j].astype(BF16), p["mix_post_g"])


def _swiglu_step(h, wg_ref, wu_ref, wd_ref, acc_ref):
    a = _bdot(h, wg_ref[...])
    u = _bdot(h, wu_ref[...])
    acc_ref[...] += _bdot((a * _sigmoid(a)) * u, wd_ref[...])


def _dense_ffn_kernel(x_ref, mod_ref, g_ref, wg_ref, wu_ref, wd_ref, pg_ref, o_ref, h_ref, acc_ref):
    f = pl.program_id(2)

    @pl.when(f == 0)
    def _():
        h_ref[...] = _norm_mod(x_ref[0], g_ref[0], mod_ref[0, 0]).astype(BF16)
        acc_ref[...] = jnp.zeros_like(acc_ref)

    _swiglu_step(h_ref[...], wg_ref.at[0], wu_ref.at[0], wd_ref.at[0], acc_ref)

    @pl.when(f == pl.num_programs(2) - 1)
    def _():
        o_ref[0] = _post_residual(x_ref[0], acc_ref[...], pg_ref[0], mod_ref[0, 0])


def _dense_ffn_layer(x, mod, layer, j, p):
    B, S, D = x.shape
    T, TF, F = min(FFN_TILE, S), FFN_TF, D_FF_DENSE
    return pl.pallas_call(
        _dense_ffn_kernel,
        grid=(B, S // T, F // TF),
        in_specs=[pl.BlockSpec((1, T, D), lambda b, i, f: (b, i, 0)), _mod_spec(layer), _vec_spec(layer),
                  pl.BlockSpec((1, D, TF), lambda b, i, f: (j, 0, f)),
                  pl.BlockSpec((1, D, TF), lambda b, i, f: (j, 0, f)),
                  pl.BlockSpec((1, TF, D), lambda b, i, f: (j, f, 0)), _vec_spec(layer)],
        out_specs=pl.BlockSpec((1, T, D), lambda b, i, f: (b, i, 0)),
        out_shape=jax.ShapeDtypeStruct((B, S, D), F32),
        scratch_shapes=[pltpu.VMEM((T, D), BF16), pltpu.VMEM((T, D), F32)],
        compiler_params=_cparams(("arbitrary", "arbitrary", "arbitrary")),
        name="dense_swiglu",
    )(x, mod, p["ffn_pre_g"], p["dense_w_gate"], p["dense_w_up"], p["dense_w_down"], p["ffn_post_g"])


def _router_kernel(x_ref, mod_ref, g_ref, wr_ref, br_ref, h_ref, ri_ref, rw_ref, cnt_ref, base_ref):
    T = ROUTE_TILE
    i = pl.program_id(0)

    @pl.when(i == 0)
    def _():
        base_ref[...] = jnp.zeros_like(base_ref)

    h = _norm_mod(x_ref[...], g_ref[0], mod_ref[0, 0])
    h_ref[...] = h
    logits = jnp.dot(h, wr_ref[...], precision=lax.Precision.HIGHEST,
                     preferred_element_type=F32) + br_ref[0]
    lane = lax.broadcasted_iota(I32, (T, LANES), 1).astype(F32)
    logits = jnp.where(lane < N_EXPERTS, logits, -jnp.inf)
    l0 = jnp.max(logits, axis=1, keepdims=True)
    e0 = jnp.min(jnp.where(logits == l0, lane, float(LANES)), axis=1, keepdims=True)
    rest = jnp.where(lane == e0, -jnp.inf, logits)
    l1 = jnp.max(rest, axis=1, keepdims=True)
    e1 = jnp.min(jnp.where(rest == l1, lane, float(LANES)), axis=1, keepdims=True)
    ex = jnp.exp(l1 - l0)
    w0 = 1.0 / (1.0 + ex)
    w1 = ex / (1.0 + ex)
    onehot = ((lane == e0) | (lane == e1)).astype(F32)
    row = lax.broadcasted_iota(I32, (T, T), 0)
    col = lax.broadcasted_iota(I32, (T, T), 1)
    tri = jnp.where(row > col, 1.0, 0.0).astype(BF16)
    before = jnp.dot(tri, onehot.astype(BF16), preferred_element_type=F32) + base_ref[...]
    r0 = jnp.sum(jnp.where(lane == e0, before, 0.0), axis=1, keepdims=True)
    r1 = jnp.sum(jnp.where(lane == e1, before, 0.0), axis=1, keepdims=True)
    base_ref[...] += jnp.sum(onehot, axis=0, keepdims=True)
    cnt_ref[...] = base_ref[...].astype(I32)
    ri = jnp.where(lane == 0, e0, jnp.where(lane == 1, e1, jnp.where(lane == 2, r0, r1)))
    ri_ref[...] = ri[:, :SUBLANES].astype(I32)
    rw_ref[...] = jnp.where(lane == 0, w0, w1)[:, :SUBLANES]


def _dispatch_kernel(pstart_ref, h_ref, ri_ref, zero_ref, buf_ref, sem):
    del zero_ref
    T = ROW_DMA_TILE

    def row_copy(t, k, dest):
        return pltpu.make_async_copy(h_ref.at[pl.ds(t, 1)], buf_ref.at[pl.ds(dest, 1)], sem)

    def issue(t, carry):
        for k in range(2):
            dest = pstart_ref[ri_ref[t, k]] + ri_ref[t, 2 + k]
            row_copy(t, k, dest).start()
        return carry

    lax.fori_loop(0, T, issue, 0)

    def drain(t, carry):
        for k in range(2):
            row_copy(0, k, 0).wait()
        return carry

    lax.fori_loop(0, T, drain, 0)


def _expert_kernel(blk_e_ref, nblk_ref, xb_ref, wg_ref, wu_ref, wd_ref, o_ref, h_ref, acc_ref):
    del blk_e_ref
    i = pl.program_id(0)
    f = pl.program_id(1)
    live = i < nblk_ref[0]

    @pl.when(f == 0)
    def _():
        h_ref[...] = xb_ref[...].astype(BF16)
        acc_ref[...] = jnp.zeros_like(acc_ref)

    @pl.when(live)
    def _():
        _swiglu_step(h_ref[...], wg_ref.at[0, 0], wu_ref.at[0, 0], wd_ref.at[0, 0], acc_ref)

    @pl.when(f == pl.num_programs(1) - 1)
    def _():
        o_ref[...] = acc_ref[...]


def _combine_kernel(pstart_ref, x_ref, mod_ref, pg_ref, ri_ref, rw_ref, yb_ref, o_ref, y0_ref, y1_ref, sem):
    T = ROW_DMA_TILE
    ys = (y0_ref, y1_ref)

    def row_copy(t, k, src):
        return pltpu.make_async_copy(yb_ref.at[pl.ds(src, 1)], ys[k].at[pl.ds(t, 1)], sem)

    def issue(t, carry):
        for k in range(2):
            src = pstart_ref[ri_ref[t, k]] + ri_ref[t, 2 + k]
            row_copy(t, k, src).start()
        return carry

    lax.fori_loop(0, T, issue, 0)

    def drain(t, carry):
        for k in range(2):
            row_copy(0, k, 0).wait()
        return carry

    lax.fori_loop(0, T, drain, 0)
    rw = rw_ref[...]
    y = rw[:, 0:1] * y0_ref[...] + rw[:, 1:2] * y1_ref[...]
    o_ref[...] = _post_residual(x_ref[...], y, pg_ref[0], mod_ref[0, 0])


def _moe_ffn_layer(x, mod, layer, j, p):
    B, S, D = x.shape
    N = B * S
    T = ROUTE_TILE
    tiles_per_seq = S // T
    xt = x.reshape(N, D)
    wr = jnp.zeros((D, LANES), F32).at[:, :N_EXPERTS].set(p["moe_w_router"][j])
    br = jnp.zeros((1, 1, LANES), F32).at[0, 0, :N_EXPERTS].set(p["moe_b_router"][j])
    h, ri, rw, cnt = pl.pallas_call(
        _router_kernel,
        grid=(N // T,),
        in_specs=[pl.BlockSpec((T, D), lambda i: (i, 0)),
                  pl.BlockSpec((1, 1, 3, D), lambda i: (layer, i // tiles_per_seq, 0, 0)),
                  _vec_spec(layer), pl.BlockSpec((D, LANES), lambda i: (0, 0)), _vec_spec(0, LANES)],
        out_specs=[pl.BlockSpec((T, D), lambda i: (i, 0)), pl.BlockSpec((T, SUBLANES), lambda i: (i, 0)),
                   pl.BlockSpec((T, SUBLANES), lambda i: (i, 0)), pl.BlockSpec((1, LANES), lambda i: (0, 0))],
        out_shape=[jax.ShapeDtypeStruct((N, D), F32), jax.ShapeDtypeStruct((N, SUBLANES), I32),
                   jax.ShapeDtypeStruct((N, SUBLANES), F32), jax.ShapeDtypeStruct((1, LANES), I32)],
        scratch_shapes=[pltpu.VMEM((1, LANES), F32)],
        compiler_params=_cparams(("arbitrary",)),
        name="moe_router",
    )(xt, mod, p["ffn_pre_g"], wr, br)

    R = MOE_ROWS
    counts = cnt[0, :N_EXPERTS]
    padded = (counts + R - 1) // R * R
    pend = jnp.cumsum(padded)
    pstart = (pend - padded).astype(I32)
    n_blocks = -(-(2 * N + N_EXPERTS * (R - 1)) // R)
    blk_e = jnp.minimum(jnp.sum((jnp.arange(n_blocks) * R)[:, None] >= pend[None, :], axis=1),
                        N_EXPERTS - 1).astype(I32)
    n_live = (pend[-1] // R).astype(I32).reshape(1)
    rows = n_blocks * R

    TD = ROW_DMA_TILE
    buf = pl.pallas_call(
        _dispatch_kernel,
        grid_spec=pltpu.PrefetchScalarGridSpec(
            num_scalar_prefetch=1,
            grid=(N // TD,),
            in_specs=[pl.BlockSpec((TD, D), lambda i, ps: (i, 0)),
                      pl.BlockSpec((TD, SUBLANES), lambda i, ps: (i, 0), memory_space=pltpu.SMEM),
                      pl.BlockSpec(memory_space=pl.ANY)],
            out_specs=pl.BlockSpec(memory_space=pl.ANY),
            scratch_shapes=[pltpu.SemaphoreType.DMA],
        ),
        out_shape=jax.ShapeDtypeStruct((rows, D), F32),
        input_output_aliases={3: 0},
        compiler_params=_cparams(("arbitrary",)),
        name="moe_dispatch",
    )(pstart, h, ri, jnp.zeros((rows, D), F32))

    TF, F = FFN_TF, D_FF_EXPERT
    yb = pl.pallas_call(
        _expert_kernel,
        grid_spec=pltpu.PrefetchScalarGridSpec(
            num_scalar_prefetch=2,
            grid=(n_blocks, F // TF),
            in_specs=[pl.BlockSpec((R, D), lambda i, f, be, nb: (i, 0)),
                      pl.BlockSpec((1, 1, D, TF), lambda i, f, be, nb: (j, be[i], 0, f)),
                      pl.BlockSpec((1, 1, D, TF), lambda i, f, be, nb: (j, be[i], 0, f)),
                      pl.BlockSpec((1, 1, TF, D), lambda i, f, be, nb: (j, be[i], f, 0))],
            out_specs=pl.BlockSpec((R, D), lambda i, f, be, nb: (i, 0)),
            scratch_shapes=[pltpu.VMEM((R, D), BF16), pltpu.VMEM((R, D), F32)],
        ),
        out_shape=jax.ShapeDtypeStruct((rows, D), F32),
        compiler_params=_cparams(("arbitrary", "arbitrary")),
        name="moe_experts",
    )(blk_e, n_live, buf, p["moe_w_gate"], p["moe_w_up"], p["moe_w_down"])

    tiles_per_seq_d = S // TD
    out = pl.pallas_call(
        _combine_kernel,
        grid_spec=pltpu.PrefetchScalarGridSpec(
            num_scalar_prefetch=1,
            grid=(N // TD,),
            in_specs=[pl.BlockSpec((TD, D), lambda i, ps: (i, 0)),
                      pl.BlockSpec((1, 1, 3, D), lambda i, ps: (layer, i // tiles_per_seq_d, 0, 0)),
                      pl.BlockSpec((1, 1, D), lambda i, ps: (layer, 0, 0)),
                      pl.BlockSpec((TD, SUBLANES), lambda i, ps: (i, 0), memory_space=pltpu.SMEM),
                      pl.BlockSpec((TD, SUBLANES), lambda i, ps: (i, 0)),
                      pl.BlockSpec(memory_space=pl.ANY)],
            out_specs=pl.BlockSpec((TD, D), lambda i, ps: (i, 0)),
            scratch_shapes=[pltpu.VMEM((TD, D), F32), pltpu.VMEM((TD, D), F32), pltpu.SemaphoreType.DMA],
        ),
        out_shape=jax.ShapeDtypeStruct((N, D), F32),
        compiler_params=_cparams(("arbitrary",)),
        name="moe_combine",
    )(pstart, xt, mod, p["ffn_post_g"], ri, rw, yb)
    return out.reshape(B, S, D)


NSA_W = D_MODEL + 6 * N_GROUPS * HEAD_DIM + LANES


def _nsa_in_kernel(x_ref, mod_ref, g_ref, w_ref, q_ref, ks_ref, vst_ref, kw_ref, vwt_ref,
                   kc_ref, vc_ref, gt_ref):
    D, G, DH = D_MODEL, N_GROUPS, HEAD_DIM
    KV = G * DH
    h = _norm_mod(x_ref[0], g_ref[0], mod_ref[0, 0])
    z = _bdot(h, w_ref[...])
    q_ref[0] = z[:, :D].astype(BF16)
    o = D
    k_s, v_s = z[:, o:o + KV], z[:, o + KV:o + 2 * KV]
    k_w, v_w = z[:, o + 2 * KV:o + 3 * KV], z[:, o + 3 * KV:o + 4 * KV]
    k_c, v_c = z[:, o + 4 * KV:o + 5 * KV], z[:, o + 5 * KV:o + 6 * KV]
    gates = _sigmoid(z[:, o + 6 * KV:])
    vst = v_s.T
    vwt = v_w.T
    for g in range(G):
        sl = slice(g * DH, (g + 1) * DH)
        ks_ref[0, g] = k_s[:, sl].astype(BF16)
        kw_ref[0, g] = k_w[:, sl].astype(BF16)
        kc_ref[0, g] = k_c[:, sl]
        vc_ref[0, g] = v_c[:, sl]
        vst_ref[0, g] = vst[sl, :].astype(BF16)
        vwt_ref[0, g] = vwt[sl, :].astype(BF16)
    gt_ref[0] = gates.T[:3 * N_HEADS, :]


def _gelu_tanh(x):
    return x * (0.5 * (1.0 + jnp.tanh(math.sqrt(2.0 / math.pi) * (x + 0.044715 * (x * x * x)))))


def _compress_kernel(kc_ref, vc_ref, kpe_ref, kw1_ref, kw2_ref, vpe_ref, vw1_ref, vw2_ref,
                     kco_ref, vcto_ref):
    nc = kc_ref.shape[2]

    def comp(c, pe_ref, w1_ref, w2_ref):
        a = _bdot(c + pe_ref[0:1], w1_ref[0])
        b = _bdot(c + pe_ref[1:2], w1_ref[1])
        b_next = pltpu.roll(b, nc - 1, axis=0)
        out = _bdot(_gelu_tanh(a + b_next), w2_ref[...])
        row = lax.broadcasted_iota(I32, out.shape, 0)
        return jnp.where(row < nc - 1, out, 0.0)

    kco_ref[0, 0] = comp(kc_ref[0, 0], kpe_ref, kw1_ref, kw2_ref).astype(BF16)
    vcto_ref[0, 0] = comp(vc_ref[0, 0], vpe_ref, vw1_ref, vw2_ref).T.astype(BF16)


def _bias_table_kernel(tbl_ref, tab_ref, ctab_ref, far_ref):
    g = pl.program_id(0)

    def bias_of(dist, h):
        v = jnp.full(dist.shape, tbl_ref[0, h], F32)
        for b in range(1, REL_BUCKETS):
            v = jnp.where(dist >= BUCKET_THR[b], tbl_ref[b, h], v)
        return jnp.where(dist >= 0, v, NEG)

    key = lax.broadcasted_iota(I32, (QT, QT), 0)
    qry = lax.broadcasted_iota(I32, (QT, QT), 1)
    crow = lax.broadcasted_iota(I32, (CMP_TAB, QT), 0)
    cqry = lax.broadcasted_iota(I32, (CMP_TAB, QT), 1)
    for hh in range(HPG):
        h = g * HPG + hh
        lanes = slice(hh * QT, (hh + 1) * QT)
        def near_slot(slot, carry, h=h, lanes=lanes):
            tab_ref[0, slot, :, lanes] = bias_of(slot * QT + qry - key, h)
            return carry

        lax.fori_loop(0, NEAR_SLOTS, near_slot, 0)
        far = jnp.full((QT, QT), tbl_ref[REL_BUCKETS - 1, h], F32)
        tab_ref[0, SLOT_FAR, :, lanes] = far
        far_ref[0, :, lanes] = far[0:1]
        dw = WINDOW + qry - key
        tab_ref[0, SLOT_WIN, :, lanes] = jnp.where(dw < WINDOW, bias_of(dw, h), NEG)
        dc = cqry - CMP_STRIDE * (crow - (CMP_NEAR - SUBLANES)) - (CMP_LEN - 1)
        ctab_ref[0, :, lanes] = jnp.where(crow < CMP_NEAR, bias_of(dc, h), NEG)


def _nsa_attn_kernel(q_ref, ks_ref, vst_ref, kw_refs, vwt_refs, kc_ref, vct_ref, gt_ref,
                     tab_ref, ctab_ref, far_ref, o_ref, qt_ref, sc_ref, psum_ref, pen_ref):
    DH = HEAD_DIM
    a = pl.program_id(2)
    nc = kc_ref.shape[2]
    n_sel = pen_ref.shape[0]

    qT = q_ref[0].astype(F32).T
    for hh in range(HPG):
        qt_ref[:, hh * QT:(hh + 1) * QT] = qT[hh * DH:(hh + 1) * DH, :].astype(BF16)
    qt = qt_ref[...]

    far = far_ref[0]
    n_hi = (QT // CMP_STRIDE) * (a + 1)
    nb = n_hi - CMP_NEAR
    nb_c = pl.multiple_of(jnp.maximum(nb, 0), SUBLANES)
    r0 = pl.multiple_of(nb_c - nb, SUBLANES)
    row = lax.broadcasted_iota(I32, (nc, QL), 0)
    raw = jnp.dot(kc_ref[0, 0], qt, preferred_element_type=F32)
    sc_ref[...] = raw + jnp.where(row < nb_c, far, NEG)
    near = jnp.dot(kc_ref[0, 0, pl.ds(nb_c, CMP_NEAR), :], qt, preferred_element_type=F32)
    sc_ref[pl.ds(nb_c, CMP_NEAR), :] = near + ctab_ref[0, pl.ds(r0, CMP_NEAR), :]
    s = sc_ref[...]
    m = jnp.max(s, axis=0, keepdims=True)
    e = jnp.exp(s - m)
    l = jnp.sum(e, axis=0, keepdims=True)
    pn = e * jnp.where(m > 0.1 * NEG, 1.0 / l, 0.0)
    o_c = jnp.dot(vct_ref[0, 0], pn.astype(BF16), preferred_element_type=F32)
    psum_ref[...] = pn[:, 0:QT] + pn[:, QT:2 * QT] + pn[:, 2 * QT:3 * QT] + pn[:, 3 * QT:4 * QT]

    ratio = SEL_BLOCK // CMP_STRIDE
    imp = psum_ref[pl.ds(0, n_sel, stride=ratio), :]
    for jj in range(1, ratio):
        imp = imp + psum_ref[pl.ds(jj, n_sel, stride=ratio), :]
    last = psum_ref[pl.ds(ratio - 1, n_sel, stride=ratio), :]
    srow = lax.broadcasted_iota(I32, (n_sel, QT), 0)
    imp = imp + jnp.where(srow > 0, pltpu.roll(last, 1, axis=0), 0.0)
    lane_q = lax.broadcasted_iota(I32, (n_sel, QT), 1)
    cur = (QT // SEL_BLOCK) * a + lane_q // SEL_BLOCK
    back = cur - srow
    valid = back >= 0
    forced = (srow == 0) | (valid & (back < N_LOCAL_BLOCKS))
    score = jnp.where(forced, 1e30, jnp.where(valid, imp, -1.0))
    chosen = jnp.zeros((n_sel, QT), F32)
    srow_f = srow.astype(F32)
    for _ in range(min(SEL_TOPK, n_sel)):
        mx = jnp.max(score, axis=0, keepdims=True)
        first = jnp.min(jnp.where(score == mx, srow_f, float(n_sel)), axis=0, keepdims=True)
        pick = srow_f == first
        chosen = jnp.where(pick, 1.0, chosen)
        score = jnp.where(pick, -jnp.inf, score)
    pen = jnp.where(chosen > 0.0, 0.0, NEG)
    for hh in range(HPG):
        pen_ref[:, hh * QT:(hh + 1) * QT] = pen

    def tile_scores(k_tile, slot):
        return jnp.dot(k_tile, qt, preferred_element_type=F32) + tab_ref[0, slot]

    def first_tile(s, vt_tile):
        m = jnp.max(s, axis=0, keepdims=True)
        p = jnp.exp(s - m)
        return m, jnp.sum(p, axis=0, keepdims=True), jnp.dot(vt_tile, p.astype(BF16), preferred_element_type=F32)

    def next_tile(state, s, vt_tile):
        m, l, acc = state
        m_new = jnp.maximum(m, jnp.max(s, axis=0, keepdims=True))
        alpha = jnp.exp(m - m_new)
        p = jnp.exp(s - m_new)
        return (m_new, alpha * l + jnp.sum(p, axis=0, keepdims=True),
                alpha * acc + jnp.dot(vt_tile, p.astype(BF16), preferred_element_type=F32))

    def sel_pen(c):
        halves = [jnp.broadcast_to(pen_ref[pl.ds((QT // SEL_BLOCK) * c + u, 1), :], (SEL_BLOCK, QL))
                  for u in range(QT // SEL_BLOCK)]
        return jnp.concatenate(halves, axis=0)

    c0 = pl.multiple_of(a * QT, QT)
    state = first_tile(tile_scores(ks_ref[0, 0, pl.ds(c0, QT), :], 0) + sel_pen(a),
                       vst_ref[0, 0, :, pl.ds(c0, QT)])

    def sel_body(c, state):
        k0 = pl.multiple_of(c * QT, QT)
        slot = jnp.minimum(a - c, SLOT_FAR)
        s = tile_scores(ks_ref[0, 0, pl.ds(k0, QT), :], slot) + sel_pen(c)
        return next_tile(state, s, vst_ref[0, 0, :, pl.ds(k0, QT)])

    m_s, l_s, acc_s = lax.fori_loop(0, a, sel_body, state)
    o_s = acc_s * (1.0 / l_s)

    state = first_tile(tile_scores(kw_refs[0][0, 0], 0), vwt_refs[0][0, 0])
    for d in range(1, WINDOW // QT + 1):
        slot = SLOT_WIN if d == WINDOW // QT else d
        s = tile_scores(kw_refs[d][0, 0], slot) + jnp.where(a >= d, 0.0, NEG)
        state = next_tile(state, s, vwt_refs[d][0, 0])
    m_w, l_w, acc_w = state
    o_w = acc_w * (1.0 / l_w)

    def gate(branch):
        return jnp.concatenate([gt_ref[0, 0, 3 * hh + branch:3 * hh + branch + 1, :] for hh in range(HPG)],
                               axis=1)

    oT = gate(0) * o_c + gate(1) * o_s + gate(2) * o_w
    stacked = jnp.concatenate([oT[:, hh * QT:(hh + 1) * QT] for hh in range(HPG)], axis=0)
    o_ref[0] = stacked.T.astype(BF16)


def _nsa_attn_wrapper(*refs):
    nw = WINDOW // QT + 1
    q_ref, ks_ref, vst_ref = refs[0:3]
    kw_refs = refs[3:3 + nw]
    vwt_refs = refs[3 + nw:3 + 2 * nw]
    rest = refs[3 + 2 * nw:]
    _nsa_attn_kernel(q_ref, ks_ref, vst_ref, kw_refs, vwt_refs, *rest)


def _out_proj_kernel(o_ref, x_ref, mod_ref, w_ref, pg_ref, out_ref):
    y = jnp.dot(o_ref[0], w_ref[...], preferred_element_type=F32)
    out_ref[0] = _post_residual(x_ref[0], y, pg_ref[0], mod_ref[0, 0])


def _nsa_layer(x, mod, layer, j, p):
    B, S, D = x.shape
    G, DH, KV = N_GROUPS, HEAD_DIM, N_GROUPS * HEAD_DIM
    T = TOK_TILE
    NC = S // CMP_STRIDE
    n_sel = S // SEL_BLOCK

    w_in = p["nsa_w_in"][j]
    cuts = [D + i * KV for i in range(7)]
    wq, wkc, wvc, wks, wvs, wkw, wvw, wg = jnp.split(w_in, cuts, axis=-1)
    wg = jnp.zeros((D, LANES), F32).at[:, :3 * N_HEADS].set(wg)
    w_all = jnp.concatenate([wq * (HEAD_DIM ** -0.5), wks, wvs, wkw, wvw, wkc, wvc, wg], axis=1).astype(BF16)

    kv_spec = pl.BlockSpec((1, G, T, DH), lambda b, i: (b, 0, i, 0))
    kvt_spec = pl.BlockSpec((1, G, DH, T), lambda b, i: (b, 0, 0, i))
    q, ks, vst, kw, vwt, kc_raw, vc_raw, gt = pl.pallas_call(
        _nsa_in_kernel,
        grid=(B, S // T),
        in_specs=[pl.BlockSpec((1, T, D), lambda b, i: (b, i, 0)), _mod_spec(layer), _vec_spec(layer),
                  pl.BlockSpec((D, NSA_W), lambda b, i: (0, 0))],
        out_specs=[pl.BlockSpec((1, T, D), lambda b, i: (b, i, 0)), kv_spec, kvt_spec, kv_spec, kvt_spec,
                   kv_spec, kv_spec, pl.BlockSpec((1, 3 * N_HEADS, T), lambda b, i: (b, 0, i))],
        out_shape=[jax.ShapeDtypeStruct((B, S, D), BF16),
                   jax.ShapeDtypeStruct((B, G, S, DH), BF16), jax.ShapeDtypeStruct((B, G, DH, S), BF16),
                   jax.ShapeDtypeStruct((B, G, S, DH), BF16), jax.ShapeDtypeStruct((B, G, DH, S), BF16),
                   jax.ShapeDtypeStruct((B, G, S, DH), F32), jax.ShapeDtypeStruct((B, G, S, DH), F32),
                   jax.ShapeDtypeStruct((B, 3 * N_HEADS, S), F32)],
        compiler_params=_cparams(("arbitrary", "arbitrary")),
        name="nsa_in_proj",
    )(x, mod, p["mix_pre_g"], w_all)

    half = CMP_LEN // 2
    chunk_w = half * DH

    def split_w1(w1):
        return w1.reshape(2, chunk_w, CMP_HIDDEN).astype(BF16)

    def split_pe(pe):
        return pe.reshape(2, chunk_w)

    c_spec = pl.BlockSpec((1, 1, NC, chunk_w), lambda b, g: (b, g, 0, 0))
    full = lambda shape: pl.BlockSpec(shape, lambda b, g: (0,) * len(shape))
    kc, vct = pl.pallas_call(
        _compress_kernel,
        grid=(B, G),
        in_specs=[c_spec, c_spec,
                  full((2, chunk_w)), full((2, chunk_w, CMP_HIDDEN)), full((CMP_HIDDEN, DH)),
                  full((2, chunk_w)), full((2, chunk_w, CMP_HIDDEN)), full((CMP_HIDDEN, DH))],
        out_specs=[pl.BlockSpec((1, 1, NC, DH), lambda b, g: (b, g, 0, 0)),
                   pl.BlockSpec((1, 1, DH, NC), lambda b, g: (b, g, 0, 0))],
        out_shape=[jax.ShapeDtypeStruct((B, G, NC, DH), BF16), jax.ShapeDtypeStruct((B, G, DH, NC), BF16)],
        compiler_params=_cparams(("arbitrary", "arbitrary")),
        name="nsa_compress",
    )(kc_raw.reshape(B, G, NC, chunk_w), vc_raw.reshape(B, G, NC, chunk_w),
      split_pe(p["nsa_k_pe"][j]), split_w1(p["nsa_k_w1"][j]), p["nsa_k_w2"][j].astype(BF16),
      split_pe(p["nsa_v_pe"][j]), split_w1(p["nsa_v_w1"][j]), p["nsa_v_w2"][j].astype(BF16))

    tab, ctab, far = pl.pallas_call(
        _bias_table_kernel,
        grid=(G,),
        in_specs=[pl.BlockSpec(memory_space=pltpu.SMEM)],
        out_specs=[pl.BlockSpec((1, N_SLOTS, QT, QL), lambda g: (g, 0, 0, 0)),
                   pl.BlockSpec((1, CMP_TAB, QL), lambda g: (g, 0, 0)),
                   pl.BlockSpec((1, 1, QL), lambda g: (g, 0, 0))],
        out_shape=[jax.ShapeDtypeStruct((G, N_SLOTS, QT, QL), F32),
                   jax.ShapeDtypeStruct((G, CMP_TAB, QL), F32),
                   jax.ShapeDtypeStruct((G, 1, QL), F32)],
        compiler_params=_cparams(("arbitrary",)),
        name="nsa_bias_tables",
    )(p["rel_table"])

    nw = WINDOW // QT + 1
    kw_specs = [pl.BlockSpec((1, 1, QT, DH), functools.partial(
        lambda b, g, a, d: (b, g, jnp.maximum(a - d, 0), 0), d=d)) for d in range(nw)]
    vwt_specs = [pl.BlockSpec((1, 1, DH, QT), functools.partial(
        lambda b, g, a, d: (b, g, 0, jnp.maximum(a - d, 0)), d=d)) for d in range(nw)]
    o = pl.pallas_call(
        _nsa_attn_wrapper,
        grid=(B, G, S // QT),
        in_specs=[pl.BlockSpec((1, QT, HPG * DH), lambda b, g, a: (b, a, g)),
                  pl.BlockSpec((1, 1, S, DH), lambda b, g, a: (b, g, 0, 0)),
                  pl.BlockSpec((1, 1, DH, S), lambda b, g, a: (b, g, 0, 0)),
                  *kw_specs, *vwt_specs,
                  pl.BlockSpec((1, 1, NC, DH), lambda b, g, a: (b, g, 0, 0)),
                  pl.BlockSpec((1, 1, DH, NC), lambda b, g, a: (b, g, 0, 0)),
                  pl.BlockSpec((1, 1, 3 * HPG, QT), lambda b, g, a: (b, g, 0, a)),
                  pl.BlockSpec((1, N_SLOTS, QT, QL), lambda b, g, a: (g, 0, 0, 0)),
                  pl.BlockSpec((1, CMP_TAB, QL), lambda b, g, a: (g, 0, 0)),
                  pl.BlockSpec((1, 1, QL), lambda b, g, a: (g, 0, 0))],
        out_specs=pl.BlockSpec((1, QT, HPG * DH), lambda b, g, a: (b, a, g)),
        out_shape=jax.ShapeDtypeStruct((B, S, D), BF16),
        scratch_shapes=[pltpu.VMEM((DH, QL), BF16), pltpu.VMEM((NC, QL), F32),
                        pltpu.VMEM((NC, QT), F32), pltpu.VMEM((n_sel, QL), F32)],
        compiler_params=_cparams(("arbitrary", "arbitrary", "arbitrary")),
        name="nsa_attention",
    )(q, ks, vst, *([kw] * nw), *([vwt] * nw), kc, vct, gt.reshape(B, G, 3 * HPG, S), tab, ctab, far)

    return pl.pallas_call(
        _out_proj_kernel,
        grid=(B, S // T),
        in_specs=[pl.BlockSpec((1, T, D), lambda b, i: (b, i, 0)),
                  pl.BlockSpec((1, T, D), lambda b, i: (b, i, 0)), _mod_spec(layer),
                  pl.BlockSpec((D, D), lambda b, i: (0, 0)), _vec_spec(layer)],
        out_specs=pl.BlockSpec((1, T, D), lambda b, i: (b, i, 0)),
        out_shape=jax.ShapeDtypeStruct((B, S, D), F32),
        compiler_params=_cparams(("arbitrary", "arbitrary")),
        name="nsa_out_proj",
    )(o, x, mod, p["nsa_w_out"][j].astype(BF16), p["mix_post_g"])


def _as_rows(v):
    return v.reshape(v.shape[0], 1, v.shape[-1])


def kernel(x, c, mix_pre_g, mix_post_g, mix_ada_w, mix_ada_b, ffn_pre_g, ffn_post_g, ffn_ada_w, ffn_ada_b, conf_w_pw1, conf_b_pw1, conf_w_dw, conf_b_dw, conf_ln_g, conf_ln_b, conf_w_pw2, conf_b_pw2, sc_w_in, sc_w_conv, sc_w_out, nsa_w_in, nsa_k_pe, nsa_k_w1, nsa_k_w2, nsa_v_pe, nsa_v_w1, nsa_v_w2, nsa_w_out, rel_table, dense_w_gate, dense_w_up, dense_w_down, moe_w_router, moe_b_router, moe_w_gate, moe_w_up, moe_w_down):
    p = dict(
        mix_pre_g=_as_rows(mix_pre_g), mix_post_g=_as_rows(mix_post_g),
        ffn_pre_g=_as_rows(ffn_pre_g), ffn_post_g=_as_rows(ffn_post_g),
        conf_w_pw1=conf_w_pw1, conf_b_pw1=_as_rows(conf_b_pw1), conf_w_dw=conf_w_dw,
        conf_b_dw=_as_rows(conf_b_dw), conf_ln_g=_as_rows(conf_ln_g), conf_ln_b=_as_rows(conf_ln_b),
        conf_w_pw2=conf_w_pw2, conf_b_pw2=_as_rows(conf_b_pw2),
        sc_w_in=sc_w_in, sc_w_conv=sc_w_conv, sc_w_out=sc_w_out,
        nsa_w_in=nsa_w_in, nsa_k_pe=nsa_k_pe, nsa_k_w1=nsa_k_w1, nsa_k_w2=nsa_k_w2,
        nsa_v_pe=nsa_v_pe, nsa_v_w1=nsa_v_w1, nsa_v_w2=nsa_v_w2, nsa_w_out=nsa_w_out,
        rel_table=rel_table,
        dense_w_gate=dense_w_gate, dense_w_up=dense_w_up, dense_w_down=dense_w_down,
        moe_w_router=moe_w_router, moe_b_router=moe_b_router,
        moe_w_gate=moe_w_gate, moe_w_up=moe_w_up, moe_w_down=moe_w_down,
    )
    mix_mod, ffn_mod = _adaln(c, mix_ada_w, mix_ada_b, ffn_ada_w, ffn_ada_b)
    mixers = (_conformer_layer, _short_conv_layer, _nsa_layer)
    for i in range(DEPTH):
        x = mixers[i % 3](x, mix_mod, i, i // 3, p)
        ffn = _dense_ffn_layer if i % 2 == 0 else _moe_ffn_layer
        x = ffn(x, ffn_mod, i, i // 2, p)
    return x
```

```python
import functools
import math

import numpy as np
import jax
import jax.numpy as jnp
from jax import lax
from jax.experimental import pallas as pl
from jax.experimental.pallas import tpu as pltpu

F32 = jnp.float32
BF16 = jnp.bfloat16
I32 = jnp.int32

D_MODEL = 1024
DEPTH = 4
RMS_EPS = 1e-6
LN_EPS = 1e-5
NEG = -1e30

CONF_KERNEL = 31
SHORT_KERNEL = 3
N_HEADS = 16
HEAD_DIM = 64
N_GROUPS = 4
HPG = 4
CMP_LEN = 32
CMP_STRIDE = 16
CMP_HIDDEN = 128
SEL_BLOCK = 64
SEL_TOPK = 16
N_LOCAL_BLOCKS = 2
WINDOW = 512
REL_BUCKETS = 32
REL_MAX_DIST = 2048
D_FF_DENSE = 2816
N_EXPERTS = 8
D_FF_EXPERT = 3584

LANES = 128
SUBLANES = 8
VMEM_LIMIT = 56 * 1024 * 1024

ADA_TN = 1024
TOK_TILE = 512
CONV_TILE = 256
CONV_HALO = 32
FFN_TILE = 1024
FFN_TF = 256
MOE_ROWS = 1024
ROUTE_TILE = 512
ROW_DMA_TILE = 256
QT = 128
QL = HPG * QT
NEAR_SLOTS = 13
SLOT_FAR = 13
SLOT_WIN = 14
SLOT_NONE = 15
N_SLOTS = 16
CMP_NEAR = 112
SEL_SUB = 4
SEL_STREAMS = 2
SEL_PEN_ROWS = SEL_STREAMS * SEL_SUB * QT // SEL_BLOCK
K_ONE = HEAD_DIM
K_PEN = 80
V_ROWS = 80
LOG2E = 1.4426950408889634


def _bucket_thresholds():
    n = np.arange(0, 4 * REL_MAX_DIST)
    max_exact = REL_BUCKETS // 2
    nf = np.maximum(n, 1).astype(np.float32)
    large = max_exact + (np.log(nf / np.float32(max_exact))
                         / np.float32(math.log(REL_MAX_DIST / max_exact))
                         * np.float32(REL_BUCKETS - max_exact)).astype(np.int32)
    large = np.minimum(large, REL_BUCKETS - 1)
    b = np.where(n < max_exact, n, large)
    return [int(np.argmax(b >= k)) for k in range(REL_BUCKETS)]


BUCKET_THR = _bucket_thresholds()
assert BUCKET_THR[REL_BUCKETS - 1] <= QT * NEAR_SLOTS - (QT - 1)
assert BUCKET_THR[REL_BUCKETS - 1] <= CMP_STRIDE * (CMP_NEAR - 8 + 1) - CMP_LEN + 1


def _cparams(sem):
    return pltpu.CompilerParams(dimension_semantics=sem, vmem_limit_bytes=VMEM_LIMIT)


def _bdot(a, b):
    return jnp.dot(a.astype(BF16), b.astype(BF16), preferred_element_type=F32)


def _sigmoid(x):
    return 1.0 / (1.0 + jnp.exp(-x))


def _rms(x, g):
    return x * lax.rsqrt(jnp.mean(x * x, axis=-1, keepdims=True) + RMS_EPS) * g


def _norm_mod(x, g, mod):
    return _rms(x, g) * (1.0 + mod[1:2]) + mod[0:1]


def _adaln_kernel(c_ref, wm_ref, bm_ref, wf_ref, bf_ref, om_ref, of_ref):
    c = c_ref[...]
    s = c * _sigmoid(c)
    om_ref[0] = jnp.dot(s, wm_ref[0], precision=lax.Precision.HIGHEST,
                        preferred_element_type=F32) + bm_ref[0]
    of_ref[0] = jnp.dot(s, wf_ref[0], precision=lax.Precision.HIGHEST,
                        preferred_element_type=F32) + bf_ref[0]


def _adaln(c, mix_w, mix_b, ffn_w, ffn_b):
    B, D = c.shape
    depth = mix_w.shape[0]
    cp = jnp.zeros((SUBLANES, D), F32).at[:B].set(c)
    w_spec = pl.BlockSpec((1, D, ADA_TN), lambda l, j: (l, 0, j))
    b_spec = pl.BlockSpec((1, 1, ADA_TN), lambda l, j: (l, 0, j))
    o_spec = pl.BlockSpec((1, SUBLANES, ADA_TN), lambda l, j: (l, 0, j))
    om, of = pl.pallas_call(
        _adaln_kernel,
        grid=(depth, 3 * D // ADA_TN),
        in_specs=[pl.BlockSpec((SUBLANES, D), lambda l, j: (0, 0)), w_spec, b_spec, w_spec, b_spec],
        out_specs=[o_spec, o_spec],
        out_shape=[jax.ShapeDtypeStruct((depth, SUBLANES, 3 * D), F32)] * 2,
        compiler_params=_cparams(("arbitrary", "arbitrary")),
        name="adaln",
    )(cp, mix_w, mix_b.reshape(depth, 1, 3 * D), ffn_w, ffn_b.reshape(depth, 1, 3 * D))
    return (om[:, :B].reshape(depth, B, 3, D), of[:, :B].reshape(depth, B, 3, D))


def _mod_spec(layer):
    return pl.BlockSpec((1, 1, 3, D_MODEL), lambda b, *_: (layer, b, 0, 0))


def _vec_spec(layer, width=D_MODEL):
    return pl.BlockSpec((1, 1, width), lambda *_: (layer, 0, 0))


def _post_residual(x, y, post_g, mod):
    return x + mod[2:3] * _rms(y, post_g)


def _conf_a_kernel(x_ref, mod_ref, g_ref, w_ref, b_ref, u_ref):
    D = D_MODEL
    h = _norm_mod(x_ref[0], g_ref[0], mod_ref[0, 0])
    z = _bdot(h, w_ref[...]) + b_ref[0]
    u_ref[0] = z[:, :D] * _sigmoid(z[:, D:])


def _conf_b_kernel(u_ref, up_ref, x_ref, mod_ref, wdw_ref, bdw_ref, lng_ref, lnb_ref,
                   w2_ref, b2_ref, pg_ref, o_ref, ext_ref, conv_ref):
    T, D, K = CONV_TILE, D_MODEL, CONF_KERNEL
    i = pl.program_id(1)
    keep = (i > 0).astype(F32)
    ext_ref[0:CONV_HALO, :] = up_ref[0] * keep
    ext_ref[CONV_HALO:, :] = u_ref[0]
    rb, cb = 64, 256
    off = CONV_HALO - (K - 1)

    def col_body(c, carry):
        c0 = pl.multiple_of(c * cb, cb)
        bias = bdw_ref[0, :, pl.ds(c0, cb)]
        for r in range(T // rb):
            acc = jnp.zeros((rb // SUBLANES, SUBLANES, cb), F32) + bias[None]
            for res in range(SUBLANES):
                offs = [o for o in range(off, off + K) if o % SUBLANES == res]
                lo, hi = offs[0], offs[-1]
                win = ext_ref[pl.ds(r * rb + lo, hi - lo + rb), pl.ds(c0, cb)]
                win = win.reshape((hi - lo + rb) // SUBLANES, SUBLANES, cb)
                for o in offs:
                    i0 = (o - lo) // SUBLANES
                    w = wdw_ref[0, o - off, :, pl.ds(c0, cb)]
                    acc = acc + w[None] * win[i0:i0 + rb // SUBLANES]
            conv_ref[pl.ds(r * rb, rb), pl.ds(c0, cb)] = acc.reshape(rb, cb)
        return carry

    lax.fori_loop(0, D // cb, col_body, 0)
    v = conv_ref[...]
    mu = jnp.mean(v, axis=-1, keepdims=True)
    var = jnp.mean(jnp.square(v - mu), axis=-1, keepdims=True)
    ln = (v - mu) * lax.rsqrt(var + LN_EPS) * lng_ref[0] + lnb_ref[0]
    act = ln * _sigmoid(ln)
    y = _bdot(act, w2_ref[...]) + b2_ref[0]
    o_ref[0] = _post_residual(x_ref[0], y, pg_ref[0], mod_ref[0, 0])


def _conformer_layer(x, mod, layer, j, p):
    B, S, D = x.shape
    T = TOK_TILE
    u = pl.pallas_call(
        _conf_a_kernel,
        grid=(B, S // T),
        in_specs=[pl.BlockSpec((1, T, D), lambda b, i: (b, i, 0)), _mod_spec(layer),
                  _vec_spec(layer), pl.BlockSpec((D, 2 * D), lambda b, i: (0, 0)),
                  _vec_spec(j, 2 * D)],
        out_specs=pl.BlockSpec((1, T, D), lambda b, i: (b, i, 0)),
        out_shape=jax.ShapeDtypeStruct((B, S, D), F32),
        compiler_params=_cparams(("arbitrary", "arbitrary")),
        name="conf_pw1_glu",
    )(x, mod, p["mix_pre_g"], p["conf_w_pw1"][j].astype(BF16), p["conf_b_pw1"])
    T = CONV_TILE
    hb = T // CONV_HALO
    return pl.pallas_call(
        _conf_b_kernel,
        grid=(B, S // T),
        in_specs=[pl.BlockSpec((1, T, D), lambda b, i: (b, i, 0)),
                  pl.BlockSpec((1, CONV_HALO, D), lambda b, i: (b, jnp.maximum(i * hb - 1, 0), 0)),
                  pl.BlockSpec((1, T, D), lambda b, i: (b, i, 0)), _mod_spec(layer),
                  pl.BlockSpec((1, CONF_KERNEL, SUBLANES, D), lambda b, i: (j, 0, 0, 0)),
                  _vec_spec(j), _vec_spec(j), _vec_spec(j),
                  pl.BlockSpec((D, D), lambda b, i: (0, 0)), _vec_spec(j), _vec_spec(layer)],
        out_specs=pl.BlockSpec((1, T, D), lambda b, i: (b, i, 0)),
        out_shape=jax.ShapeDtypeStruct((B, S, D), F32),
        scratch_shapes=[pltpu.VMEM((T + CONV_HALO, D), F32), pltpu.VMEM((T, D), F32)],
        compiler_params=_cparams(("arbitrary", "arbitrary")),
        name="conf_conv_pw2",
    )(u, u, x, mod, jnp.broadcast_to(p["conf_w_dw"][:, :, None, :], (*p["conf_w_dw"].shape[:2], SUBLANES, D)),
      p["conf_b_dw"], p["conf_ln_g"], p["conf_ln_b"],
      p["conf_w_pw2"][j].astype(BF16), p["conf_b_pw2"], p["mix_post_g"])


def _short_conv_kernel(x_ref, mod_ref, g_ref, win_ref, wc_ref, wout_ref, pg_ref, o_ref, ext_ref):
    T, D = TOK_TILE, D_MODEL
    i = pl.program_id(1)

    @pl.when(i == 0)
    def _():
        ext_ref[0:SUBLANES, :] = jnp.zeros((SUBLANES, D), F32)

    x = x_ref[0]
    mod = mod_ref[0, 0]
    h = _norm_mod(x, g_ref[0], mod)
    z = _bdot(h, win_ref[...])
    v, gate_b, gate_c = z[:, :D], z[:, D:2 * D], z[:, 2 * D:]
    ext_ref[SUBLANES:, :] = gate_c * v
    wc = wc_ref[0]
    u = (wc[0:1] * ext_ref[pl.ds(SUBLANES - 2, T), :] + wc[1:2] * ext_ref[pl.ds(SUBLANES - 1, T), :]
         + wc[2:3] * ext_ref[pl.ds(SUBLANES, T), :])
    ext_ref[0:SUBLANES, :] = ext_ref[pl.ds(T, SUBLANES), :]
    y = _bdot(gate_b * u, wout_ref[...])
    o_ref[0] = _post_residual(x, y, pg_ref[0], mod)


def _short_conv_layer(x, mod, layer, j, p):
    B, S, D = x.shape
    T = TOK_TILE
    return pl.pallas_call(
        _short_conv_kernel,
        grid=(B, S // T),
        in_specs=[pl.BlockSpec((1, T, D), lambda b, i: (b, i, 0)), _mod_spec(layer), _vec_spec(layer),
                  pl.BlockSpec((D, 3 * D), lambda b, i: (0, 0)),
                  pl.BlockSpec((1, SHORT_KERNEL, D), lambda b, i: (j, 0, 0)),
                  pl.BlockSpec((D, D), lambda b, i: (0, 0)), _vec_spec(layer)],
        out_specs=pl.BlockSpec((1, T, D), lambda b, i: (b, i, 0)),
        out_shape=jax.ShapeDtypeStruct((B, S, D), F32),
        scratch_shapes=[pltpu.VMEM((T + SUBLANES, D), F32)],
        compiler_params=_cparams(("arbitrary", "arbitrary")),
        name="short_conv",
    )(x, mod, p["mix_pre_g"], p["sc_w_in"][j].astype(BF16), p["sc_w_conv"],
      p["sc_w_out"][j].astype(BF16), p["mix_post_g"])


def _swiglu_step(h, wg_ref, wu_ref, wd_ref, acc_ref):
    a = _bdot(h, wg_ref[...])
    u = _bdot(h, wu_ref[...])
    acc_ref[...] += _bdot((a * _sigmoid(a)) * u, wd_ref[...])


def _dense_ffn_kernel(x_ref, mod_ref, g_ref, wg_ref, wu_ref, wd_ref, pg_ref, o_ref, h_ref, acc_ref):
    f = pl.program_id(2)

    @pl.when(f == 0)
    def _():
        h_ref[...] = _norm_mod(x_ref[0], g_ref[0], mod_ref[0, 0]).astype(BF16)
        acc_ref[...] = jnp.zeros_like(acc_ref)

    _swiglu_step(h_ref[...], wg_ref.at[0], wu_ref.at[0], wd_ref.at[0], acc_ref)

    @pl.when(f == pl.num_programs(2) - 1)
    def _():
        o_ref[0] = _post_residual(x_ref[0], acc_ref[...], pg_ref[0], mod_ref[0, 0])


def _dense_ffn_layer(x, mod, layer, j, p):
    B, S, D = x.shape
    T, TF, F = min(FFN_TILE, S), FFN_TF, D_FF_DENSE
    return pl.pallas_call(
        _dense_ffn_kernel,
        grid=(B, S // T, F // TF),
        in_specs=[pl.BlockSpec((1, T, D), lambda b, i, f: (b, i, 0)), _mod_spec(layer), _vec_spec(layer),
                  pl.BlockSpec((1, D, TF), lambda b, i, f: (j, 0, f)),
                  pl.BlockSpec((1, D, TF), lambda b, i, f: (j, 0, f)),
                  pl.BlockSpec((1, TF, D), lambda b, i, f: (j, f, 0)), _vec_spec(layer)],
        out_specs=pl.BlockSpec((1, T, D), lambda b, i, f: (b, i, 0)),
        out_shape=jax.ShapeDtypeStruct((B, S, D), F32),
        scratch_shapes=[pltpu.VMEM((T, D), BF16), pltpu.VMEM((T, D), F32)],
        compiler_params=_cparams(("arbitrary", "arbitrary", "arbitrary")),
        name="dense_swiglu",
    )(x, mod, p["ffn_pre_g"], p["dense_w_gate"], p["dense_w_up"], p["dense_w_down"], p["ffn_post_g"])


def _router_kernel(x_ref, mod_ref, g_ref, wr_ref, br_ref, h_ref, ri_ref, rw_ref, cnt_ref, base_ref):
    T = ROUTE_TILE
    i = pl.program_id(0)

    @pl.when(i == 0)
    def _():
        base_ref[...] = jnp.zeros_like(base_ref)

    h = _norm_mod(x_ref[...], g_ref[0], mod_ref[0, 0])
    h_ref[...] = h
    logits = jnp.dot(h, wr_ref[...], precision=lax.Precision.HIGHEST,
                     preferred_element_type=F32) + br_ref[0]
    lane = lax.broadcasted_iota(I32, (T, LANES), 1).astype(F32)
    logits = jnp.where(lane < N_EXPERTS, logits, -jnp.inf)
    l0 = jnp.max(logits, axis=1, keepdims=True)
    e0 = jnp.min(jnp.where(logits == l0, lane, float(LANES)), axis=1, keepdims=True)
    rest = jnp.where(lane == e0, -jnp.inf, logits)
    l1 = jnp.max(rest, axis=1, keepdims=True)
    e1 = jnp.min(jnp.where(rest == l1, lane, float(LANES)), axis=1, keepdims=True)
    ex = jnp.exp(l1 - l0)
    w0 = 1.0 / (1.0 + ex)
    w1 = ex / (1.0 + ex)
    onehot = ((lane == e0) | (lane == e1)).astype(F32)
    row = lax.broadcasted_iota(I32, (T, T), 0)
    col = lax.broadcasted_iota(I32, (T, T), 1)
    tri = jnp.where(row > col, 1.0, 0.0).astype(BF16)
    before = jnp.dot(tri, onehot.astype(BF16), preferred_element_type=F32) + base_ref[...]
    r0 = jnp.sum(jnp.where(lane == e0, before, 0.0), axis=1, keepdims=True)
    r1 = jnp.sum(jnp.where(lane == e1, before, 0.0), axis=1, keepdims=True)
    base_ref[...] += jnp.sum(onehot, axis=0, keepdims=True)
    cnt_ref[...] = base_ref[...].astype(I32)
    ri = jnp.where(lane == 0, e0, jnp.where(lane == 1, e1, jnp.where(lane == 2, r0, r1)))
    ri_ref[...] = ri[:, :SUBLANES].astype(I32)
    rw_ref[...] = jnp.where(lane == 0, w0, w1)[:, :SUBLANES]


def _dispatch_kernel(pstart_ref, h_ref, ri_ref, zero_ref, buf_ref, sem):
    del zero_ref
    T = ROW_DMA_TILE

    def row_copy(t, k, dest):
        return pltpu.make_async_copy(h_ref.at[pl.ds(t, 1)], buf_ref.at[pl.ds(dest, 1)], sem)

    def issue(t, carry):
        for k in range(2):
            dest = pstart_ref[ri_ref[t, k]] + ri_ref[t, 2 + k]
            row_copy(t, k, dest).start()
        return carry

    lax.fori_loop(0, T, issue, 0)

    def drain(t, carry):
        for k in range(2):
            row_copy(0, k, 0).wait()
        return carry

    lax.fori_loop(0, T, drain, 0)


def _expert_kernel(blk_e_ref, nblk_ref, xb_ref, wg_ref, wu_ref, wd_ref, o_ref, h_ref, acc_ref):
    del blk_e_ref
    i = pl.program_id(0)
    f = pl.program_id(1)
    live = i < nblk_ref[0]

    @pl.when(f == 0)
    def _():
        h_ref[...] = xb_ref[...].astype(BF16)
        acc_ref[...] = jnp.zeros_like(acc_ref)

    @pl.when(live)
    def _():
        _swiglu_step(h_ref[...], wg_ref.at[0, 0], wu_ref.at[0, 0], wd_ref.at[0, 0], acc_ref)

    @pl.when(f == pl.num_programs(1) - 1)
    def _():
        o_ref[...] = acc_ref[...]


def _combine_kernel(pstart_ref, x_ref, mod_ref, pg_ref, ri_ref, rw_ref, yb_ref, o_ref, y0_ref, y1_ref, sem):
    T = ROW_DMA_TILE
    ys = (y0_ref, y1_ref)

    def row_copy(t, k, src):
        return pltpu.make_async_copy(yb_ref.at[pl.ds(src, 1)], ys[k].at[pl.ds(t, 1)], sem)

    def issue(t, carry):
        for k in range(2):
            src = pstart_ref[ri_ref[t, k]] + ri_ref[t, 2 + k]
            row_copy(t, k, src).start()
        return carry

    lax.fori_loop(0, T, issue, 0)

    def drain(t, carry):
        for k in range(2):
            row_copy(0, k, 0).wait()
        return carry

    lax.fori_loop(0, T, drain, 0)
    rw = rw_ref[...]
    y = rw[:, 0:1] * y0_ref[...] + rw[:, 1:2] * y1_ref[...]
    o_ref[...] = _post_residual(x_ref[...], y, pg_ref[0], mod_ref[0, 0])


def _moe_ffn_layer(x, mod, layer, j, p):
    B, S, D = x.shape
    N = B * S
    T = ROUTE_TILE
    tiles_per_seq = S // T
    xt = x.reshape(N, D)
    wr = jnp.zeros((D, LANES), F32).at[:, :N_EXPERTS].set(p["moe_w_router"][j])
    br = jnp.zeros((1, 1, LANES), F32).at[0, 0, :N_EXPERTS].set(p["moe_b_router"][j])
    h, ri, rw, cnt = pl.pallas_call(
        _router_kernel,
        grid=(N // T,),
        in_specs=[pl.BlockSpec((T, D), lambda i: (i, 0)),
                  pl.BlockSpec((1, 1, 3, D), lambda i: (layer, i // tiles_per_seq, 0, 0)),
                  _vec_spec(layer), pl.BlockSpec((D, LANES), lambda i: (0, 0)), _vec_spec(0, LANES)],
        out_specs=[pl.BlockSpec((T, D), lambda i: (i, 0)), pl.BlockSpec((T, SUBLANES), lambda i: (i, 0)),
                   pl.BlockSpec((T, SUBLANES), lambda i: (i, 0)), pl.BlockSpec((1, LANES), lambda i: (0, 0))],
        out_shape=[jax.ShapeDtypeStruct((N, D), F32), jax.ShapeDtypeStruct((N, SUBLANES), I32),
                   jax.ShapeDtypeStruct((N, SUBLANES), F32), jax.ShapeDtypeStruct((1, LANES), I32)],
        scratch_shapes=[pltpu.VMEM((1, LANES), F32)],
        compiler_params=_cparams(("arbitrary",)),
        name="moe_router",
    )(xt, mod, p["ffn_pre_g"], wr, br)

    R = MOE_ROWS
    counts = cnt[0, :N_EXPERTS]
    padded = (counts + R - 1) // R * R
    pend = jnp.cumsum(padded)
    pstart = (pend - padded).astype(I32)
    n_blocks = -(-(2 * N + N_EXPERTS * (R - 1)) // R)
    blk_e = jnp.minimum(jnp.sum((jnp.arange(n_blocks) * R)[:, None] >= pend[None, :], axis=1),
                        N_EXPERTS - 1).astype(I32)
    n_live = (pend[-1] // R).astype(I32).reshape(1)
    rows = n_blocks * R

    TD = ROW_DMA_TILE
    buf = pl.pallas_call(
        _dispatch_kernel,
        grid_spec=pltpu.PrefetchScalarGridSpec(
            num_scalar_prefetch=1,
            grid=(N // TD,),
            in_specs=[pl.BlockSpec((TD, D), lambda i, ps: (i, 0)),
                      pl.BlockSpec((TD, SUBLANES), lambda i, ps: (i, 0), memory_space=pltpu.SMEM),
                      pl.BlockSpec(memory_space=pl.ANY)],
            out_specs=pl.BlockSpec(memory_space=pl.ANY),
            scratch_shapes=[pltpu.SemaphoreType.DMA],
        ),
        out_shape=jax.ShapeDtypeStruct((rows, D), F32),
        input_output_aliases={3: 0},
        compiler_params=_cparams(("arbitrary",)),
        name="moe_dispatch",
    )(pstart, h, ri, jnp.zeros((rows, D), F32))

    TF, F = FFN_TF, D_FF_EXPERT
    yb = pl.pallas_call(
        _expert_kernel,
        grid_spec=pltpu.PrefetchScalarGridSpec(
            num_scalar_prefetch=2,
            grid=(n_blocks, F // TF),
            in_specs=[pl.BlockSpec((R, D), lambda i, f, be, nb: (i, 0)),
                      pl.BlockSpec((1, 1, D, TF), lambda i, f, be, nb: (j, be[i], 0, f)),
                      pl.BlockSpec((1, 1, D, TF), lambda i, f, be, nb: (j, be[i], 0, f)),
                      pl.BlockSpec((1, 1, TF, D), lambda i, f, be, nb: (j, be[i], f, 0))],
            out_specs=pl.BlockSpec((R, D), lambda i, f, be, nb: (i, 0)),
            scratch_shapes=[pltpu.VMEM((R, D), BF16), pltpu.VMEM((R, D), F32)],
        ),
        out_shape=jax.ShapeDtypeStruct((rows, D), F32),
        compiler_params=_cparams(("arbitrary", "arbitrary")),
        name="moe_experts",
    )(blk_e, n_live, buf, p["moe_w_gate"], p["moe_w_up"], p["moe_w_down"])

    tiles_per_seq_d = S // TD
    out = pl.pallas_call(
        _combine_kernel,
        grid_spec=pltpu.PrefetchScalarGridSpec(
            num_scalar_prefetch=1,
            grid=(N // TD,),
            in_specs=[pl.BlockSpec((TD, D), lambda i, ps: (i, 0)),
                      pl.BlockSpec((1, 1, 3, D), lambda i, ps: (layer, i // tiles_per_seq_d, 0, 0)),
                      pl.BlockSpec((1, 1, D), lambda i, ps: (layer, 0, 0)),
                      pl.BlockSpec((TD, SUBLANES), lambda i, ps: (i, 0), memory_space=pltpu.SMEM),
                      pl.BlockSpec((TD, SUBLANES), lambda i, ps: (i, 0)),
                      pl.BlockSpec(memory_space=pl.ANY)],
            out_specs=pl.BlockSpec((TD, D), lambda i, ps: (i, 0)),
            scratch_shapes=[pltpu.VMEM((TD, D), F32), pltpu.VMEM((TD, D), F32), pltpu.SemaphoreType.DMA],
        ),
        out_shape=jax.ShapeDtypeStruct((N, D), F32),
        compiler_params=_cparams(("arbitrary",)),
        name="moe_combine",
    )(pstart, xt, mod, p["ffn_post_g"], ri, rw, yb)
    return out.reshape(B, S, D)


NSA_W = D_MODEL + 2 * N_GROUPS * LANES + 4 * N_GROUPS * HEAD_DIM + LANES


def _ones_rows(n):
    r = lax.broadcasted_iota(I32, (V_ROWS - HEAD_DIM, n), 0)
    return jnp.where(r == 0, 1.0, 0.0)


def _nsa_in_kernel(x_ref, mod_ref, g_ref, w_ref, q_ref, ks_ref, vst_ref, kw_ref, vwt_ref,
                   kc_ref, vc_ref, gt_ref):
    D, G, DH = D_MODEL, N_GROUPS, HEAD_DIM
    KV = G * DH
    T = x_ref.shape[1]
    h = _norm_mod(x_ref[0], g_ref[0], mod_ref[0, 0])
    z = _bdot(h, w_ref[...])
    q_ref[0] = z[:, :D].astype(BF16)
    lane = lax.broadcasted_iota(I32, (T, LANES), 1)
    pos = pl.program_id(1) * T + lax.broadcasted_iota(I32, (T, LANES), 0)
    one_col = jnp.where(lane == K_ONE, 1.0, 0.0)
    blk_col = jnp.where(lane == K_PEN + (pos // SEL_BLOCK) % SEL_PEN_ROWS, 1.0, 0.0)
    o = D
    for g in range(G):
        ks_ref[0, g] = (z[:, o + g * LANES:o + (g + 1) * LANES] + (one_col + blk_col)).astype(BF16)
    o += G * LANES
    for g in range(G):
        kw_ref[0, g] = (z[:, o + g * LANES:o + (g + 1) * LANES] + one_col).astype(BF16)
    o += G * LANES
    v_s, v_w = z[:, o:o + KV], z[:, o + KV:o + 2 * KV]
    k_c, v_c = z[:, o + 2 * KV:o + 3 * KV], z[:, o + 3 * KV:o + 4 * KV]
    gates = _sigmoid(z[:, o + 4 * KV:])
    vst = v_s.T
    vwt = v_w.T
    ones = _ones_rows(T)
    for g in range(G):
        sl = slice(g * DH, (g + 1) * DH)
        kc_ref[0, g] = k_c[:, sl]
        vc_ref[0, g] = v_c[:, sl]
        vst_ref[0, g] = jnp.concatenate([vst[sl, :], ones], axis=0).astype(BF16)
        vwt_ref[0, g] = jnp.concatenate([vwt[sl, :], ones], axis=0).astype(BF16)
    gt_ref[0] = gates.T[:3 * N_HEADS, :]


def _gelu_tanh(x):
    return x * (0.5 * (1.0 + jnp.tanh(math.sqrt(2.0 / math.pi) * (x + 0.044715 * (x * x * x)))))


def _compress_kernel(kc_ref, vc_ref, kpe_ref, kw1_ref, kw2_ref, vpe_ref, vw1_ref, vw2_ref,
                     kco_ref, vcto_ref):
    nc = kc_ref.shape[2]

    def comp(c, pe_ref, w1_ref, w2_ref):
        a = _bdot(c + pe_ref[0:1], w1_ref[0])
        b = _bdot(c + pe_ref[1:2], w1_ref[1])
        b_next = pltpu.roll(b, nc - 1, axis=0)
        out = _bdot(_gelu_tanh(a + b_next), w2_ref[...])
        row = lax.broadcasted_iota(I32, out.shape, 0)
        return jnp.where(row < nc - 1, out, 0.0)

    kc = comp(kc_ref[0, 0], kpe_ref, kw1_ref, kw2_ref)
    lane = lax.broadcasted_iota(I32, kc.shape, 1)
    kco_ref[0, 0] = (kc + jnp.where(lane == K_ONE, 1.0, 0.0)).astype(BF16)
    vct = comp(vc_ref[0, 0], vpe_ref, vw1_ref, vw2_ref).T
    vcto_ref[0, 0] = jnp.concatenate([vct, _ones_rows(nc)], axis=0).astype(BF16)


def _bias_table_kernel(tbl_ref, tab_ref, ctab_ref, far_ref):
    g = pl.program_id(0)

    def bias_of(dist, h, far):
        v = jnp.full(dist.shape, tbl_ref[0, h], F32)
        for b in range(1, REL_BUCKETS):
            v = jnp.where(dist >= BUCKET_THR[b], tbl_ref[b, h], v)
        return jnp.where(dist >= 0, v * LOG2E - far, NEG)

    nc = (ctab_ref.shape[1] - LANES) // 2
    key = lax.broadcasted_iota(I32, (QT, QT), 0)
    qry = lax.broadcasted_iota(I32, (QT, QT), 1)
    crow = lax.broadcasted_iota(I32, (CMP_NEAR, QT), 0)
    cqry = lax.broadcasted_iota(I32, (CMP_NEAR, QT), 1)
    for hh in range(HPG):
        h = g * HPG + hh
        lanes = slice(hh * QT, (hh + 1) * QT)
        far = jnp.full((1, QT), tbl_ref[REL_BUCKETS - 1, h] * LOG2E, F32).astype(BF16).astype(F32)
        far_ref[0, :, lanes] = far

        def near_slot(slot, carry, h=h, lanes=lanes, far=far):
            tab_ref[0, slot, :, lanes] = bias_of(slot * QT + qry - key, h, far)
            return carry

        lax.fori_loop(0, NEAR_SLOTS, near_slot, 0)
        tab_ref[0, SLOT_FAR, :, lanes] = jnp.zeros((QT, QT), F32)
        tab_ref[0, SLOT_NONE, :, lanes] = jnp.full((QT, QT), NEG, F32)
        dw = WINDOW + qry - key
        tab_ref[0, SLOT_WIN, :, lanes] = jnp.where(dw < WINDOW, bias_of(dw, h, far), NEG)
        ctab_ref[0, 0:nc, lanes] = jnp.zeros((nc, QT), F32)
        dc = cqry - CMP_STRIDE * (crow - (CMP_NEAR - SUBLANES)) - (CMP_LEN - 1)
        ctab_ref[0, nc:nc + CMP_NEAR, lanes] = bias_of(dc, h, far)
        ctab_ref[0, nc + CMP_NEAR:, lanes] = jnp.full((nc + LANES - CMP_NEAR, QT), NEG, F32)


def _nsa_attn_kernel(q_ref, ks_ref, vst_ref, kw_refs, vwt_refs, kc_ref, vct_ref, gt_ref,
                     tab_ref, ctab_ref, far_ref, o_ref, qt_ref, s_even_ref, s_odd_ref, psum_ref, pen_ref):
    DH = HEAD_DIM
    a = pl.program_id(2)
    nc = kc_ref.shape[2]
    n_sel = pen_ref.shape[0]

    qT = q_ref[0].astype(F32).T
    aug_row = lax.broadcasted_iota(I32, (LANES - DH, QL), 0)
    aug = jnp.where(aug_row == 0, far_ref[0], 0.0).astype(BF16)
    for i in range(2):
        for hh in range(HPG):
            qt_ref[i, 0:DH, hh * QT:(hh + 1) * QT] = qT[hh * DH:(hh + 1) * DH, :].astype(BF16)
        qt_ref[i, DH:, :] = aug
    qt = qt_ref[0]

    nw = WINDOW // QT + 1
    s_w = []
    for d in range(nw):
        slot = SLOT_WIN if d == nw - 1 else d
        if d > 0:
            slot = jnp.where(a >= d, slot, SLOT_NONE)
        s_w.append(jnp.dot(kw_refs[d][0, 0], qt, preferred_element_type=F32) + tab_ref[0, slot])
    m_w = functools.reduce(jnp.maximum, [jnp.max(s_d, axis=0, keepdims=True) for s_d in s_w])
    acc_w = functools.reduce(lambda x, y: x + y, [
        jnp.dot(vwt_refs[d][0, 0], jnp.exp2(s_w[d] - m_w).astype(BF16), preferred_element_type=F32)
        for d in range(nw)])
    o_w = acc_w[0:DH] * (1.0 / acc_w[DH:DH + 1])

    n_hi = (QT // CMP_STRIDE) * (a + 1)
    c0 = pl.multiple_of(nc - (n_hi - CMP_NEAR), SUBLANES)
    s = jnp.dot(kc_ref[0, 0], qt, preferred_element_type=F32) + ctab_ref[0, pl.ds(c0, nc), :]
    m = jnp.max(s, axis=0, keepdims=True)
    e = jnp.exp2(s - m)
    l = jnp.sum(e, axis=0, keepdims=True)
    pn = e * jnp.where(m > 0.1 * NEG, 1.0 / l, 0.0)
    o_c = jnp.dot(vct_ref[0, 0], pn.astype(BF16), preferred_element_type=F32)[0:DH]
    psum_ref[...] = pn[:, 0:QT] + pn[:, QT:2 * QT] + pn[:, 2 * QT:3 * QT] + pn[:, 3 * QT:4 * QT]

    ratio = SEL_BLOCK // CMP_STRIDE
    imp = psum_ref[pl.ds(0, n_sel, stride=ratio), :]
    for jj in range(1, ratio):
        imp = imp + psum_ref[pl.ds(jj, n_sel, stride=ratio), :]
    last = psum_ref[pl.ds(ratio - 1, n_sel, stride=ratio), :]
    srow = lax.broadcasted_iota(I32, (n_sel, QT), 0)
    imp = imp + jnp.where(srow > 0, pltpu.roll(last, 1, axis=0), 0.0)
    lane_q = lax.broadcasted_iota(I32, (n_sel, QT), 1)
    cur = (QT // SEL_BLOCK) * a + lane_q // SEL_BLOCK
    back = cur - srow
    valid = back >= 0
    forced = (srow == 0) | (valid & (back < N_LOCAL_BLOCKS))
    n_forced = 1 + N_LOCAL_BLOCKS
    score = jnp.where(forced, -jnp.inf, jnp.where(valid, imp, -1.0))
    srow_f = srow.astype(F32)
    for _ in range(min(SEL_TOPK, n_sel) - n_forced):
        mx = jnp.max(score, axis=0, keepdims=True)
        first = jnp.min(jnp.where(score == mx, srow_f, float(n_sel)), axis=0, keepdims=True)
        score = jnp.where(srow_f == first, -jnp.inf, score)
    pen = jnp.where(score == -jnp.inf, 0.0, NEG)
    for hh in range(HPG):
        pen_ref[:, hh * QT:(hh + 1) * QT] = pen

    big = SEL_SUB * QT
    span = SEL_STREAMS * big
    sub_per_step = SEL_STREAMS * SEL_SUB

    n_span = ks_ref.shape[2] // span

    def scores(step, buf):
        sc = jnp.minimum(step, n_span - 1)
        pen_rows = pen_ref[pl.ds(pl.multiple_of(sc * SEL_PEN_ROWS, SEL_PEN_ROWS), SEL_PEN_ROWS), :]
        qt_ref[buf, K_PEN:K_PEN + SEL_PEN_ROWS, :] = pen_rows.astype(BF16)
        k0 = pl.multiple_of(sc * span, span)
        return jnp.dot(ks_ref[0, 0, pl.ds(k0, span), :], qt_ref[buf], preferred_element_type=F32)

    def update(s_all, step, state, near):
        k0 = pl.multiple_of(jnp.minimum(step, n_span - 1) * span, span)
        out = []
        for t in range(SEL_STREAMS):
            m, acc = state[t]
            s = s_all[t * big:(t + 1) * big]
            if near:
                parts = []
                for u in range(SEL_SUB):
                    d = a - (step * sub_per_step + t * SEL_SUB + u)
                    slot = jnp.where(d >= 0, jnp.minimum(d, SLOT_FAR), SLOT_NONE)
                    parts.append(s[u * QT:(u + 1) * QT] + tab_ref[0, slot])
                s = jnp.concatenate(parts, axis=0)
            m_new = jnp.maximum(m, jnp.max(s, axis=0, keepdims=True))
            p = jnp.exp2(s - m_new).astype(BF16)
            acc = jnp.exp2(m - m_new) * acc + jnp.dot(vst_ref[0, 0, :, pl.ds(k0 + t * big, big)], p,
                                                      preferred_element_type=F32)
            out.append((m_new, acc))
        return tuple(out)

    def two_steps(jj, state, near):
        s_odd_ref[...] = scores(2 * jj + 1, 1)
        state = update(s_even_ref[...], 2 * jj, state, near)
        s_even_ref[...] = scores(2 * jj + 2, 0)
        return update(s_odd_ref[...], 2 * jj + 1, state, near)

    n_steps = a // sub_per_step + 1
    n_far2 = jnp.maximum(a - (NEAR_SLOTS - 1), 0) // (2 * sub_per_step)
    n_pairs = (n_steps + 1) // 2
    s_even_ref[...] = scores(0, 0)
    state = tuple((jnp.full((1, QL), NEG, F32), jnp.zeros((V_ROWS, QL), F32)) for _ in range(SEL_STREAMS))
    state = lax.fori_loop(0, n_far2, functools.partial(two_steps, near=False), state)
    state = lax.fori_loop(n_far2, n_pairs, functools.partial(two_steps, near=True), state)
    m_s = functools.reduce(jnp.maximum, [m for m, _ in state])
    acc_s = functools.reduce(lambda x, y: x + y, [jnp.exp2(m - m_s) * acc for m, acc in state])
    o_s = acc_s[0:DH] * (1.0 / acc_s[DH:DH + 1])

    def gate(branch):
        return jnp.concatenate([gt_ref[0, 0, 3 * hh + branch:3 * hh + branch + 1, :] for hh in range(HPG)],
                               axis=1)

    oT = gate(0) * o_c + gate(1) * o_s + gate(2) * o_w
    stacked = jnp.concatenate([oT[:, hh * QT:(hh + 1) * QT] for hh in range(HPG)], axis=0)
    o_ref[0] = stacked.T.astype(BF16)


def _nsa_attn_wrapper(*refs):
    nw = WINDOW // QT + 1
    q_ref, ks_ref, vst_ref = refs[0:3]
    kw_refs = refs[3:3 + nw]
    vwt_refs = refs[3 + nw:3 + 2 * nw]
    rest = refs[3 + 2 * nw:]
    _nsa_attn_kernel(q_ref, ks_ref, vst_ref, kw_refs, vwt_refs, *rest)


def _out_proj_kernel(o_ref, x_ref, mod_ref, w_ref, pg_ref, out_ref):
    y = jnp.dot(o_ref[0], w_ref[...], preferred_element_type=F32)
    out_ref[0] = _post_residual(x_ref[0], y, pg_ref[0], mod_ref[0, 0])


def _nsa_layer(x, mod, layer, j, p):
    B, S, D = x.shape
    G, DH, KV = N_GROUPS, HEAD_DIM, N_GROUPS * HEAD_DIM
    T = TOK_TILE
    NC = S // CMP_STRIDE
    n_sel = S // SEL_BLOCK

    w_in = p["nsa_w_in"][j]
    cuts = [D + i * KV for i in range(7)]
    wq, wkc, wvc, wks, wvs, wkw, wvw, wg = jnp.split(w_in, cuts, axis=-1)
    wg = jnp.zeros((D, LANES), F32).at[:, :3 * N_HEADS].set(wg)

    def slabs(w):
        return jnp.zeros((D, G, LANES), F32).at[:, :, :DH].set(w.reshape(D, G, DH)).reshape(D, G * LANES)

    w_all = jnp.concatenate([wq * (HEAD_DIM ** -0.5 * LOG2E), slabs(wks), slabs(wkw), wvs, wvw, wkc, wvc, wg],
                            axis=1).astype(BF16)

    k_spec = pl.BlockSpec((1, G, T, LANES), lambda b, i: (b, 0, i, 0))
    kc_spec = pl.BlockSpec((1, G, T, DH), lambda b, i: (b, 0, i, 0))
    vt_spec = pl.BlockSpec((1, G, V_ROWS, T), lambda b, i: (b, 0, 0, i))
    q, ks, vst, kw, vwt, kc_raw, vc_raw, gt = pl.pallas_call(
        _nsa_in_kernel,
        grid=(B, S // T),
        in_specs=[pl.BlockSpec((1, T, D), lambda b, i: (b, i, 0)), _mod_spec(layer), _vec_spec(layer),
                  pl.BlockSpec((D, NSA_W), lambda b, i: (0, 0))],
        out_specs=[pl.BlockSpec((1, T, D), lambda b, i: (b, i, 0)), k_spec, vt_spec, k_spec, vt_spec,
                   kc_spec, kc_spec, pl.BlockSpec((1, 3 * N_HEADS, T), lambda b, i: (b, 0, i))],
        out_shape=[jax.ShapeDtypeStruct((B, S, D), BF16),
                   jax.ShapeDtypeStruct((B, G, S, LANES), BF16), jax.ShapeDtypeStruct((B, G, V_ROWS, S), BF16),
                   jax.ShapeDtypeStruct((B, G, S, LANES), BF16), jax.ShapeDtypeStruct((B, G, V_ROWS, S), BF16),
                   jax.ShapeDtypeStruct((B, G, S, DH), F32), jax.ShapeDtypeStruct((B, G, S, DH), F32),
                   jax.ShapeDtypeStruct((B, 3 * N_HEADS, S), F32)],
        compiler_params=_cparams(("arbitrary", "arbitrary")),
        name="nsa_in_proj",
    )(x, mod, p["mix_pre_g"], w_all)

    half = CMP_LEN // 2
    chunk_w = half * DH

    def split_w1(w1):
        return w1.reshape(2, chunk_w, CMP_HIDDEN).astype(BF16)

    def split_pe(pe):
        return pe.reshape(2, chunk_w)

    c_spec = pl.BlockSpec((1, 1, NC, chunk_w), lambda b, g: (b, g, 0, 0))
    full = lambda shape: pl.BlockSpec(shape, lambda b, g: (0,) * len(shape))
    kc, vct = pl.pallas_call(
        _compress_kernel,
        grid=(B, G),
        in_specs=[c_spec, c_spec,
                  full((2, chunk_w)), full((2, chunk_w, CMP_HIDDEN)), full((CMP_HIDDEN, LANES)),
                  full((2, chunk_w)), full((2, chunk_w, CMP_HIDDEN)), full((CMP_HIDDEN, DH))],
        out_specs=[pl.BlockSpec((1, 1, NC, LANES), lambda b, g: (b, g, 0, 0)),
                   pl.BlockSpec((1, 1, V_ROWS, NC), lambda b, g: (b, g, 0, 0))],
        out_shape=[jax.ShapeDtypeStruct((B, G, NC, LANES), BF16),
                   jax.ShapeDtypeStruct((B, G, V_ROWS, NC), BF16)],
        compiler_params=_cparams(("arbitrary", "arbitrary")),
        name="nsa_compress",
    )(kc_raw.reshape(B, G, NC, chunk_w), vc_raw.reshape(B, G, NC, chunk_w),
      split_pe(p["nsa_k_pe"][j]), split_w1(p["nsa_k_w1"][j]),
      jnp.zeros((CMP_HIDDEN, LANES), BF16).at[:, :DH].set(p["nsa_k_w2"][j].astype(BF16)),
      split_pe(p["nsa_v_pe"][j]), split_w1(p["nsa_v_w1"][j]), p["nsa_v_w2"][j].astype(BF16))

    ctab_rows = 2 * NC + LANES
    tab, ctab, far = pl.pallas_call(
        _bias_table_kernel,
        grid=(G,),
        in_specs=[pl.BlockSpec(memory_space=pltpu.SMEM)],
        out_specs=[pl.BlockSpec((1, N_SLOTS, QT, QL), lambda g: (g, 0, 0, 0)),
                   pl.BlockSpec((1, ctab_rows, QL), lambda g: (g, 0, 0)),
                   pl.BlockSpec((1, 1, QL), lambda g: (g, 0, 0))],
        out_shape=[jax.ShapeDtypeStruct((G, N_SLOTS, QT, QL), F32),
                   jax.ShapeDtypeStruct((G, ctab_rows, QL), F32),
                   jax.ShapeDtypeStruct((G, 1, QL), F32)],
        compiler_params=_cparams(("arbitrary",)),
        name="nsa_bias_tables",
    )(p["rel_table"])

    nw = WINDOW // QT + 1
    kw_specs = [pl.BlockSpec((1, 1, QT, LANES), functools.partial(
        lambda b, g, a, d: (b, g, jnp.maximum(a - d, 0), 0), d=d)) for d in range(nw)]
    vwt_specs = [pl.BlockSpec((1, 1, V_ROWS, QT), functools.partial(
        lambda b, g, a, d: (b, g, 0, jnp.maximum(a - d, 0)), d=d)) for d in range(nw)]
    o = pl.pallas_call(
        _nsa_attn_wrapper,
        grid=(B, G, S // QT),
        in_specs=[pl.BlockSpec((1, QT, HPG * DH), lambda b, g, a: (b, a, g)),
                  pl.BlockSpec((1, 1, S, LANES), lambda b, g, a: (b, g, 0, 0)),
                  pl.BlockSpec((1, 1, V_ROWS, S), lambda b, g, a: (b, g, 0, 0)),
                  *kw_specs, *vwt_specs,
                  pl.BlockSpec((1, 1, NC, LANES), lambda b, g, a: (b, g, 0, 0)),
                  pl.BlockSpec((1, 1, V_ROWS, NC), lambda b, g, a: (b, g, 0, 0)),
                  pl.BlockSpec((1, 1, 3 * HPG, QT), lambda b, g, a: (b, g, 0, a)),
                  pl.BlockSpec((1, N_SLOTS, QT, QL), lambda b, g, a: (g, 0, 0, 0)),
                  pl.BlockSpec((1, ctab_rows, QL), lambda b, g, a: (g, 0, 0)),
                  pl.BlockSpec((1, 1, QL), lambda b, g, a: (g, 0, 0))],
        out_specs=pl.BlockSpec((1, QT, HPG * DH), lambda b, g, a: (b, a, g)),
        out_shape=jax.ShapeDtypeStruct((B, S, D), BF16),
        scratch_shapes=[pltpu.VMEM((2, LANES, QL), BF16),
                        pltpu.VMEM((SEL_STREAMS * SEL_SUB * QT, QL), F32),
                        pltpu.VMEM((SEL_STREAMS * SEL_SUB * QT, QL), F32),
                        pltpu.VMEM((NC, QT), F32), pltpu.VMEM((n_sel, QL), F32)],
        compiler_params=_cparams(("arbitrary", "arbitrary", "arbitrary")),
        name="nsa_attention",
    )(q, ks, vst, *([kw] * nw), *([vwt] * nw), kc, vct, gt.reshape(B, G, 3 * HPG, S), tab, ctab, far)

    return pl.pallas_call(
        _out_proj_kernel,
        grid=(B, S // T),
        in_specs=[pl.BlockSpec((1, T, D), lambda b, i: (b, i, 0)),
                  pl.BlockSpec((1, T, D), lambda b, i: (b, i, 0)), _mod_spec(layer),
                  pl.BlockSpec((D, D), lambda b, i: (0, 0)), _vec_spec(layer)],
        out_specs=pl.BlockSpec((1, T, D), lambda b, i: (b, i, 0)),
        out_shape=jax.ShapeDtypeStruct((B, S, D), F32),
        compiler_params=_cparams(("arbitrary", "arbitrary")),
        name="nsa_out_proj",
    )(o, x, mod, p["nsa_w_out"][j].astype(BF16), p["mix_post_g"])


def _as_rows(v):
    return v.reshape(v.shape[0], 1, v.shape[-1])


def kernel(x, c, mix_pre_g, mix_post_g, mix_ada_w, mix_ada_b, ffn_pre_g, ffn_post_g, ffn_ada_w, ffn_ada_b, conf_w_pw1, conf_b_pw1, conf_w_dw, conf_b_dw, conf_ln_g, conf_ln_b, conf_w_pw2, conf_b_pw2, sc_w_in, sc_w_conv, sc_w_out, nsa_w_in, nsa_k_pe, nsa_k_w1, nsa_k_w2, nsa_v_pe, nsa_v_w1, nsa_v_w2, nsa_w_out, rel_table, dense_w_gate, dense_w_up, dense_w_down, moe_w_router, moe_b_router, moe_w_gate, moe_w_up, moe_w_down):
    p = dict(
        mix_pre_g=_as_rows(mix_pre_g), mix_post_g=_as_rows(mix_post_g),
        ffn_pre_g=_as_rows(ffn_pre_g), ffn_post_g=_as_rows(ffn_post_g),
        conf_w_pw1=conf_w_pw1, conf_b_pw1=_as_rows(conf_b_pw1), conf_w_dw=conf_w_dw,
        conf_b_dw=_as_rows(conf_b_dw), conf_ln_g=_as_rows(conf_ln_g), conf_ln_b=_as_rows(conf_ln_b),
        conf_w_pw2=conf_w_pw2, conf_b_pw2=_as_rows(conf_b_pw2),
        sc_w_in=sc_w_in, sc_w_conv=sc_w_conv, sc_w_out=sc_w_out,
        nsa_w_in=nsa_w_in, nsa_k_pe=nsa_k_pe, nsa_k_w1=nsa_k_w1, nsa_k_w2=nsa_k_w2,
        nsa_v_pe=nsa_v_pe, nsa_v_w1=nsa_v_w1, nsa_v_w2=nsa_v_w2, nsa_w_out=nsa_w_out,
        rel_table=rel_table,
        dense_w_gate=dense_w_gate, dense_w_up=dense_w_up, dense_w_down=dense_w_down,
        moe_w_router=moe_w_router, moe_b_router=moe_b_router,
        moe_w_gate=moe_w_gate, moe_w_up=moe_w_up, moe_w_down=moe_w_down,
    )
    mix_mod, ffn_mod = _adaln(c, mix_ada_w, mix_ada_b, ffn_ada_w, ffn_ada_b)
    mixers = (_conformer_layer, _short_conv_layer, _nsa_layer)
    for i in range(DEPTH):
        x = mixers[i % 3](x, mix_mod, i, i // 3, p)
        ffn = _dense_ffn_layer if i % 2 == 0 else _moe_ffn_layer
        x = ffn(x, ffn_mod, i, i // 2, p)
    return x
```

```python
import functools
import math

import numpy as np
import jax
import jax.numpy as jnp
from jax import lax
from jax.experimental import pallas as pl
from jax.experimental.pallas import tpu as pltpu

F32 = jnp.float32
BF16 = jnp.bfloat16
I32 = jnp.int32

D_MODEL = 1024
DEPTH = 4
RMS_EPS = 1e-6
LN_EPS = 1e-5
NEG = -1e30

CONF_KERNEL = 31
SHORT_KERNEL = 3
N_HEADS = 16
HEAD_DIM = 64
N_GROUPS = 4
HPG = 4
CMP_LEN = 32
CMP_STRIDE = 16
CMP_HIDDEN = 128
SEL_BLOCK = 64
SEL_TOPK = 16
N_LOCAL_BLOCKS = 2
WINDOW = 512
REL_BUCKETS = 32
REL_MAX_DIST = 2048
D_FF_DENSE = 2816
N_EXPERTS = 8
D_FF_EXPERT = 3584

LANES = 128
SUBLANES = 8
VMEM_LIMIT = 56 * 1024 * 1024

ADA_TN = 1024
TOK_TILE = 512
CONV_TILE = 256
CONV_HALO = 32
FFN_TILE = 1024
FFN_TF = 256
MOE_ROWS = 1024
ROUTE_TILE = 512
ROW_DMA_TILE = 256
ROW_DMA_UNROLL = 8
DEST_TILE = 2048
QT = 128
QL = HPG * QT
NEAR_SLOTS = 13
SLOT_FAR = 13
SLOT_WIN = 14
SLOT_NONE = 15
N_SLOTS = 16
CMP_NEAR = 112
SEL_SUB = 4
SEL_STREAMS = 2
SEL_PEN_ROWS = SEL_STREAMS * SEL_SUB * QT // SEL_BLOCK
K_ONE = HEAD_DIM
K_PEN = 80
V_ROWS = 80
LOG2E = 1.4426950408889634


def _bucket_thresholds():
    n = np.arange(0, 4 * REL_MAX_DIST)
    max_exact = REL_BUCKETS // 2
    nf = np.maximum(n, 1).astype(np.float32)
    large = max_exact + (np.log(nf / np.float32(max_exact))
                         / np.float32(math.log(REL_MAX_DIST / max_exact))
                         * np.float32(REL_BUCKETS - max_exact)).astype(np.int32)
    large = np.minimum(large, REL_BUCKETS - 1)
    b = np.where(n < max_exact, n, large)
    return [int(np.argmax(b >= k)) for k in range(REL_BUCKETS)]


BUCKET_THR = _bucket_thresholds()
assert BUCKET_THR[REL_BUCKETS - 1] <= QT * NEAR_SLOTS - (QT - 1)
assert BUCKET_THR[REL_BUCKETS - 1] <= CMP_STRIDE * (CMP_NEAR - 8 + 1) - CMP_LEN + 1


def _cparams(sem):
    return pltpu.CompilerParams(dimension_semantics=sem, vmem_limit_bytes=VMEM_LIMIT)


def _bdot(a, b):
    return jnp.dot(a.astype(BF16), b.astype(BF16), preferred_element_type=F32)


def _sigmoid(x):
    return 1.0 / (1.0 + jnp.exp(-x))


def _rms(x, g):
    return x * lax.rsqrt(jnp.mean(x * x, axis=-1, keepdims=True) + RMS_EPS) * g


def _norm_mod(x, g, mod):
    return _rms(x, g) * (1.0 + mod[1:2]) + mod[0:1]


def _adaln_kernel(c_ref, wm_ref, bm_ref, wf_ref, bf_ref, om_ref, of_ref):
    c = c_ref[...]
    s = c * _sigmoid(c)
    om_ref[0] = jnp.dot(s, wm_ref[0], precision=lax.Precision.HIGHEST,
                        preferred_element_type=F32) + bm_ref[0]
    of_ref[0] = jnp.dot(s, wf_ref[0], precision=lax.Precision.HIGHEST,
                        preferred_element_type=F32) + bf_ref[0]


def _adaln(c, mix_w, mix_b, ffn_w, ffn_b):
    B, D = c.shape
    depth = mix_w.shape[0]
    cp = jnp.zeros((SUBLANES, D), F32).at[:B].set(c)
    w_spec = pl.BlockSpec((1, D, ADA_TN), lambda l, j: (l, 0, j))
    b_spec = pl.BlockSpec((1, 1, ADA_TN), lambda l, j: (l, 0, j))
    o_spec = pl.BlockSpec((1, SUBLANES, ADA_TN), lambda l, j: (l, 0, j))
    om, of = pl.pallas_call(
        _adaln_kernel,
        grid=(depth, 3 * D // ADA_TN),
        in_specs=[pl.BlockSpec((SUBLANES, D), lambda l, j: (0, 0)), w_spec, b_spec, w_spec, b_spec],
        out_specs=[o_spec, o_spec],
        out_shape=[jax.ShapeDtypeStruct((depth, SUBLANES, 3 * D), F32)] * 2,
        compiler_params=_cparams(("arbitrary", "arbitrary")),
        name="adaln",
    )(cp, mix_w, mix_b.reshape(depth, 1, 3 * D), ffn_w, ffn_b.reshape(depth, 1, 3 * D))
    return (om[:, :B].reshape(depth, B, 3, D), of[:, :B].reshape(depth, B, 3, D))


def _mod_spec(layer):
    return pl.BlockSpec((1, 1, 3, D_MODEL), lambda b, *_: (layer, b, 0, 0))


def _vec_spec(layer, width=D_MODEL):
    return pl.BlockSpec((1, 1, width), lambda *_: (layer, 0, 0))


def _post_residual(x, y, post_g, mod):
    return x + mod[2:3] * _rms(y, post_g)


def _conf_a_kernel(x_ref, mod_ref, g_ref, w_ref, b_ref, u_ref):
    D = D_MODEL
    h = _norm_mod(x_ref[0], g_ref[0], mod_ref[0, 0])
    z = _bdot(h, w_ref[...]) + b_ref[0]
    u_ref[0] = z[:, :D] * _sigmoid(z[:, D:])


def _conf_b_kernel(u_ref, up_ref, x_ref, mod_ref, wdw_ref, bdw_ref, lng_ref, lnb_ref,
                   w2_ref, b2_ref, pg_ref, o_ref, ext_ref, conv_ref):
    T, D, K = CONV_TILE, D_MODEL, CONF_KERNEL
    i = pl.program_id(1)
    keep = (i > 0).astype(F32)
    ext_ref[0:CONV_HALO, :] = up_ref[0] * keep
    ext_ref[CONV_HALO:, :] = u_ref[0]
    rb, cb = 64, 256
    off = CONV_HALO - (K - 1)

    def col_body(c, carry):
        c0 = pl.multiple_of(c * cb, cb)
        bias = bdw_ref[0, :, pl.ds(c0, cb)]
        for r in range(T // rb):
            acc = jnp.zeros((rb // SUBLANES, SUBLANES, cb), F32) + bias[None]
            for res in range(SUBLANES):
                offs = [o for o in range(off, off + K) if o % SUBLANES == res]
                lo, hi = offs[0], offs[-1]
                win = ext_ref[pl.ds(r * rb + lo, hi - lo + rb), pl.ds(c0, cb)]
                win = win.reshape((hi - lo + rb) // SUBLANES, SUBLANES, cb)
                for o in offs:
                    i0 = (o - lo) // SUBLANES
                    w = wdw_ref[0, o - off, :, pl.ds(c0, cb)]
                    acc = acc + w[None] * win[i0:i0 + rb // SUBLANES]
            conv_ref[pl.ds(r * rb, rb), pl.ds(c0, cb)] = acc.reshape(rb, cb)
        return carry

    lax.fori_loop(0, D // cb, col_body, 0)
    v = conv_ref[...]
    mu = jnp.mean(v, axis=-1, keepdims=True)
    var = jnp.mean(jnp.square(v - mu), axis=-1, keepdims=True)
    ln = (v - mu) * lax.rsqrt(var + LN_EPS) * lng_ref[0] + lnb_ref[0]
    act = ln * _sigmoid(ln)
    y = _bdot(act, w2_ref[...]) + b2_ref[0]
    o_ref[0] = _post_residual(x_ref[0], y, pg_ref[0], mod_ref[0, 0])


def _conformer_layer(x, mod, layer, j, p):
    B, S, D = x.shape
    T = TOK_TILE
    u = pl.pallas_call(
        _conf_a_kernel,
        grid=(B, S // T),
        in_specs=[pl.BlockSpec((1, T, D), lambda b, i: (b, i, 0)), _mod_spec(layer),
                  _vec_spec(layer), pl.BlockSpec((D, 2 * D), lambda b, i: (0, 0)),
                  _vec_spec(j, 2 * D)],
        out_specs=pl.BlockSpec((1, T, D), lambda b, i: (b, i, 0)),
        out_shape=jax.ShapeDtypeStruct((B, S, D), F32),
        compiler_params=_cparams(("arbitrary", "arbitrary")),
        name="conf_pw1_glu",
    )(x, mod, p["mix_pre_g"], p["conf_w_pw1"][j].astype(BF16), p["conf_b_pw1"])
    T = CONV_TILE
    hb = T // CONV_HALO
    return pl.pallas_call(
        _conf_b_kernel,
        grid=(B, S // T),
        in_specs=[pl.BlockSpec((1, T, D), lambda b, i: (b, i, 0)),
                  pl.BlockSpec((1, CONV_HALO, D), lambda b, i: (b, jnp.maximum(i * hb - 1, 0), 0)),
                  pl.BlockSpec((1, T, D), lambda b, i: (b, i, 0)), _mod_spec(layer),
                  pl.BlockSpec((1, CONF_KERNEL, SUBLANES, D), lambda b, i: (j, 0, 0, 0)),
                  _vec_spec(j), _vec_spec(j), _vec_spec(j),
                  pl.BlockSpec((D, D), lambda b, i: (0, 0)), _vec_spec(j), _vec_spec(layer)],
        out_specs=pl.BlockSpec((1, T, D), lambda b, i: (b, i, 0)),
        out_shape=jax.ShapeDtypeStruct((B, S, D), F32),
        scratch_shapes=[pltpu.VMEM((T + CONV_HALO, D), F32), pltpu.VMEM((T, D), F32)],
        compiler_params=_cparams(("arbitrary", "arbitrary")),
        name="conf_conv_pw2",
    )(u, u, x, mod, jnp.broadcast_to(p["conf_w_dw"][:, :, None, :], (*p["conf_w_dw"].shape[:2], SUBLANES, D)),
      p["conf_b_dw"], p["conf_ln_g"], p["conf_ln_b"],
      p["conf_w_pw2"][j].astype(BF16), p["conf_b_pw2"], p["mix_post_g"])


def _short_conv_kernel(x_ref, mod_ref, g_ref, win_ref, wc_ref, wout_ref, pg_ref, o_ref, ext_ref):
    T, D = TOK_TILE, D_MODEL
    i = pl.program_id(1)

    @pl.when(i == 0)
    def _():
        ext_ref[0:SUBLANES, :] = jnp.zeros((SUBLANES, D), F32)

    x = x_ref[0]
    mod = mod_ref[0, 0]
    h = _norm_mod(x, g_ref[0], mod)
    z = _bdot(h, win_ref[...])
    v, gate_b, gate_c = z[:, :D], z[:, D:2 * D], z[:, 2 * D:]
    ext_ref[SUBLANES:, :] = gate_c * v
    wc = wc_ref[0]
    u = (wc[0:1] * ext_ref[pl.ds(SUBLANES - 2, T), :] + wc[1:2] * ext_ref[pl.ds(SUBLANES - 1, T), :]
         + wc[2:3] * ext_ref[pl.ds(SUBLANES, T), :])
    ext_ref[0:SUBLANES, :] = ext_ref[pl.ds(T, SUBLANES), :]
    y = _bdot(gate_b * u, wout_ref[...])
    o_ref[0] = _post_residual(x, y, pg_ref[0], mod)


def _short_conv_layer(x, mod, layer, j, p):
    B, S, D = x.shape
    T = TOK_TILE
    return pl.pallas_call(
        _short_conv_kernel,
        grid=(B, S // T),
        in_specs=[pl.BlockSpec((1, T, D), lambda b, i: (b, i, 0)), _mod_spec(layer), _vec_spec(layer),
                  pl.BlockSpec((D, 3 * D), lambda b, i: (0, 0)),
                  pl.BlockSpec((1, SHORT_KERNEL, D), lambda b, i: (j, 0, 0)),
                  pl.BlockSpec((D, D), lambda b, i: (0, 0)), _vec_spec(layer)],
        out_specs=pl.BlockSpec((1, T, D), lambda b, i: (b, i, 0)),
        out_shape=jax.ShapeDtypeStruct((B, S, D), F32),
        scratch_shapes=[pltpu.VMEM((T + SUBLANES, D), F32)],
        compiler_params=_cparams(("arbitrary", "arbitrary")),
        name="short_conv",
    )(x, mod, p["mix_pre_g"], p["sc_w_in"][j].astype(BF16), p["sc_w_conv"],
      p["sc_w_out"][j].astype(BF16), p["mix_post_g"])


def _swiglu_step(h, wg_ref, wu_ref, wd_ref, acc_ref):
    a = _bdot(h, wg_ref[...])
    u = _bdot(h, wu_ref[...])
    acc_ref[...] += _bdot((a * _sigmoid(a)) * u, wd_ref[...])


def _dense_ffn_kernel(x_ref, mod_ref, g_ref, wg_ref, wu_ref, wd_ref, pg_ref, o_ref, h_ref, acc_ref):
    f = pl.program_id(2)

    @pl.when(f == 0)
    def _():
        h_ref[...] = _norm_mod(x_ref[0], g_ref[0], mod_ref[0, 0]).astype(BF16)
        acc_ref[...] = jnp.zeros_like(acc_ref)

    _swiglu_step(h_ref[...], wg_ref.at[0], wu_ref.at[0], wd_ref.at[0], acc_ref)

    @pl.when(f == pl.num_programs(2) - 1)
    def _():
        o_ref[0] = _post_residual(x_ref[0], acc_ref[...], pg_ref[0], mod_ref[0, 0])


def _dense_ffn_layer(x, mod, layer, j, p):
    B, S, D = x.shape
    T, TF, F = min(FFN_TILE, S), FFN_TF, D_FF_DENSE
    return pl.pallas_call(
        _dense_ffn_kernel,
        grid=(B, S // T, F // TF),
        in_specs=[pl.BlockSpec((1, T, D), lambda b, i, f: (b, i, 0)), _mod_spec(layer), _vec_spec(layer),
                  pl.BlockSpec((1, D, TF), lambda b, i, f: (j, 0, f)),
                  pl.BlockSpec((1, D, TF), lambda b, i, f: (j, 0, f)),
                  pl.BlockSpec((1, TF, D), lambda b, i, f: (j, f, 0)), _vec_spec(layer)],
        out_specs=pl.BlockSpec((1, T, D), lambda b, i, f: (b, i, 0)),
        out_shape=jax.ShapeDtypeStruct((B, S, D), F32),
        scratch_shapes=[pltpu.VMEM((T, D), BF16), pltpu.VMEM((T, D), F32)],
        compiler_params=_cparams(("arbitrary", "arbitrary", "arbitrary")),
        name="dense_swiglu",
    )(x, mod, p["ffn_pre_g"], p["dense_w_gate"], p["dense_w_up"], p["dense_w_down"], p["ffn_post_g"])


def _router_kernel(x_ref, mod_ref, g_ref, wr_ref, br_ref, h_ref, ri_ref, rw_ref, cnt_ref, base_ref):
    T = ROUTE_TILE
    i = pl.program_id(0)

    @pl.when(i == 0)
    def _():
        base_ref[...] = jnp.zeros_like(base_ref)

    h = _norm_mod(x_ref[...], g_ref[0], mod_ref[0, 0])
    h_ref[...] = h
    logits = jnp.dot(h, wr_ref[...], precision=lax.Precision.HIGHEST,
                     preferred_element_type=F32) + br_ref[0]
    lane = lax.broadcasted_iota(I32, (T, LANES), 1).astype(F32)
    logits = jnp.where(lane < N_EXPERTS, logits, -jnp.inf)
    l0 = jnp.max(logits, axis=1, keepdims=True)
    e0 = jnp.min(jnp.where(logits == l0, lane, float(LANES)), axis=1, keepdims=True)
    rest = jnp.where(lane == e0, -jnp.inf, logits)
    l1 = jnp.max(rest, axis=1, keepdims=True)
    e1 = jnp.min(jnp.where(rest == l1, lane, float(LANES)), axis=1, keepdims=True)
    ex = jnp.exp(l1 - l0)
    w0 = 1.0 / (1.0 + ex)
    w1 = ex / (1.0 + ex)
    onehot = ((lane == e0) | (lane == e1)).astype(F32)
    row = lax.broadcasted_iota(I32, (T, T), 0)
    col = lax.broadcasted_iota(I32, (T, T), 1)
    tri = jnp.where(row > col, 1.0, 0.0).astype(BF16)
    before = jnp.dot(tri, onehot.astype(BF16), preferred_element_type=F32) + base_ref[...]
    r0 = jnp.sum(jnp.where(lane == e0, before, 0.0), axis=1, keepdims=True)
    r1 = jnp.sum(jnp.where(lane == e1, before, 0.0), axis=1, keepdims=True)
    base_ref[...] += jnp.sum(onehot, axis=0, keepdims=True)
    cnt_ref[...] = base_ref[...].astype(I32)
    ri = jnp.where(lane == 0, e0, jnp.where(lane == 1, e1, jnp.where(lane == 2, r0, r1)))
    ri_ref[...] = ri[:, :SUBLANES].astype(I32)
    rw_ref[...] = jnp.where(lane == 0, w0, w1)[:, :SUBLANES]


def _dest_kernel(pstart_ref, ri_ref, dst_ref):
    ri = ri_ref[...]
    lane = lax.broadcasted_iota(I32, ri.shape, 1)
    expert = jnp.where(lane == 0, ri[:, 0:1], ri[:, 1:2])
    rank = jnp.where(lane == 0, ri[:, 2:3], ri[:, 3:4])
    start = jnp.zeros(ri.shape, I32)
    for e in range(N_EXPERTS):
        start = jnp.where(expert == e, pstart_ref[e], start)
    dst_ref[...] = start + rank


def _dispatch_kernel(dst_ref, h_ref, zero_ref, buf_ref, sem):
    del zero_ref
    T = ROW_DMA_TILE
    i = pl.program_id(0)
    slot = i % 2

    def row_copy(tok, dest, s):
        return pltpu.make_async_copy(h_ref.at[pl.ds(tok, 1)], buf_ref.at[pl.ds(dest, 1)], sem.at[s])

    def issue(t, carry):
        for k in range(2):
            row_copy(i * T + t, dst_ref[t, k], slot).start()
        return carry

    lax.fori_loop(0, T, issue, 0, unroll=ROW_DMA_UNROLL)

    def drain(s):
        def body(t, carry):
            for k in range(2):
                row_copy(0, 0, s).wait()
            return carry

        lax.fori_loop(0, T, body, 0, unroll=ROW_DMA_UNROLL)

    @pl.when(i > 0)
    def _():
        drain(1 - slot)

    @pl.when(i == pl.num_programs(0) - 1)
    def _():
        drain(slot)


def _expert_kernel(blk_e_ref, nblk_ref, xb_ref, wg_ref, wu_ref, wd_ref, o_ref, h_ref, acc_ref):
    del blk_e_ref
    i = pl.program_id(0)
    f = pl.program_id(1)
    live = i < nblk_ref[0]

    @pl.when(f == 0)
    def _():
        h_ref[...] = xb_ref[...].astype(BF16)
        acc_ref[...] = jnp.zeros_like(acc_ref)

    @pl.when(live)
    def _():
        _swiglu_step(h_ref[...], wg_ref.at[0, 0], wu_ref.at[0, 0], wd_ref.at[0, 0], acc_ref)

    @pl.when(f == pl.num_programs(1) - 1)
    def _():
        o_ref[...] = acc_ref[...]


def _combine_kernel(x_ref, mod_ref, pg_ref, dst_ref, dst_next_ref, rw_ref, yb_ref, o_ref, y_ref, sem):
    T = ROW_DMA_TILE
    i = pl.program_id(0)
    slot = i % 2

    def row_copy(src, t, k, s):
        return pltpu.make_async_copy(yb_ref.at[pl.ds(src, 1)], y_ref.at[s, k, pl.ds(t, 1)], sem.at[s])

    def gather(idx_ref, s):
        def body(t, carry):
            for k in range(2):
                row_copy(idx_ref[t, k], t, k, s).start()
            return carry

        lax.fori_loop(0, T, body, 0, unroll=ROW_DMA_UNROLL)

    @pl.when(i == 0)
    def _():
        gather(dst_ref, slot)

    @pl.when(i + 1 < pl.num_programs(0))
    def _():
        gather(dst_next_ref, 1 - slot)

    def drain(t, carry):
        for k in range(2):
            row_copy(0, 0, k, slot).wait()
        return carry

    lax.fori_loop(0, T, drain, 0, unroll=ROW_DMA_UNROLL)
    rw = rw_ref[...]
    y = rw[:, 0:1] * y_ref[slot, 0] + rw[:, 1:2] * y_ref[slot, 1]
    o_ref[...] = _post_residual(x_ref[...], y, pg_ref[0], mod_ref[0, 0])


def _moe_ffn_layer(x, mod, layer, j, p):
    B, S, D = x.shape
    N = B * S
    T = ROUTE_TILE
    tiles_per_seq = S // T
    xt = x.reshape(N, D)
    wr = jnp.zeros((D, LANES), F32).at[:, :N_EXPERTS].set(p["moe_w_router"][j])
    br = jnp.zeros((1, 1, LANES), F32).at[0, 0, :N_EXPERTS].set(p["moe_b_router"][j])
    h, ri, rw, cnt = pl.pallas_call(
        _router_kernel,
        grid=(N // T,),
        in_specs=[pl.BlockSpec((T, D), lambda i: (i, 0)),
                  pl.BlockSpec((1, 1, 3, D), lambda i: (layer, i // tiles_per_seq, 0, 0)),
                  _vec_spec(layer), pl.BlockSpec((D, LANES), lambda i: (0, 0)), _vec_spec(0, LANES)],
        out_specs=[pl.BlockSpec((T, D), lambda i: (i, 0)), pl.BlockSpec((T, SUBLANES), lambda i: (i, 0)),
                   pl.BlockSpec((T, SUBLANES), lambda i: (i, 0)), pl.BlockSpec((1, LANES), lambda i: (0, 0))],
        out_shape=[jax.ShapeDtypeStruct((N, D), F32), jax.ShapeDtypeStruct((N, SUBLANES), I32),
                   jax.ShapeDtypeStruct((N, SUBLANES), F32), jax.ShapeDtypeStruct((1, LANES), I32)],
        scratch_shapes=[pltpu.VMEM((1, LANES), F32)],
        compiler_params=_cparams(("arbitrary",)),
        name="moe_router",
    )(xt, mod, p["ffn_pre_g"], wr, br)

    R = MOE_ROWS
    counts = cnt[0, :N_EXPERTS]
    padded = (counts + R - 1) // R * R
    pend = jnp.cumsum(padded)
    pstart = (pend - padded).astype(I32)
    n_blocks = -(-(2 * N + N_EXPERTS * (R - 1)) // R)
    blk_e = jnp.minimum(jnp.sum((jnp.arange(n_blocks) * R)[:, None] >= pend[None, :], axis=1),
                        N_EXPERTS - 1).astype(I32)
    n_live = (pend[-1] // R).astype(I32).reshape(1)
    rows = n_blocks * R

    TR = DEST_TILE
    dst = pl.pallas_call(
        _dest_kernel,
        grid_spec=pltpu.PrefetchScalarGridSpec(
            num_scalar_prefetch=1,
            grid=(N // TR,),
            in_specs=[pl.BlockSpec((TR, SUBLANES), lambda i, ps: (i, 0))],
            out_specs=pl.BlockSpec((TR, SUBLANES), lambda i, ps: (i, 0)),
        ),
        out_shape=jax.ShapeDtypeStruct((N, SUBLANES), I32),
        compiler_params=_cparams(("arbitrary",)),
        name="moe_dest",
    )(pstart, ri)

    TD = ROW_DMA_TILE
    n_dma_tiles = N // TD
    dst_spec = pl.BlockSpec((TD, SUBLANES), lambda i: (i, 0), memory_space=pltpu.SMEM)
    buf = pl.pallas_call(
        _dispatch_kernel,
        grid=(n_dma_tiles,),
        in_specs=[dst_spec, pl.BlockSpec(memory_space=pl.ANY), pl.BlockSpec(memory_space=pl.ANY)],
        out_specs=pl.BlockSpec(memory_space=pl.ANY),
        out_shape=jax.ShapeDtypeStruct((rows, D), F32),
        scratch_shapes=[pltpu.SemaphoreType.DMA((2,))],
        input_output_aliases={2: 0},
        compiler_params=_cparams(("arbitrary",)),
        name="moe_dispatch",
    )(dst, h, jnp.zeros((rows, D), F32))

    TF, F = FFN_TF, D_FF_EXPERT
    yb = pl.pallas_call(
        _expert_kernel,
        grid_spec=pltpu.PrefetchScalarGridSpec(
            num_scalar_prefetch=2,
            grid=(n_blocks, F // TF),
            in_specs=[pl.BlockSpec((R, D), lambda i, f, be, nb: (i, 0)),
                      pl.BlockSpec((1, 1, D, TF), lambda i, f, be, nb: (j, be[i], 0, jnp.where(i < nb[0], f, 0))),
                      pl.BlockSpec((1, 1, D, TF), lambda i, f, be, nb: (j, be[i], 0, jnp.where(i < nb[0], f, 0))),
                      pl.BlockSpec((1, 1, TF, D), lambda i, f, be, nb: (j, be[i], jnp.where(i < nb[0], f, 0), 0))],
            out_specs=pl.BlockSpec((R, D), lambda i, f, be, nb: (i, 0)),
            scratch_shapes=[pltpu.VMEM((R, D), BF16), pltpu.VMEM((R, D), F32)],
        ),
        out_shape=jax.ShapeDtypeStruct((rows, D), F32),
        compiler_params=_cparams(("arbitrary", "arbitrary")),
        name="moe_experts",
    )(blk_e, n_live, buf, p["moe_w_gate"], p["moe_w_up"], p["moe_w_down"])

    tiles_per_seq_d = S // TD
    out = pl.pallas_call(
        _combine_kernel,
        grid=(n_dma_tiles,),
        in_specs=[pl.BlockSpec((TD, D), lambda i: (i, 0)),
                  pl.BlockSpec((1, 1, 3, D), lambda i: (layer, i // tiles_per_seq_d, 0, 0)),
                  pl.BlockSpec((1, 1, D), lambda i: (layer, 0, 0)),
                  dst_spec,
                  pl.BlockSpec((TD, SUBLANES), lambda i: (jnp.minimum(i + 1, n_dma_tiles - 1), 0),
                               memory_space=pltpu.SMEM),
                  pl.BlockSpec((TD, SUBLANES), lambda i: (i, 0)),
                  pl.BlockSpec(memory_space=pl.ANY)],
        out_specs=pl.BlockSpec((TD, D), lambda i: (i, 0)),
        out_shape=jax.ShapeDtypeStruct((N, D), F32),
        scratch_shapes=[pltpu.VMEM((2, 2, TD, D), F32), pltpu.SemaphoreType.DMA((2,))],
        compiler_params=_cparams(("arbitrary",)),
        name="moe_combine",
    )(xt, mod, p["ffn_post_g"], dst, dst, rw, yb)
    return out.reshape(B, S, D)


NSA_W = D_MODEL + 2 * N_GROUPS * LANES + 4 * N_GROUPS * HEAD_DIM + LANES


def _ones_rows(n):
    r = lax.broadcasted_iota(I32, (V_ROWS - HEAD_DIM, n), 0)
    return jnp.where(r == 0, 1.0, 0.0)


def _nsa_in_kernel(x_ref, mod_ref, g_ref, w_ref, q_ref, ks_ref, vst_ref, kw_ref, vwt_ref,
                   kc_ref, vc_ref, gt_ref):
    D, G, DH = D_MODEL, N_GROUPS, HEAD_DIM
    KV = G * DH
    T = x_ref.shape[1]
    h = _norm_mod(x_ref[0], g_ref[0], mod_ref[0, 0])
    z = _bdot(h, w_ref[...])
    q_ref[0] = z[:, :D].astype(BF16)
    lane = lax.broadcasted_iota(I32, (T, LANES), 1)
    pos = pl.program_id(1) * T + lax.broadcasted_iota(I32, (T, LANES), 0)
    one_col = jnp.where(lane == K_ONE, 1.0, 0.0)
    blk_col = jnp.where(lane == K_PEN + (pos // SEL_BLOCK) % SEL_PEN_ROWS, 1.0, 0.0)
    o = D
    for g in range(G):
        ks_ref[0, g] = (z[:, o + g * LANES:o + (g + 1) * LANES] + (one_col + blk_col)).astype(BF16)
    o += G * LANES
    for g in range(G):
        kw_ref[0, g] = (z[:, o + g * LANES:o + (g + 1) * LANES] + one_col).astype(BF16)
    o += G * LANES
    v_s, v_w = z[:, o:o + KV], z[:, o + KV:o + 2 * KV]
    k_c, v_c = z[:, o + 2 * KV:o + 3 * KV], z[:, o + 3 * KV:o + 4 * KV]
    gates = _sigmoid(z[:, o + 4 * KV:])
    vst = v_s.T
    vwt = v_w.T
    ones = _ones_rows(T)
    for g in range(G):
        sl = slice(g * DH, (g + 1) * DH)
        kc_ref[0, g] = k_c[:, sl]
        vc_ref[0, g] = v_c[:, sl]
        vst_ref[0, g] = jnp.concatenate([vst[sl, :], ones], axis=0).astype(BF16)
        vwt_ref[0, g] = jnp.concatenate([vwt[sl, :], ones], axis=0).astype(BF16)
    gt_ref[0] = gates.T[:3 * N_HEADS, :]


def _gelu_tanh(x):
    return x * (0.5 * (1.0 + jnp.tanh(math.sqrt(2.0 / math.pi) * (x + 0.044715 * (x * x * x)))))


def _compress_kernel(kc_ref, vc_ref, kpe_ref, kw1_ref, kw2_ref, vpe_ref, vw1_ref, vw2_ref,
                     kco_ref, vcto_ref):
    nc = kc_ref.shape[2]

    def comp(c, pe_ref, w1_ref, w2_ref):
        a = _bdot(c + pe_ref[0:1], w1_ref[0])
        b = _bdot(c + pe_ref[1:2], w1_ref[1])
        b_next = pltpu.roll(b, nc - 1, axis=0)
        out = _bdot(_gelu_tanh(a + b_next), w2_ref[...])
        row = lax.broadcasted_iota(I32, out.shape, 0)
        return jnp.where(row < nc - 1, out, 0.0)

    kc = comp(kc_ref[0, 0], kpe_ref, kw1_ref, kw2_ref)
    lane = lax.broadcasted_iota(I32, kc.shape, 1)
    kco_ref[0, 0] = (kc + jnp.where(lane == K_ONE, 1.0, 0.0)).astype(BF16)
    vct = comp(vc_ref[0, 0], vpe_ref, vw1_ref, vw2_ref).T
    vcto_ref[0, 0] = jnp.concatenate([vct, _ones_rows(nc)], axis=0).astype(BF16)


def _bias_table_kernel(tbl_ref, tab_ref, ctab_ref, far_ref):
    g = pl.program_id(0)

    def bias_of(dist, h, far):
        v = jnp.full(dist.shape, tbl_ref[0, h], F32)
        for b in range(1, REL_BUCKETS):
            v = jnp.where(dist >= BUCKET_THR[b], tbl_ref[b, h], v)
        return jnp.where(dist >= 0, v * LOG2E - far, NEG)

    nc = (ctab_ref.shape[1] - LANES) // 2
    key = lax.broadcasted_iota(I32, (QT, QT), 0)
    qry = lax.broadcasted_iota(I32, (QT, QT), 1)
    crow = lax.broadcasted_iota(I32, (CMP_NEAR, QT), 0)
    cqry = lax.broadcasted_iota(I32, (CMP_NEAR, QT), 1)
    for hh in range(HPG):
        h = g * HPG + hh
        lanes = slice(hh * QT, (hh + 1) * QT)
        far = jnp.full((1, QT), tbl_ref[REL_BUCKETS - 1, h] * LOG2E, F32).astype(BF16).astype(F32)
        far_ref[0, :, lanes] = far

        def near_slot(slot, carry, h=h, lanes=lanes, far=far):
            tab_ref[0, slot, :, lanes] = bias_of(slot * QT + qry - key, h, far)
            return carry

        lax.fori_loop(0, NEAR_SLOTS, near_slot, 0)
        tab_ref[0, SLOT_FAR, :, lanes] = jnp.zeros((QT, QT), F32)
        tab_ref[0, SLOT_NONE, :, lanes] = jnp.full((QT, QT), NEG, F32)
        dw = WINDOW + qry - key
        tab_ref[0, SLOT_WIN, :, lanes] = jnp.where(dw < WINDOW, bias_of(dw, h, far), NEG)
        ctab_ref[0, 0:nc, lanes] = jnp.zeros((nc, QT), F32)
        dc = cqry - CMP_STRIDE * (crow - (CMP_NEAR - SUBLANES)) - (CMP_LEN - 1)
        ctab_ref[0, nc:nc + CMP_NEAR, lanes] = bias_of(dc, h, far)
        ctab_ref[0, nc + CMP_NEAR:, lanes] = jnp.full((nc + LANES - CMP_NEAR, QT), NEG, F32)


def _nsa_attn_kernel(q_ref, ks_ref, vst_ref, kw_refs, vwt_refs, kc_ref, vct_ref, gt_ref,
                     tab_ref, ctab_ref, far_ref, o_ref, qt_ref, s_even_ref, s_odd_ref, psum_ref, pen_ref, ocw_ref):
    DH = HEAD_DIM
    a = pl.program_id(2)
    nc = kc_ref.shape[2]
    n_sel = pen_ref.shape[0]

    qT = q_ref[0].astype(F32).T
    aug_row = lax.broadcasted_iota(I32, (LANES - DH, QL), 0)
    aug = jnp.where(aug_row == 0, far_ref[0], 0.0).astype(BF16)
    for i in range(2):
        for hh in range(HPG):
            qt_ref[i, 0:DH, hh * QT:(hh + 1) * QT] = qT[hh * DH:(hh + 1) * DH, :].astype(BF16)
        qt_ref[i, DH:, :] = aug
    qt = qt_ref[0]

    def window():
        nw = WINDOW // QT + 1
        s_w = []
        for d in range(nw):
            slot = SLOT_WIN if d == nw - 1 else d
            if d > 0:
                slot = jnp.where(a >= d, slot, SLOT_NONE)
            s_w.append(jnp.dot(kw_refs[d][0, 0], qt, preferred_element_type=F32) + tab_ref[0, slot])
        m_w = functools.reduce(jnp.maximum, [jnp.max(s_d, axis=0, keepdims=True) for s_d in s_w])
        acc_w = functools.reduce(lambda x, y: x + y, [
            jnp.dot(vwt_refs[d][0, 0], jnp.exp2(s_w[d] - m_w).astype(BF16), preferred_element_type=F32)
            for d in range(nw)])
        ocw_ref[1] = acc_w[0:DH] * (1.0 / acc_w[DH:DH + 1])

    n_hi = (QT // CMP_STRIDE) * (a + 1)
    c0 = pl.multiple_of(nc - (n_hi - CMP_NEAR), SUBLANES)
    ratio = SEL_BLOCK // CMP_STRIDE

    def compressed_and_select(rows_c):
        rows_s = rows_c // ratio
        s = (jnp.dot(kc_ref[0, 0, 0:rows_c, :], qt, preferred_element_type=F32)
             + ctab_ref[0, pl.ds(c0, rows_c), :])
        m = jnp.max(s, axis=0, keepdims=True)
        e = jnp.exp2(s - m)
        l = jnp.sum(e, axis=0, keepdims=True)
        pn = e * jnp.where(m > 0.1 * NEG, 1.0 / l, 0.0)
        ocw_ref[0] = jnp.dot(vct_ref[0, 0, :, 0:rows_c], pn.astype(BF16), preferred_element_type=F32)[0:DH]
        psum_ref[0:rows_c, :] = pn[:, 0:QT] + pn[:, QT:2 * QT] + pn[:, 2 * QT:3 * QT] + pn[:, 3 * QT:4 * QT]

        imp = psum_ref[pl.ds(0, rows_s, stride=ratio), :]
        for jj in range(1, ratio):
            imp = imp + psum_ref[pl.ds(jj, rows_s, stride=ratio), :]
        last = psum_ref[pl.ds(ratio - 1, rows_s, stride=ratio), :]
        srow = lax.broadcasted_iota(I32, (rows_s, QT), 0)
        imp = imp + jnp.where(srow > 0, pltpu.roll(last, 1, axis=0), 0.0)
        lane_q = lax.broadcasted_iota(I32, (rows_s, QT), 1)
        cur = (QT // SEL_BLOCK) * a + lane_q // SEL_BLOCK
        back = cur - srow
        valid = back >= 0
        forced = (srow == 0) | (valid & (back < N_LOCAL_BLOCKS))
        n_forced = 1 + N_LOCAL_BLOCKS
        score = jnp.where(forced, -jnp.inf, jnp.where(valid, imp, -1.0))
        srow_f = srow.astype(F32)
        for _ in range(min(SEL_TOPK, n_sel) - n_forced):
            mx = jnp.max(score, axis=0, keepdims=True)
            first = jnp.min(jnp.where(score == mx, srow_f, float(rows_s)), axis=0, keepdims=True)
            score = jnp.where(srow_f == first, -jnp.inf, score)
        pen = jnp.where(score == -jnp.inf, 0.0, NEG)
        for hh in range(HPG):
            pen_ref[0:rows_s, hh * QT:(hh + 1) * QT] = pen
        if rows_s < n_sel:
            pen_ref[rows_s:, :] = jnp.full((n_sel - rows_s, QL), NEG, F32)
        window()

    sizes = [r for r in (nc // 8, nc // 4, nc // 2, nc) if r >= CMP_NEAR + 2 * SUBLANES]
    tiles_per = QT // CMP_STRIDE
    for idx, rows_c in enumerate(sizes):
        lo = 0 if idx == 0 else sizes[idx - 1] // tiles_per
        hi = rows_c // tiles_per
        pl.when((a >= lo) & (a < hi))(functools.partial(compressed_and_select, rows_c))
    o_c = ocw_ref[0]
    o_w = ocw_ref[1]

    big = SEL_SUB * QT
    span = SEL_STREAMS * big
    sub_per_step = SEL_STREAMS * SEL_SUB

    n_span = ks_ref.shape[2] // span

    def scores(step, buf):
        sc = jnp.minimum(step, n_span - 1)
        pen_rows = pen_ref[pl.ds(pl.multiple_of(sc * SEL_PEN_ROWS, SEL_PEN_ROWS), SEL_PEN_ROWS), :]
        qt_ref[buf, K_PEN:K_PEN + SEL_PEN_ROWS, :] = pen_rows.astype(BF16)
        k0 = pl.multiple_of(sc * span, span)
        return jnp.dot(ks_ref[0, 0, pl.ds(k0, span), :], qt_ref[buf], preferred_element_type=F32)

    def update(s_all, step, state, near):
        k0 = pl.multiple_of(jnp.minimum(step, n_span - 1) * span, span)
        out = []
        for t in range(SEL_STREAMS):
            m, acc = state[t]
            s = s_all[t * big:(t + 1) * big]
            if near:
                parts = []
                for u in range(SEL_SUB):
                    d = a - (step * sub_per_step + t * SEL_SUB + u)
                    slot = jnp.where(d >= 0, jnp.minimum(d, SLOT_FAR), SLOT_NONE)
                    parts.append(s[u * QT:(u + 1) * QT] + tab_ref[0, slot])
                s = jnp.concatenate(parts, axis=0)
            m_new = jnp.maximum(m, jnp.max(s, axis=0, keepdims=True))
            p = jnp.exp2(s - m_new).astype(BF16)
            acc = jnp.exp2(m - m_new) * acc + jnp.dot(vst_ref[0, 0, :, pl.ds(k0 + t * big, big)], p,
                                                      preferred_element_type=F32)
            out.append((m_new, acc))
        return tuple(out)

    def two_steps(jj, state, near):
        s_odd_ref[...] = scores(2 * jj + 1, 1)
        state = update(s_even_ref[...], 2 * jj, state, near)
        s_even_ref[...] = scores(2 * jj + 2, 0)
        return update(s_odd_ref[...], 2 * jj + 1, state, near)

    n_steps = a // sub_per_step + 1
    n_far2 = jnp.maximum(a - (NEAR_SLOTS - 1), 0) // (2 * sub_per_step)
    n_pairs = n_steps // 2
    s_even_ref[...] = scores(0, 0)
    state = tuple((jnp.full((1, QL), NEG, F32), jnp.zeros((V_ROWS, QL), F32)) for _ in range(SEL_STREAMS))
    state = lax.fori_loop(0, n_far2, functools.partial(two_steps, near=False), state)
    state = lax.fori_loop(n_far2, n_pairs, functools.partial(two_steps, near=True), state)
    state = lax.cond(n_steps % 2 == 1,
                     lambda st: update(s_even_ref[...], n_steps - 1, st, True),
                     lambda st: st, state)
    m_s = functools.reduce(jnp.maximum, [m for m, _ in state])
    acc_s = functools.reduce(lambda x, y: x + y, [jnp.exp2(m - m_s) * acc for m, acc in state])
    o_s = acc_s[0:DH] * (1.0 / acc_s[DH:DH + 1])

    def gate(branch):
        return jnp.concatenate([gt_ref[0, 0, 3 * hh + branch:3 * hh + branch + 1, :] for hh in range(HPG)],
                               axis=1)

    oT = gate(0) * o_c + gate(1) * o_s + gate(2) * o_w
    stacked = jnp.concatenate([oT[:, hh * QT:(hh + 1) * QT] for hh in range(HPG)], axis=0)
    o_ref[0] = stacked.T.astype(BF16)


def _nsa_attn_wrapper(*refs):
    nw = WINDOW // QT + 1
    q_ref, ks_ref, vst_ref = refs[0:3]
    kw_refs = refs[3:3 + nw]
    vwt_refs = refs[3 + nw:3 + 2 * nw]
    rest = refs[3 + 2 * nw:]
    _nsa_attn_kernel(q_ref, ks_ref, vst_ref, kw_refs, vwt_refs, *rest)


def _out_proj_kernel(o_ref, x_ref, mod_ref, w_ref, pg_ref, out_ref):
    y = jnp.dot(o_ref[0], w_ref[...], preferred_element_type=F32)
    out_ref[0] = _post_residual(x_ref[0], y, pg_ref[0], mod_ref[0, 0])


def _nsa_layer(x, mod, layer, j, p):
    B, S, D = x.shape
    G, DH, KV = N_GROUPS, HEAD_DIM, N_GROUPS * HEAD_DIM
    T = TOK_TILE
    NC = S // CMP_STRIDE
    n_sel = S // SEL_BLOCK

    w_in = p["nsa_w_in"][j]
    cuts = [D + i * KV for i in range(7)]
    wq, wkc, wvc, wks, wvs, wkw, wvw, wg = jnp.split(w_in, cuts, axis=-1)
    wg = jnp.zeros((D, LANES), F32).at[:, :3 * N_HEADS].set(wg)

    def slabs(w):
        return jnp.zeros((D, G, LANES), F32).at[:, :, :DH].set(w.reshape(D, G, DH)).reshape(D, G * LANES)

    w_all = jnp.concatenate([wq * (HEAD_DIM ** -0.5 * LOG2E), slabs(wks), slabs(wkw), wvs, wvw, wkc, wvc, wg],
                            axis=1).astype(BF16)

    k_spec = pl.BlockSpec((1, G, T, LANES), lambda b, i: (b, 0, i, 0))
    kc_spec = pl.BlockSpec((1, G, T, DH), lambda b, i: (b, 0, i, 0))
    vt_spec = pl.BlockSpec((1, G, V_ROWS, T), lambda b, i: (b, 0, 0, i))
    q, ks, vst, kw, vwt, kc_raw, vc_raw, gt = pl.pallas_call(
        _nsa_in_kernel,
        grid=(B, S // T),
        in_specs=[pl.BlockSpec((1, T, D), lambda b, i: (b, i, 0)), _mod_spec(layer), _vec_spec(layer),
                  pl.BlockSpec((D, NSA_W), lambda b, i: (0, 0))],
        out_specs=[pl.BlockSpec((1, T, D), lambda b, i: (b, i, 0)), k_spec, vt_spec, k_spec, vt_spec,
                   kc_spec, kc_spec, pl.BlockSpec((1, 3 * N_HEADS, T), lambda b, i: (b, 0, i))],
        out_shape=[jax.ShapeDtypeStruct((B, S, D), BF16),
                   jax.ShapeDtypeStruct((B, G, S, LANES), BF16), jax.ShapeDtypeStruct((B, G, V_ROWS, S), BF16),
                   jax.ShapeDtypeStruct((B, G, S, LANES), BF16), jax.ShapeDtypeStruct((B, G, V_ROWS, S), BF16),
                   jax.ShapeDtypeStruct((B, G, S, DH), F32), jax.ShapeDtypeStruct((B, G, S, DH), F32),
                   jax.ShapeDtypeStruct((B, 3 * N_HEADS, S), F32)],
        compiler_params=_cparams(("arbitrary", "arbitrary")),
        name="nsa_in_proj",
    )(x, mod, p["mix_pre_g"], w_all)

    half = CMP_LEN // 2
    chunk_w = half * DH

    def split_w1(w1):
        return w1.reshape(2, chunk_w, CMP_HIDDEN).astype(BF16)

    def split_pe(pe):
        return pe.reshape(2, chunk_w)

    c_spec = pl.BlockSpec((1, 1, NC, chunk_w), lambda b, g: (b, g, 0, 0))
    full = lambda shape: pl.BlockSpec(shape, lambda b, g: (0,) * len(shape))
    kc, vct = pl.pallas_call(
        _compress_kernel,
        grid=(B, G),
        in_specs=[c_spec, c_spec,
                  full((2, chunk_w)), full((2, chunk_w, CMP_HIDDEN)), full((CMP_HIDDEN, LANES)),
                  full((2, chunk_w)), full((2, chunk_w, CMP_HIDDEN)), full((CMP_HIDDEN, DH))],
        out_specs=[pl.BlockSpec((1, 1, NC, LANES), lambda b, g: (b, g, 0, 0)),
                   pl.BlockSpec((1, 1, V_ROWS, NC), lambda b, g: (b, g, 0, 0))],
        out_shape=[jax.ShapeDtypeStruct((B, G, NC, LANES), BF16),
                   jax.ShapeDtypeStruct((B, G, V_ROWS, NC), BF16)],
        compiler_params=_cparams(("arbitrary", "arbitrary")),
        name="nsa_compress",
    )(kc_raw.reshape(B, G, NC, chunk_w), vc_raw.reshape(B, G, NC, chunk_w),
      split_pe(p["nsa_k_pe"][j]), split_w1(p["nsa_k_w1"][j]),
      jnp.zeros((CMP_HIDDEN, LANES), BF16).at[:, :DH].set(p["nsa_k_w2"][j].astype(BF16)),
      split_pe(p["nsa_v_pe"][j]), split_w1(p["nsa_v_w1"][j]), p["nsa_v_w2"][j].astype(BF16))

    ctab_rows = 2 * NC + LANES
    tab, ctab, far = pl.pallas_call(
        _bias_table_kernel,
        grid=(G,),
        in_specs=[pl.BlockSpec(memory_space=pltpu.SMEM)],
        out_specs=[pl.BlockSpec((1, N_SLOTS, QT, QL), lambda g: (g, 0, 0, 0)),
                   pl.BlockSpec((1, ctab_rows, QL), lambda g: (g, 0, 0)),
                   pl.BlockSpec((1, 1, QL), lambda g: (g, 0, 0))],
        out_shape=[jax.ShapeDtypeStruct((G, N_SLOTS, QT, QL), F32),
                   jax.ShapeDtypeStruct((G, ctab_rows, QL), F32),
                   jax.ShapeDtypeStruct((G, 1, QL), F32)],
        compiler_params=_cparams(("arbitrary",)),
        name="nsa_bias_tables",
    )(p["rel_table"])

    nw = WINDOW // QT + 1
    kw_specs = [pl.BlockSpec((1, 1, QT, LANES), functools.partial(
        lambda b, g, a, d: (b, g, jnp.maximum(a - d, 0), 0), d=d)) for d in range(nw)]
    vwt_specs = [pl.BlockSpec((1, 1, V_ROWS, QT), functools.partial(
        lambda b, g, a, d: (b, g, 0, jnp.maximum(a - d, 0)), d=d)) for d in range(nw)]
    o = pl.pallas_call(
        _nsa_attn_wrapper,
        grid=(B, G, S // QT),
        in_specs=[pl.BlockSpec((1, QT, HPG * DH), lambda b, g, a: (b, a, g)),
                  pl.BlockSpec((1, 1, S, LANES), lambda b, g, a: (b, g, 0, 0)),
                  pl.BlockSpec((1, 1, V_ROWS, S), lambda b, g, a: (b, g, 0, 0)),
                  *kw_specs, *vwt_specs,
                  pl.BlockSpec((1, 1, NC, LANES), lambda b, g, a: (b, g, 0, 0)),
                  pl.BlockSpec((1, 1, V_ROWS, NC), lambda b, g, a: (b, g, 0, 0)),
                  pl.BlockSpec((1, 1, 3 * HPG, QT), lambda b, g, a: (b, g, 0, a)),
                  pl.BlockSpec((1, N_SLOTS, QT, QL), lambda b, g, a: (g, 0, 0, 0)),
                  pl.BlockSpec((1, ctab_rows, QL), lambda b, g, a: (g, 0, 0)),
                  pl.BlockSpec((1, 1, QL), lambda b, g, a: (g, 0, 0))],
        out_specs=pl.BlockSpec((1, QT, HPG * DH), lambda b, g, a: (b, a, g)),
        out_shape=jax.ShapeDtypeStruct((B, S, D), BF16),
        scratch_shapes=[pltpu.VMEM((2, LANES, QL), BF16),
                        pltpu.VMEM((SEL_STREAMS * SEL_SUB * QT, QL), F32),
                        pltpu.VMEM((SEL_STREAMS * SEL_SUB * QT, QL), F32),
                        pltpu.VMEM((NC, QT), F32), pltpu.VMEM((n_sel, QL), F32),
                        pltpu.VMEM((2, DH, QL), F32)],
        compiler_params=_cparams(("arbitrary", "arbitrary", "arbitrary")),
        name="nsa_attention",
    )(q, ks, vst, *([kw] * nw), *([vwt] * nw), kc, vct, gt.reshape(B, G, 3 * HPG, S), tab, ctab, far)

    return pl.pallas_call(
        _out_proj_kernel,
        grid=(B, S // T),
        in_specs=[pl.BlockSpec((1, T, D), lambda b, i: (b, i, 0)),
                  pl.BlockSpec((1, T, D), lambda b, i: (b, i, 0)), _mod_spec(layer),
                  pl.BlockSpec((D, D), lambda b, i: (0, 0)), _vec_spec(layer)],
        out_specs=pl.BlockSpec((1, T, D), lambda b, i: (b, i, 0)),
        out_shape=jax.ShapeDtypeStruct((B, S, D), F32),
        compiler_params=_cparams(("arbitrary", "arbitrary")),
        name="nsa_out_proj",
    )(o, x, mod, p["nsa_w_out"][j].astype(BF16), p["mix_post_g"])


def _as_rows(v):
    return v.reshape(v.shape[0], 1, v.shape[-1])


def kernel(x, c, mix_pre_g, mix_post_g, mix_ada_w, mix_ada_b, ffn_pre_g, ffn_post_g, ffn_ada_w, ffn_ada_b, conf_w_pw1, conf_b_pw1, conf_w_dw, conf_b_dw, conf_ln_g, conf_ln_b, conf_w_pw2, conf_b_pw2, sc_w_in, sc_w_conv, sc_w_out, nsa_w_in, nsa_k_pe, nsa_k_w1, nsa_k_w2, nsa_v_pe, nsa_v_w1, nsa_v_w2, nsa_w_out, rel_table, dense_w_gate, dense_w_up, dense_w_down, moe_w_router, moe_b_router, moe_w_gate, moe_w_up, moe_w_down):
    p = dict(
        mix_pre_g=_as_rows(mix_pre_g), mix_post_g=_as_rows(mix_post_g),
        ffn_pre_g=_as_rows(ffn_pre_g), ffn_post_g=_as_rows(ffn_post_g),
        conf_w_pw1=conf_w_pw1, conf_b_pw1=_as_rows(conf_b_pw1), conf_w_dw=conf_w_dw,
        conf_b_dw=_as_rows(conf_b_dw), conf_ln_g=_as_rows(conf_ln_g), conf_ln_b=_as_rows(conf_ln_b),
        conf_w_pw2=conf_w_pw2, conf_b_pw2=_as_rows(conf_b_pw2),
        sc_w_in=sc_w_in, sc_w_conv=sc_w_conv, sc_w_out=sc_w_out,
        nsa_w_in=nsa_w_in, nsa_k_pe=nsa_k_pe, nsa_k_w1=nsa_k_w1, nsa_k_w2=nsa_k_w2,
        nsa_v_pe=nsa_v_pe, nsa_v_w1=nsa_v_w1, nsa_v_w2=nsa_v_w2, nsa_w_out=nsa_w_out,
        rel_table=rel_table,
        dense_w_gate=dense_w_gate, dense_w_up=dense_w_up, dense_w_down=dense_w_down,
        moe_w_router=moe_w_router, moe_b_router=moe_b_router,
        moe_w_gate=moe_w_gate, moe_w_up=moe_w_up, moe_w_down=moe_w_down,
    )
    mix_mod, ffn_mod = _adaln(c, mix_ada_w, mix_ada_b, ffn_ada_w, ffn_ada_b)
    mixers = (_conformer_layer, _short_conv_layer, _nsa_layer)
    for i in range(DEPTH):
        x = mixers[i % 3](x, mix_mod, i, i // 3, p)
        ffn = _dense_ffn_layer if i % 2 == 0 else _moe_ffn_layer
        x = ffn(x, ffn_mod, i, i // 2, p)
    return x
```

```python
import functools
import math

import numpy as np
import jax
import jax.numpy as jnp
from jax import lax
from jax.experimental import pallas as pl
from jax.experimental.pallas import tpu as pltpu

F32 = jnp.float32
BF16 = jnp.bfloat16
I32 = jnp.int32

D_MODEL = 1024
DEPTH = 4
RMS_EPS = 1e-6
LN_EPS = 1e-5
NEG = -1e30

CONF_KERNEL = 31
SHORT_KERNEL = 3
N_HEADS = 16
HEAD_DIM = 64
N_GROUPS = 4
HPG = 4
CMP_LEN = 32
CMP_STRIDE = 16
CMP_HIDDEN = 128
SEL_BLOCK = 64
SEL_TOPK = 16
N_LOCAL_BLOCKS = 2
WINDOW = 512
REL_BUCKETS = 32
REL_MAX_DIST = 2048
D_FF_DENSE = 2816
N_EXPERTS = 8
D_FF_EXPERT = 3584

LANES = 128
SUBLANES = 8
VMEM_LIMIT = 56 * 1024 * 1024

ADA_TN = 1024
TOK_TILE = 512
CONV_TILE = 256
CONV_HALO = 32
FFN_TILE = 1024
FFN_TF = 256
MOE_TF = 512
MOE_ROWS = 1024
ROUTE_TILE = 512
ROW_DMA_TILE = 256
ROW_DMA_UNROLL = 8
DEST_TILE = 2048
QT = 128
QL = HPG * QT
NEAR_SLOTS = 13
SLOT_FAR = 13
SLOT_WIN = 14
SLOT_NONE = 15
N_SLOTS = 16
CMP_NEAR = 112
SEL_SUB = 4
SEL_STREAMS = 2
SEL_PEN_ROWS = SEL_STREAMS * SEL_SUB * QT // SEL_BLOCK
K_ONE = HEAD_DIM
K_PEN = 80
V_ROWS = 80
LOG2E = 1.4426950408889634


def _bucket_thresholds():
    n = np.arange(0, 4 * REL_MAX_DIST)
    max_exact = REL_BUCKETS // 2
    nf = np.maximum(n, 1).astype(np.float32)
    large = max_exact + (np.log(nf / np.float32(max_exact))
                         / np.float32(math.log(REL_MAX_DIST / max_exact))
                         * np.float32(REL_BUCKETS - max_exact)).astype(np.int32)
    large = np.minimum(large, REL_BUCKETS - 1)
    b = np.where(n < max_exact, n, large)
    return [int(np.argmax(b >= k)) for k in range(REL_BUCKETS)]


BUCKET_THR = _bucket_thresholds()
assert BUCKET_THR[REL_BUCKETS - 1] <= QT * NEAR_SLOTS - (QT - 1)
assert BUCKET_THR[REL_BUCKETS - 1] <= CMP_STRIDE * (CMP_NEAR - 8 + 1) - CMP_LEN + 1


def _cparams(sem):
    return pltpu.CompilerParams(dimension_semantics=sem, vmem_limit_bytes=VMEM_LIMIT)


def _bdot(a, b):
    return jnp.dot(a.astype(BF16), b.astype(BF16), preferred_element_type=F32)


def _sigmoid(x):
    return 1.0 / (1.0 + jnp.exp(-x))


def _rms(x, g):
    return x * lax.rsqrt(jnp.mean(x * x, axis=-1, keepdims=True) + RMS_EPS) * g


def _norm_mod(x, g, mod):
    return _rms(x, g) * (1.0 + mod[1:2]) + mod[0:1]


def _adaln_kernel(c_ref, wm_ref, bm_ref, wf_ref, bf_ref, om_ref, of_ref):
    c = c_ref[...]
    s = c * _sigmoid(c)
    om_ref[0] = jnp.dot(s, wm_ref[0], precision=lax.Precision.HIGHEST,
                        preferred_element_type=F32) + bm_ref[0]
    of_ref[0] = jnp.dot(s, wf_ref[0], precision=lax.Precision.HIGHEST,
                        preferred_element_type=F32) + bf_ref[0]


def _adaln(c, mix_w, mix_b, ffn_w, ffn_b):
    B, D = c.shape
    depth = mix_w.shape[0]
    cp = jnp.zeros((SUBLANES, D), F32).at[:B].set(c)
    w_spec = pl.BlockSpec((1, D, ADA_TN), lambda l, j: (l, 0, j))
    b_spec = pl.BlockSpec((1, 1, ADA_TN), lambda l, j: (l, 0, j))
    o_spec = pl.BlockSpec((1, SUBLANES, ADA_TN), lambda l, j: (l, 0, j))
    om, of = pl.pallas_call(
        _adaln_kernel,
        grid=(depth, 3 * D // ADA_TN),
        in_specs=[pl.BlockSpec((SUBLANES, D), lambda l, j: (0, 0)), w_spec, b_spec, w_spec, b_spec],
        out_specs=[o_spec, o_spec],
        out_shape=[jax.ShapeDtypeStruct((depth, SUBLANES, 3 * D), F32)] * 2,
        compiler_params=_cparams(("arbitrary", "arbitrary")),
        name="adaln",
    )(cp, mix_w, mix_b.reshape(depth, 1, 3 * D), ffn_w, ffn_b.reshape(depth, 1, 3 * D))
    return (om[:, :B].reshape(depth, B, 3, D), of[:, :B].reshape(depth, B, 3, D))


def _mod_spec(layer):
    return pl.BlockSpec((1, 1, 3, D_MODEL), lambda b, *_: (layer, b, 0, 0))


def _vec_spec(layer, width=D_MODEL):
    return pl.BlockSpec((1, 1, width), lambda *_: (layer, 0, 0))


def _post_residual(x, y, post_g, mod):
    return x + mod[2:3] * _rms(y, post_g)


def _conf_a_kernel(x_ref, mod_ref, g_ref, w_ref, b_ref, u_ref):
    D = D_MODEL
    h = _norm_mod(x_ref[0], g_ref[0], mod_ref[0, 0])
    z = _bdot(h, w_ref[...]) + b_ref[0]
    u_ref[0] = z[:, :D] * _sigmoid(z[:, D:])


def _conf_b_kernel(u_ref, up_ref, x_ref, mod_ref, wdw_ref, bdw_ref, lng_ref, lnb_ref,
                   w2_ref, b2_ref, pg_ref, o_ref, ext_ref, conv_ref):
    T, D, K = CONV_TILE, D_MODEL, CONF_KERNEL
    i = pl.program_id(1)
    keep = (i > 0).astype(F32)
    ext_ref[0:CONV_HALO, :] = up_ref[0] * keep
    ext_ref[CONV_HALO:, :] = u_ref[0]
    rb, cb = 64, 256
    off = CONV_HALO - (K - 1)

    def col_body(c, carry):
        c0 = pl.multiple_of(c * cb, cb)
        bias = bdw_ref[0, :, pl.ds(c0, cb)]
        for r in range(T // rb):
            acc = jnp.zeros((rb // SUBLANES, SUBLANES, cb), F32) + bias[None]
            for res in range(SUBLANES):
                offs = [o for o in range(off, off + K) if o % SUBLANES == res]
                lo, hi = offs[0], offs[-1]
                win = ext_ref[pl.ds(r * rb + lo, hi - lo + rb), pl.ds(c0, cb)]
                win = win.reshape((hi - lo + rb) // SUBLANES, SUBLANES, cb)
                for o in offs:
                    i0 = (o - lo) // SUBLANES
                    w = wdw_ref[0, o - off, :, pl.ds(c0, cb)]
                    acc = acc + w[None] * win[i0:i0 + rb // SUBLANES]
            conv_ref[pl.ds(r * rb, rb), pl.ds(c0, cb)] = acc.reshape(rb, cb)
        return carry

    lax.fori_loop(0, D // cb, col_body, 0)
    v = conv_ref[...]
    mu = jnp.mean(v, axis=-1, keepdims=True)
    var = jnp.mean(jnp.square(v - mu), axis=-1, keepdims=True)
    ln = (v - mu) * lax.rsqrt(var + LN_EPS) * lng_ref[0] + lnb_ref[0]
    act = ln * _sigmoid(ln)
    y = _bdot(act, w2_ref[...]) + b2_ref[0]
    o_ref[0] = _post_residual(x_ref[0], y, pg_ref[0], mod_ref[0, 0])


def _conformer_layer(x, mod, layer, j, p):
    B, S, D = x.shape
    T = TOK_TILE
    u = pl.pallas_call(
        _conf_a_kernel,
        grid=(B, S // T),
        in_specs=[pl.BlockSpec((1, T, D), lambda b, i: (b, i, 0)), _mod_spec(layer),
                  _vec_spec(layer), pl.BlockSpec((D, 2 * D), lambda b, i: (0, 0)),
                  _vec_spec(j, 2 * D)],
        out_specs=pl.BlockSpec((1, T, D), lambda b, i: (b, i, 0)),
        out_shape=jax.ShapeDtypeStruct((B, S, D), F32),
        compiler_params=_cparams(("arbitrary", "arbitrary")),
        name="conf_pw1_glu",
    )(x, mod, p["mix_pre_g"], p["conf_w_pw1"][j].astype(BF16), p["conf_b_pw1"])
    T = CONV_TILE
    hb = T // CONV_HALO
    return pl.pallas_call(
        _conf_b_kernel,
        grid=(B, S // T),
        in_specs=[pl.BlockSpec((1, T, D), lambda b, i: (b, i, 0)),
                  pl.BlockSpec((1, CONV_HALO, D), lambda b, i: (b, jnp.maximum(i * hb - 1, 0), 0)),
                  pl.BlockSpec((1, T, D), lambda b, i: (b, i, 0)), _mod_spec(layer),
                  pl.BlockSpec((1, CONF_KERNEL, SUBLANES, D), lambda b, i: (j, 0, 0, 0)),
                  _vec_spec(j), _vec_spec(j), _vec_spec(j),
                  pl.BlockSpec((D, D), lambda b, i: (0, 0)), _vec_spec(j), _vec_spec(layer)],
        out_specs=pl.BlockSpec((1, T, D), lambda b, i: (b, i, 0)),
        out_shape=jax.ShapeDtypeStruct((B, S, D), F32),
        scratch_shapes=[pltpu.VMEM((T + CONV_HALO, D), F32), pltpu.VMEM((T, D), F32)],
        compiler_params=_cparams(("arbitrary", "arbitrary")),
        name="conf_conv_pw2",
    )(u, u, x, mod, jnp.broadcast_to(p["conf_w_dw"][:, :, None, :], (*p["conf_w_dw"].shape[:2], SUBLANES, D)),
      p["conf_b_dw"], p["conf_ln_g"], p["conf_ln_b"],
      p["conf_w_pw2"][j].astype(BF16), p["conf_b_pw2"], p["mix_post_g"])


def _short_conv_kernel(x_ref, mod_ref, g_ref, win_ref, wc_ref, wout_ref, pg_ref, o_ref, ext_ref):
    T, D = TOK_TILE, D_MODEL
    i = pl.program_id(1)

    @pl.when(i == 0)
    def _():
        ext_ref[0:SUBLANES, :] = jnp.zeros((SUBLANES, D), F32)

    x = x_ref[0]
    mod = mod_ref[0, 0]
    h = _norm_mod(x, g_ref[0], mod)
    z = _bdot(h, win_ref[...])
    v, gate_b, gate_c = z[:, :D], z[:, D:2 * D], z[:, 2 * D:]
    ext_ref[SUBLANES:, :] = gate_c * v
    wc = wc_ref[0]
    u = (wc[0:1] * ext_ref[pl.ds(SUBLANES - 2, T), :] + wc[1:2] * ext_ref[pl.ds(SUBLANES - 1, T), :]
         + wc[2:3] * ext_ref[pl.ds(SUBLANES, T), :])
    ext_ref[0:SUBLANES, :] = ext_ref[pl.ds(T, SUBLANES), :]
    y = _bdot(gate_b * u, wout_ref[...])
    o_ref[0] = _post_residual(x, y, pg_ref[0], mod)


def _short_conv_layer(x, mod, layer, j, p):
    B, S, D = x.shape
    T = TOK_TILE
    return pl.pallas_call(
        _short_conv_kernel,
        grid=(B, S // T),
        in_specs=[pl.BlockSpec((1, T, D), lambda b, i: (b, i, 0)), _mod_spec(layer), _vec_spec(layer),
                  pl.BlockSpec((D, 3 * D), lambda b, i: (0, 0)),
                  pl.BlockSpec((1, SHORT_KERNEL, D), lambda b, i: (j, 0, 0)),
                  pl.BlockSpec((D, D), lambda b, i: (0, 0)), _vec_spec(layer)],
        out_specs=pl.BlockSpec((1, T, D), lambda b, i: (b, i, 0)),
        out_shape=jax.ShapeDtypeStruct((B, S, D), F32),
        scratch_shapes=[pltpu.VMEM((T + SUBLANES, D), F32)],
        compiler_params=_cparams(("arbitrary", "arbitrary")),
        name="short_conv",
    )(x, mod, p["mix_pre_g"], p["sc_w_in"][j].astype(BF16), p["sc_w_conv"],
      p["sc_w_out"][j].astype(BF16), p["mix_post_g"])


def _swiglu_step(h, wg_ref, wu_ref, wd_ref, acc_ref):
    a = _bdot(h, wg_ref[...])
    u = _bdot(h, wu_ref[...])
    acc_ref[...] += _bdot((a * _sigmoid(a)) * u, wd_ref[...])


def _dense_ffn_kernel(x_ref, mod_ref, g_ref, wg_ref, wu_ref, wd_ref, pg_ref, o_ref, h_ref, acc_ref):
    f = pl.program_id(2)

    @pl.when(f == 0)
    def _():
        h_ref[...] = _norm_mod(x_ref[0], g_ref[0], mod_ref[0, 0]).astype(BF16)
        acc_ref[...] = jnp.zeros_like(acc_ref)

    _swiglu_step(h_ref[...], wg_ref.at[0], wu_ref.at[0], wd_ref.at[0], acc_ref)

    @pl.when(f == pl.num_programs(2) - 1)
    def _():
        o_ref[0] = _post_residual(x_ref[0], acc_ref[...], pg_ref[0], mod_ref[0, 0])


def _dense_ffn_layer(x, mod, layer, j, p):
    B, S, D = x.shape
    T, TF, F = min(FFN_TILE, S), FFN_TF, D_FF_DENSE
    return pl.pallas_call(
        _dense_ffn_kernel,
        grid=(B, S // T, F // TF),
        in_specs=[pl.BlockSpec((1, T, D), lambda b, i, f: (b, i, 0)), _mod_spec(layer), _vec_spec(layer),
                  pl.BlockSpec((1, D, TF), lambda b, i, f: (j, 0, f)),
                  pl.BlockSpec((1, D, TF), lambda b, i, f: (j, 0, f)),
                  pl.BlockSpec((1, TF, D), lambda b, i, f: (j, f, 0)), _vec_spec(layer)],
        out_specs=pl.BlockSpec((1, T, D), lambda b, i, f: (b, i, 0)),
        out_shape=jax.ShapeDtypeStruct((B, S, D), F32),
        scratch_shapes=[pltpu.VMEM((T, D), BF16), pltpu.VMEM((T, D), F32)],
        compiler_params=_cparams(("arbitrary", "arbitrary", "arbitrary")),
        name="dense_swiglu",
    )(x, mod, p["ffn_pre_g"], p["dense_w_gate"], p["dense_w_up"], p["dense_w_down"], p["ffn_post_g"])


def _router_kernel(x_ref, mod_ref, g_ref, wr_ref, br_ref, h_ref, ri_ref, rw_ref, cnt_ref, base_ref):
    T = ROUTE_TILE
    i = pl.program_id(0)

    @pl.when(i == 0)
    def _():
        base_ref[...] = jnp.zeros_like(base_ref)

    h = _norm_mod(x_ref[...], g_ref[0], mod_ref[0, 0])
    h_ref[...] = h
    logits = jnp.dot(h, wr_ref[...], precision=lax.Precision.HIGHEST,
                     preferred_element_type=F32) + br_ref[0]
    lane = lax.broadcasted_iota(I32, (T, LANES), 1).astype(F32)
    logits = jnp.where(lane < N_EXPERTS, logits, -jnp.inf)
    l0 = jnp.max(logits, axis=1, keepdims=True)
    e0 = jnp.min(jnp.where(logits == l0, lane, float(LANES)), axis=1, keepdims=True)
    rest = jnp.where(lane == e0, -jnp.inf, logits)
    l1 = jnp.max(rest, axis=1, keepdims=True)
    e1 = jnp.min(jnp.where(rest == l1, lane, float(LANES)), axis=1, keepdims=True)
    ex = jnp.exp(l1 - l0)
    w0 = 1.0 / (1.0 + ex)
    w1 = ex / (1.0 + ex)
    onehot = ((lane == e0) | (lane == e1)).astype(F32)
    row = lax.broadcasted_iota(I32, (T, T), 0)
    col = lax.broadcasted_iota(I32, (T, T), 1)
    tri = jnp.where(row > col, 1.0, 0.0).astype(BF16)
    before = jnp.dot(tri, onehot.astype(BF16), preferred_element_type=F32) + base_ref[...]
    r0 = jnp.sum(jnp.where(lane == e0, before, 0.0), axis=1, keepdims=True)
    r1 = jnp.sum(jnp.where(lane == e1, before, 0.0), axis=1, keepdims=True)
    base_ref[...] += jnp.sum(onehot, axis=0, keepdims=True)
    cnt_ref[...] = base_ref[...].astype(I32)
    ri = jnp.where(lane == 0, e0, jnp.where(lane == 1, e1, jnp.where(lane == 2, r0, r1)))
    ri_ref[...] = ri[:, :SUBLANES].astype(I32)
    rw_ref[...] = jnp.where(lane == 0, w0, w1)[:, :SUBLANES]


def _dest_kernel(pstart_ref, ri_ref, dst_ref):
    ri = ri_ref[...]
    lane = lax.broadcasted_iota(I32, ri.shape, 1)
    expert = jnp.where(lane == 0, ri[:, 0:1], ri[:, 1:2])
    rank = jnp.where(lane == 0, ri[:, 2:3], ri[:, 3:4])
    start = jnp.zeros(ri.shape, I32)
    for e in range(N_EXPERTS):
        start = jnp.where(expert == e, pstart_ref[e], start)
    dst_ref[...] = start + rank


def _dispatch_kernel(dst_ref, h_ref, zero_ref, buf_ref, stage_ref, sem):
    del zero_ref
    T = ROW_DMA_TILE
    i = pl.program_id(0)
    slot = i % 2
    stage_ref[slot] = h_ref[...]

    def row_copy(t, dest, s):
        return pltpu.make_async_copy(stage_ref.at[s, pl.ds(t, 1)], buf_ref.at[pl.ds(dest, 1)], sem.at[s])

    def issue(t, carry):
        for k in range(2):
            row_copy(t, dst_ref[t, k], slot).start()
        return carry

    lax.fori_loop(0, T, issue, 0, unroll=ROW_DMA_UNROLL)

    def drain(s):
        def body(t, carry):
            for k in range(2):
                row_copy(0, 0, s).wait()
            return carry

        lax.fori_loop(0, T, body, 0, unroll=ROW_DMA_UNROLL)

    @pl.when(i > 0)
    def _():
        drain(1 - slot)

    @pl.when(i == pl.num_programs(0) - 1)
    def _():
        drain(slot)


def _expert_kernel(blk_e_ref, nblk_ref, xb_ref, wg_ref, wu_ref, wd_ref, o_ref, h_ref, acc_ref):
    del blk_e_ref
    i = pl.program_id(0)
    f = pl.program_id(1)
    live = i < nblk_ref[0]

    @pl.when(f == 0)
    def _():
        h_ref[...] = xb_ref[...].astype(BF16)
        acc_ref[...] = jnp.zeros_like(acc_ref)

    @pl.when(live)
    def _():
        _swiglu_step(h_ref[...], wg_ref.at[0, 0], wu_ref.at[0, 0], wd_ref.at[0, 0], acc_ref)

    @pl.when(f == pl.num_programs(1) - 1)
    def _():
        o_ref[...] = acc_ref[...]


def _combine_kernel(x_ref, mod_ref, pg_ref, dst_ref, dst_next_ref, rw_ref, yb_ref, o_ref, y_ref, sem):
    T = ROW_DMA_TILE
    i = pl.program_id(0)
    slot = i % 2

    def row_copy(src, t, k, s):
        return pltpu.make_async_copy(yb_ref.at[pl.ds(src, 1)], y_ref.at[s, k, pl.ds(t, 1)], sem.at[s])

    def gather(idx_ref, s):
        def body(t, carry):
            for k in range(2):
                row_copy(idx_ref[t, k], t, k, s).start()
            return carry

        lax.fori_loop(0, T, body, 0, unroll=ROW_DMA_UNROLL)

    @pl.when(i == 0)
    def _():
        gather(dst_ref, slot)

    @pl.when(i + 1 < pl.num_programs(0))
    def _():
        gather(dst_next_ref, 1 - slot)

    def drain(t, carry):
        for k in range(2):
            row_copy(0, 0, k, slot).wait()
        return carry

    lax.fori_loop(0, T, drain, 0, unroll=ROW_DMA_UNROLL)
    rw = rw_ref[...]
    y = rw[:, 0:1] * y_ref[slot, 0] + rw[:, 1:2] * y_ref[slot, 1]
    o_ref[...] = _post_residual(x_ref[...], y, pg_ref[0], mod_ref[0, 0])


def _moe_ffn_layer(x, mod, layer, j, p):
    B, S, D = x.shape
    N = B * S
    T = ROUTE_TILE
    tiles_per_seq = S // T
    xt = x.reshape(N, D)
    wr = jnp.zeros((D, LANES), F32).at[:, :N_EXPERTS].set(p["moe_w_router"][j])
    br = jnp.zeros((1, 1, LANES), F32).at[0, 0, :N_EXPERTS].set(p["moe_b_router"][j])
    h, ri, rw, cnt = pl.pallas_call(
        _router_kernel,
        grid=(N // T,),
        in_specs=[pl.BlockSpec((T, D), lambda i: (i, 0)),
                  pl.BlockSpec((1, 1, 3, D), lambda i: (layer, i // tiles_per_seq, 0, 0)),
                  _vec_spec(layer), pl.BlockSpec((D, LANES), lambda i: (0, 0)), _vec_spec(0, LANES)],
        out_specs=[pl.BlockSpec((T, D), lambda i: (i, 0)), pl.BlockSpec((T, SUBLANES), lambda i: (i, 0)),
                   pl.BlockSpec((T, SUBLANES), lambda i: (i, 0)), pl.BlockSpec((1, LANES), lambda i: (0, 0))],
        out_shape=[jax.ShapeDtypeStruct((N, D), F32), jax.ShapeDtypeStruct((N, SUBLANES), I32),
                   jax.ShapeDtypeStruct((N, SUBLANES), F32), jax.ShapeDtypeStruct((1, LANES), I32)],
        scratch_shapes=[pltpu.VMEM((1, LANES), F32)],
        compiler_params=_cparams(("arbitrary",)),
        name="moe_router",
    )(xt, mod, p["ffn_pre_g"], wr, br)

    R = MOE_ROWS
    counts = cnt[0, :N_EXPERTS]
    padded = (counts + R - 1) // R * R
    pend = jnp.cumsum(padded)
    pstart = (pend - padded).astype(I32)
    n_blocks = -(-(2 * N + N_EXPERTS * (R - 1)) // R)
    blk_e = jnp.minimum(jnp.sum((jnp.arange(n_blocks) * R)[:, None] >= pend[None, :], axis=1),
                        N_EXPERTS - 1).astype(I32)
    n_live = (pend[-1] // R).astype(I32).reshape(1)
    rows = n_blocks * R

    TR = DEST_TILE
    dst = pl.pallas_call(
        _dest_kernel,
        grid_spec=pltpu.PrefetchScalarGridSpec(
            num_scalar_prefetch=1,
            grid=(N // TR,),
            in_specs=[pl.BlockSpec((TR, SUBLANES), lambda i, ps: (i, 0))],
            out_specs=pl.BlockSpec((TR, SUBLANES), lambda i, ps: (i, 0)),
        ),
        out_shape=jax.ShapeDtypeStruct((N, SUBLANES), I32),
        compiler_params=_cparams(("arbitrary",)),
        name="moe_dest",
    )(pstart, ri)

    TD = ROW_DMA_TILE
    n_dma_tiles = N // TD
    dst_spec = pl.BlockSpec((TD, SUBLANES), lambda i: (i, 0), memory_space=pltpu.SMEM)
    buf = pl.pallas_call(
        _dispatch_kernel,
        grid=(n_dma_tiles,),
        in_specs=[dst_spec, pl.BlockSpec((TD, D), lambda i: (i, 0)), pl.BlockSpec(memory_space=pl.ANY)],
        out_specs=pl.BlockSpec(memory_space=pl.ANY),
        out_shape=jax.ShapeDtypeStruct((rows, D), F32),
        scratch_shapes=[pltpu.VMEM((2, TD, D), F32), pltpu.SemaphoreType.DMA((2,))],
        input_output_aliases={2: 0},
        compiler_params=_cparams(("arbitrary",)),
        name="moe_dispatch",
    )(dst, h, jnp.zeros((rows, D), F32))

    TF, F = MOE_TF, D_FF_EXPERT
    yb = pl.pallas_call(
        _expert_kernel,
        grid_spec=pltpu.PrefetchScalarGridSpec(
            num_scalar_prefetch=2,
            grid=(n_blocks, F // TF),
            in_specs=[pl.BlockSpec((R, D), lambda i, f, be, nb: (i, 0)),
                      pl.BlockSpec((1, 1, D, TF), lambda i, f, be, nb: (j, be[i], 0, jnp.where(i < nb[0], f, 0))),
                      pl.BlockSpec((1, 1, D, TF), lambda i, f, be, nb: (j, be[i], 0, jnp.where(i < nb[0], f, 0))),
                      pl.BlockSpec((1, 1, TF, D), lambda i, f, be, nb: (j, be[i], jnp.where(i < nb[0], f, 0), 0))],
            out_specs=pl.BlockSpec((R, D), lambda i, f, be, nb: (i, 0)),
            scratch_shapes=[pltpu.VMEM((R, D), BF16), pltpu.VMEM((R, D), F32)],
        ),
        out_shape=jax.ShapeDtypeStruct((rows, D), F32),
        compiler_params=_cparams(("arbitrary", "arbitrary")),
        name="moe_experts",
    )(blk_e, n_live, buf, p["moe_w_gate"], p["moe_w_up"], p["moe_w_down"])

    tiles_per_seq_d = S // TD
    out = pl.pallas_call(
        _combine_kernel,
        grid=(n_dma_tiles,),
        in_specs=[pl.BlockSpec((TD, D), lambda i: (i, 0)),
                  pl.BlockSpec((1, 1, 3, D), lambda i: (layer, i // tiles_per_seq_d, 0, 0)),
                  pl.BlockSpec((1, 1, D), lambda i: (layer, 0, 0)),
                  dst_spec,
                  pl.BlockSpec((TD, SUBLANES), lambda i: (jnp.minimum(i + 1, n_dma_tiles - 1), 0),
                               memory_space=pltpu.SMEM),
                  pl.BlockSpec((TD, SUBLANES), lambda i: (i, 0)),
                  pl.BlockSpec(memory_space=pl.ANY)],
        out_specs=pl.BlockSpec((TD, D), lambda i: (i, 0)),
        out_shape=jax.ShapeDtypeStruct((N, D), F32),
        scratch_shapes=[pltpu.VMEM((2, 2, TD, D), F32), pltpu.SemaphoreType.DMA((2,))],
        compiler_params=_cparams(("arbitrary",)),
        name="moe_combine",
    )(xt, mod, p["ffn_post_g"], dst, dst, rw, yb)
    return out.reshape(B, S, D)


NSA_W = D_MODEL + 2 * N_GROUPS * LANES + 4 * N_GROUPS * HEAD_DIM + LANES


def _ones_rows(n):
    r = lax.broadcasted_iota(I32, (V_ROWS - HEAD_DIM, n), 0)
    return jnp.where(r == 0, 1.0, 0.0)


def _nsa_in_kernel(x_ref, mod_ref, g_ref, w_ref, q_ref, ks_ref, vst_ref, kw_ref, vwt_ref,
                   kc_ref, vc_ref, gt_ref):
    D, G, DH = D_MODEL, N_GROUPS, HEAD_DIM
    KV = G * DH
    T = x_ref.shape[1]
    h = _norm_mod(x_ref[0], g_ref[0], mod_ref[0, 0])
    z = _bdot(h, w_ref[...])
    q_ref[0] = z[:, :D].astype(BF16)
    lane = lax.broadcasted_iota(I32, (T, LANES), 1)
    pos = pl.program_id(1) * T + lax.broadcasted_iota(I32, (T, LANES), 0)
    one_col = jnp.where(lane == K_ONE, 1.0, 0.0)
    blk_col = jnp.where(lane == K_PEN + (pos // SEL_BLOCK) % SEL_PEN_ROWS, 1.0, 0.0)
    o = D
    for g in range(G):
        ks_ref[0, g] = (z[:, o + g * LANES:o + (g + 1) * LANES] + (one_col + blk_col)).astype(BF16)
    o += G * LANES
    for g in range(G):
        kw_ref[0, g] = (z[:, o + g * LANES:o + (g + 1) * LANES] + one_col).astype(BF16)
    o += G * LANES
    v_s, v_w = z[:, o:o + KV], z[:, o + KV:o + 2 * KV]
    k_c, v_c = z[:, o + 2 * KV:o + 3 * KV], z[:, o + 3 * KV:o + 4 * KV]
    gates = _sigmoid(z[:, o + 4 * KV:])
    vst = v_s.T
    vwt = v_w.T
    ones = _ones_rows(T)
    for g in range(G):
        sl = slice(g * DH, (g + 1) * DH)
        kc_ref[0, g] = k_c[:, sl]
        vc_ref[0, g] = v_c[:, sl]
        vst_ref[0, g] = jnp.concatenate([vst[sl, :], ones], axis=0).astype(BF16)
        vwt_ref[0, g] = jnp.concatenate([vwt[sl, :], ones], axis=0).astype(BF16)
    gt_ref[0] = gates.T[:3 * N_HEADS, :]


def _gelu_tanh(x):
    return x * (0.5 * (1.0 + jnp.tanh(math.sqrt(2.0 / math.pi) * (x + 0.044715 * (x * x * x)))))


def _compress_kernel(kc_ref, vc_ref, kpe_ref, kw1_ref, kw2_ref, vpe_ref, vw1_ref, vw2_ref,
                     kco_ref, vcto_ref):
    nc = kc_ref.shape[2]

    def comp(c, pe_ref, w1_ref, w2_ref):
        a = _bdot(c + pe_ref[0:1], w1_ref[0])
        b = _bdot(c + pe_ref[1:2], w1_ref[1])
        b_next = pltpu.roll(b, nc - 1, axis=0)
        out = _bdot(_gelu_tanh(a + b_next), w2_ref[...])
        row = lax.broadcasted_iota(I32, out.shape, 0)
        return jnp.where(row < nc - 1, out, 0.0)

    kc = comp(kc_ref[0, 0], kpe_ref, kw1_ref, kw2_ref)
    lane = lax.broadcasted_iota(I32, kc.shape, 1)
    kco_ref[0, 0] = (kc + jnp.where(lane == K_ONE, 1.0, 0.0)).astype(BF16)
    vct = comp(vc_ref[0, 0], vpe_ref, vw1_ref, vw2_ref).T
    vcto_ref[0, 0] = jnp.concatenate([vct, _ones_rows(nc)], axis=0).astype(BF16)


def _bias_table_kernel(tbl_ref, tab_ref, ctab_ref, far_ref):
    g = pl.program_id(0)

    def bias_of(dist, h, far):
        v = jnp.full(dist.shape, tbl_ref[0, h], F32)
        for b in range(1, REL_BUCKETS):
            v = jnp.where(dist >= BUCKET_THR[b], tbl_ref[b, h], v)
        return jnp.where(dist >= 0, v * LOG2E - far, NEG)

    nc = (ctab_ref.shape[1] - LANES) // 2
    key = lax.broadcasted_iota(I32, (QT, QT), 0)
    qry = lax.broadcasted_iota(I32, (QT, QT), 1)
    crow = lax.broadcasted_iota(I32, (CMP_NEAR, QT), 0)
    cqry = lax.broadcasted_iota(I32, (CMP_NEAR, QT), 1)
    for hh in range(HPG):
        h = g * HPG + hh
        lanes = slice(hh * QT, (hh + 1) * QT)
        far = jnp.full((1, QT), tbl_ref[REL_BUCKETS - 1, h] * LOG2E, F32).astype(BF16).astype(F32)
        far_ref[0, :, lanes] = far

        def near_slot(slot, carry, h=h, lanes=lanes, far=far):
            tab_ref[0, slot, :, lanes] = bias_of(slot * QT + qry - key, h, far)
            return carry

        lax.fori_loop(0, NEAR_SLOTS, near_slot, 0)
        tab_ref[0, SLOT_FAR, :, lanes] = jnp.zeros((QT, QT), F32)
        tab_ref[0, SLOT_NONE, :, lanes] = jnp.full((QT, QT), NEG, F32)
        dw = WINDOW + qry - key
        tab_ref[0, SLOT_WIN, :, lanes] = jnp.where(dw < WINDOW, bias_of(dw, h, far), NEG)
        ctab_ref[0, 0:nc, lanes] = jnp.zeros((nc, QT), F32)
        dc = cqry - CMP_STRIDE * (crow - (CMP_NEAR - SUBLANES)) - (CMP_LEN - 1)
        ctab_ref[0, nc:nc + CMP_NEAR, lanes] = bias_of(dc, h, far)
        ctab_ref[0, nc + CMP_NEAR:, lanes] = jnp.full((nc + LANES - CMP_NEAR, QT), NEG, F32)


def _nsa_attn_kernel(q_ref, ks_ref, vst_ref, kw_refs, vwt_refs, kc_ref, vct_ref, gt_ref,
                     tab_ref, ctab_ref, far_ref, o_ref, qt_ref, s_even_ref, s_odd_ref, psum_ref, pen_ref, ocw_ref):
    DH = HEAD_DIM
    a = pl.program_id(2)
    nc = kc_ref.shape[2]
    n_sel = pen_ref.shape[0]

    qT = q_ref[0].astype(F32).T
    aug_row = lax.broadcasted_iota(I32, (LANES - DH, QL), 0)
    aug = jnp.where(aug_row == 0, far_ref[0], 0.0).astype(BF16)
    for i in range(2):
        for hh in range(HPG):
            qt_ref[i, 0:DH, hh * QT:(hh + 1) * QT] = qT[hh * DH:(hh + 1) * DH, :].astype(BF16)
        qt_ref[i, DH:, :] = aug
    qt = qt_ref[0]

    def window():
        nw = WINDOW // QT + 1
        s_w = []
        for d in range(nw):
            slot = SLOT_WIN if d == nw - 1 else d
            if d > 0:
                slot = jnp.where(a >= d, slot, SLOT_NONE)
            s_w.append(jnp.dot(kw_refs[d][0, 0], qt, preferred_element_type=F32) + tab_ref[0, slot])
        m_w = functools.reduce(jnp.maximum, [jnp.max(s_d, axis=0, keepdims=True) for s_d in s_w])
        acc_w = functools.reduce(lambda x, y: x + y, [
            jnp.dot(vwt_refs[d][0, 0], jnp.exp2(s_w[d] - m_w).astype(BF16), preferred_element_type=F32)
            for d in range(nw)])
        ocw_ref[1] = acc_w[0:DH] * (1.0 / acc_w[DH:DH + 1])

    n_hi = (QT // CMP_STRIDE) * (a + 1)
    c0 = pl.multiple_of(nc - (n_hi - CMP_NEAR), SUBLANES)
    ratio = SEL_BLOCK // CMP_STRIDE

    def compressed_and_select(rows_c):
        rows_s = rows_c // ratio
        s = (jnp.dot(kc_ref[0, 0, 0:rows_c, :], qt, preferred_element_type=F32)
             + ctab_ref[0, pl.ds(c0, rows_c), :])
        m = jnp.max(s, axis=0, keepdims=True)
        e = jnp.exp2(s - m)
        l = jnp.sum(e, axis=0, keepdims=True)
        pn = e * jnp.where(m > 0.1 * NEG, 1.0 / l, 0.0)
        ocw_ref[0] = jnp.dot(vct_ref[0, 0, :, 0:rows_c], pn.astype(BF16), preferred_element_type=F32)[0:DH]
        psum_ref[0:rows_c, :] = pn[:, 0:QT] + pn[:, QT:2 * QT] + pn[:, 2 * QT:3 * QT] + pn[:, 3 * QT:4 * QT]

        imp = psum_ref[pl.ds(0, rows_s, stride=ratio), :]
        for jj in range(1, ratio):
            imp = imp + psum_ref[pl.ds(jj, rows_s, stride=ratio), :]
        last = psum_ref[pl.ds(ratio - 1, rows_s, stride=ratio), :]
        srow = lax.broadcasted_iota(I32, (rows_s, QT), 0)
        imp = imp + jnp.where(srow > 0, pltpu.roll(last, 1, axis=0), 0.0)
        lane_q = lax.broadcasted_iota(I32, (rows_s, QT), 1)
        cur = (QT // SEL_BLOCK) * a + lane_q // SEL_BLOCK
        back = cur - srow
        valid = back >= 0
        forced = (srow == 0) | (valid & (back < N_LOCAL_BLOCKS))
        n_forced = 1 + N_LOCAL_BLOCKS
        score = jnp.where(forced, -jnp.inf, jnp.where(valid, imp, -1.0))
        srow_f = srow.astype(F32)
        for _ in range(min(SEL_TOPK, n_sel) - n_forced):
            mx = jnp.max(score, axis=0, keepdims=True)
            first = jnp.min(jnp.where(score == mx, srow_f, float(rows_s)), axis=0, keepdims=True)
            score = jnp.where(srow_f == first, -jnp.inf, score)
        pen = jnp.where(score == -jnp.inf, 0.0, NEG)
        for hh in range(HPG):
            pen_ref[0:rows_s, hh * QT:(hh + 1) * QT] = pen
        if rows_s < n_sel:
            pen_ref[rows_s:, :] = jnp.full((n_sel - rows_s, QL), NEG, F32)
        window()

    sizes = [r for r in (nc // 8, nc // 4, nc // 2, nc) if r >= CMP_NEAR + 2 * SUBLANES]
    tiles_per = QT // CMP_STRIDE
    for idx, rows_c in enumerate(sizes):
        lo = 0 if idx == 0 else sizes[idx - 1] // tiles_per
        hi = rows_c // tiles_per
        pl.when((a >= lo) & (a < hi))(functools.partial(compressed_and_select, rows_c))
    o_c = ocw_ref[0]
    o_w = ocw_ref[1]

    big = SEL_SUB * QT
    span = SEL_STREAMS * big
    sub_per_step = SEL_STREAMS * SEL_SUB

    n_span = ks_ref.shape[2] // span

    def scores(step, buf):
        sc = jnp.minimum(step, n_span - 1)
        pen_rows = pen_ref[pl.ds(pl.multiple_of(sc * SEL_PEN_ROWS, SEL_PEN_ROWS), SEL_PEN_ROWS), :]
        qt_ref[buf, K_PEN:K_PEN + SEL_PEN_ROWS, :] = pen_rows.astype(BF16)
        k0 = pl.multiple_of(sc * span, span)
        return jnp.dot(ks_ref[0, 0, pl.ds(k0, span), :], qt_ref[buf], preferred_element_type=F32)

    def update(s_all, step, state, near):
        k0 = pl.multiple_of(jnp.minimum(step, n_span - 1) * span, span)
        out = []
        for t in range(SEL_STREAMS):
            m, acc = state[t]
            s = s_all[t * big:(t + 1) * big]
            if near:
                parts = []
                for u in range(SEL_SUB):
                    d = a - (step * sub_per_step + t * SEL_SUB + u)
                    slot = jnp.where(d >= 0, jnp.minimum(d, SLOT_FAR), SLOT_NONE)
                    parts.append(s[u * QT:(u + 1) * QT] + tab_ref[0, slot])
                s = jnp.concatenate(parts, axis=0)
            m_new = jnp.maximum(m, jnp.max(s, axis=0, keepdims=True))
            p = jnp.exp2(s - m_new).astype(BF16)
            acc = jnp.exp2(m - m_new) * acc + jnp.dot(vst_ref[0, 0, :, pl.ds(k0 + t * big, big)], p,
                                                      preferred_element_type=F32)
            out.append((m_new, acc))
        return tuple(out)

    def two_steps(jj, state, near):
        s_odd_ref[...] = scores(2 * jj + 1, 1)
        state = update(s_even_ref[...], 2 * jj, state, near)
        s_even_ref[...] = scores(2 * jj + 2, 0)
        return update(s_odd_ref[...], 2 * jj + 1, state, near)

    n_steps = a // sub_per_step + 1
    n_far2 = jnp.maximum(a - (NEAR_SLOTS - 1), 0) // (2 * sub_per_step)
    n_pairs = n_steps // 2
    s_even_ref[...] = scores(0, 0)
    state = tuple((jnp.full((1, QL), NEG, F32), jnp.zeros((V_ROWS, QL), F32)) for _ in range(SEL_STREAMS))
    state = lax.fori_loop(0, n_far2, functools.partial(two_steps, near=False), state)
    state = lax.fori_loop(n_far2, n_pairs, functools.partial(two_steps, near=True), state)
    state = lax.cond(n_steps % 2 == 1,
                     lambda st: update(s_even_ref[...], n_steps - 1, st, True),
                     lambda st: st, state)
    m_s = functools.reduce(jnp.maximum, [m for m, _ in state])
    acc_s = functools.reduce(lambda x, y: x + y, [jnp.exp2(m - m_s) * acc for m, acc in state])
    o_s = acc_s[0:DH] * (1.0 / acc_s[DH:DH + 1])

    def gate(branch):
        return jnp.concatenate([gt_ref[0, 0, 3 * hh + branch:3 * hh + branch + 1, :] for hh in range(HPG)],
                               axis=1)

    oT = gate(0) * o_c + gate(1) * o_s + gate(2) * o_w
    stacked = jnp.concatenate([oT[:, hh * QT:(hh + 1) * QT] for hh in range(HPG)], axis=0)
    o_ref[0] = stacked.T.astype(BF16)


def _nsa_attn_wrapper(*refs):
    nw = WINDOW // QT + 1
    q_ref, ks_ref, vst_ref = refs[0:3]
    kw_refs = refs[3:3 + nw]
    vwt_refs = refs[3 + nw:3 + 2 * nw]
    rest = refs[3 + 2 * nw:]
    _nsa_attn_kernel(q_ref, ks_ref, vst_ref, kw_refs, vwt_refs, *rest)


def _out_proj_kernel(o_ref, x_ref, mod_ref, w_ref, pg_ref, out_ref):
    y = jnp.dot(o_ref[0], w_ref[...], preferred_element_type=F32)
    out_ref[0] = _post_residual(x_ref[0], y, pg_ref[0], mod_ref[0, 0])


def _nsa_layer(x, mod, layer, j, p):
    B, S, D = x.shape
    G, DH, KV = N_GROUPS, HEAD_DIM, N_GROUPS * HEAD_DIM
    T = TOK_TILE
    NC = S // CMP_STRIDE
    n_sel = S // SEL_BLOCK

    w_in = p["nsa_w_in"][j]
    cuts = [D + i * KV for i in range(7)]
    wq, wkc, wvc, wks, wvs, wkw, wvw, wg = jnp.split(w_in, cuts, axis=-1)
    wg = jnp.zeros((D, LANES), F32).at[:, :3 * N_HEADS].set(wg)

    def slabs(w):
        return jnp.zeros((D, G, LANES), F32).at[:, :, :DH].set(w.reshape(D, G, DH)).reshape(D, G * LANES)

    w_all = jnp.concatenate([wq * (HEAD_DIM ** -0.5 * LOG2E), slabs(wks), slabs(wkw), wvs, wvw, wkc, wvc, wg],
                            axis=1).astype(BF16)

    k_spec = pl.BlockSpec((1, G, T, LANES), lambda b, i: (b, 0, i, 0))
    kc_spec = pl.BlockSpec((1, G, T, DH), lambda b, i: (b, 0, i, 0))
    vt_spec = pl.BlockSpec((1, G, V_ROWS, T), lambda b, i: (b, 0, 0, i))
    q, ks, vst, kw, vwt, kc_raw, vc_raw, gt = pl.pallas_call(
        _nsa_in_kernel,
        grid=(B, S // T),
        in_specs=[pl.BlockSpec((1, T, D), lambda b, i: (b, i, 0)), _mod_spec(layer), _vec_spec(layer),
                  pl.BlockSpec((D, NSA_W), lambda b, i: (0, 0))],
        out_specs=[pl.BlockSpec((1, T, D), lambda b, i: (b, i, 0)), k_spec, vt_spec, k_spec, vt_spec,
                   kc_spec, kc_spec, pl.BlockSpec((1, 3 * N_HEADS, T), lambda b, i: (b, 0, i))],
        out_shape=[jax.ShapeDtypeStruct((B, S, D), BF16),
                   jax.ShapeDtypeStruct((B, G, S, LANES), BF16), jax.ShapeDtypeStruct((B, G, V_ROWS, S), BF16),
                   jax.ShapeDtypeStruct((B, G, S, LANES), BF16), jax.ShapeDtypeStruct((B, G, V_ROWS, S), BF16),
                   jax.ShapeDtypeStruct((B, G, S, DH), F32), jax.ShapeDtypeStruct((B, G, S, DH), F32),
                   jax.ShapeDtypeStruct((B, 3 * N_HEADS, S), F32)],
        compiler_params=_cparams(("arbitrary", "arbitrary")),
        name="nsa_in_proj",
    )(x, mod, p["mix_pre_g"], w_all)

    half = CMP_LEN // 2
    chunk_w = half * DH

    def split_w1(w1):
        return w1.reshape(2, chunk_w, CMP_HIDDEN).astype(BF16)

    def split_pe(pe):
        return pe.reshape(2, chunk_w)

    c_spec = pl.BlockSpec((1, 1, NC, chunk_w), lambda b, g: (b, g, 0, 0))
    full = lambda shape: pl.BlockSpec(shape, lambda b, g: (0,) * len(shape))
    kc, vct = pl.pallas_call(
        _compress_kernel,
        grid=(B, G),
        in_specs=[c_spec, c_spec,
                  full((2, chunk_w)), full((2, chunk_w, CMP_HIDDEN)), full((CMP_HIDDEN, LANES)),
                  full((2, chunk_w)), full((2, chunk_w, CMP_HIDDEN)), full((CMP_HIDDEN, DH))],
        out_specs=[pl.BlockSpec((1, 1, NC, LANES), lambda b, g: (b, g, 0, 0)),
                   pl.BlockSpec((1, 1, V_ROWS, NC), lambda b, g: (b, g, 0, 0))],
        out_shape=[jax.ShapeDtypeStruct((B, G, NC, LANES), BF16),
                   jax.ShapeDtypeStruct((B, G, V_ROWS, NC), BF16)],
        compiler_params=_cparams(("arbitrary", "arbitrary")),
        name="nsa_compress",
    )(kc_raw.reshape(B, G, NC, chunk_w), vc_raw.reshape(B, G, NC, chunk_w),
      split_pe(p["nsa_k_pe"][j]), split_w1(p["nsa_k_w1"][j]),
      jnp.zeros((CMP_HIDDEN, LANES), BF16).at[:, :DH].set(p["nsa_k_w2"][j].astype(BF16)),
      split_pe(p["nsa_v_pe"][j]), split_w1(p["nsa_v_w1"][j]), p["nsa_v_w2"][j].astype(BF16))

    ctab_rows = 2 * NC + LANES
    tab, ctab, far = pl.pallas_call(
        _bias_table_kernel,
        grid=(G,),
        in_specs=[pl.BlockSpec(memory_space=pltpu.SMEM)],
        out_specs=[pl.BlockSpec((1, N_SLOTS, QT, QL), lambda g: (g, 0, 0, 0)),
                   pl.BlockSpec((1, ctab_rows, QL), lambda g: (g, 0, 0)),
                   pl.BlockSpec((1, 1, QL), lambda g: (g, 0, 0))],
        out_shape=[jax.ShapeDtypeStruct((G, N_SLOTS, QT, QL), F32),
                   jax.ShapeDtypeStruct((G, ctab_rows, QL), F32),
                   jax.ShapeDtypeStruct((G, 1, QL), F32)],
        compiler_params=_cparams(("arbitrary",)),
        name="nsa_bias_tables",
    )(p["rel_table"])

    nw = WINDOW // QT + 1
    kw_specs = [pl.BlockSpec((1, 1, QT, LANES), functools.partial(
        lambda b, g, a, d: (b, g, jnp.maximum(a - d, 0), 0), d=d)) for d in range(nw)]
    vwt_specs = [pl.BlockSpec((1, 1, V_ROWS, QT), functools.partial(
        lambda b, g, a, d: (b, g, 0, jnp.maximum(a - d, 0)), d=d)) for d in range(nw)]
    o = pl.pallas_call(
        _nsa_attn_wrapper,
        grid=(B, G, S // QT),
        in_specs=[pl.BlockSpec((1, QT, HPG * DH), lambda b, g, a: (b, a, g)),
                  pl.BlockSpec((1, 1, S, LANES), lambda b, g, a: (b, g, 0, 0)),
                  pl.BlockSpec((1, 1, V_ROWS, S), lambda b, g, a: (b, g, 0, 0)),
                  *kw_specs, *vwt_specs,
                  pl.BlockSpec((1, 1, NC, LANES), lambda b, g, a: (b, g, 0, 0)),
                  pl.BlockSpec((1, 1, V_ROWS, NC), lambda b, g, a: (b, g, 0, 0)),
                  pl.BlockSpec((1, 1, 3 * HPG, QT), lambda b, g, a: (b, g, 0, a)),
                  pl.BlockSpec((1, N_SLOTS, QT, QL), lambda b, g, a: (g, 0, 0, 0)),
                  pl.BlockSpec((1, ctab_rows, QL), lambda b, g, a: (g, 0, 0)),
                  pl.BlockSpec((1, 1, QL), lambda b, g, a: (g, 0, 0))],
        out_specs=pl.BlockSpec((1, QT, HPG * DH), lambda b, g, a: (b, a, g)),
        out_shape=jax.ShapeDtypeStruct((B, S, D), BF16),
        scratch_shapes=[pltpu.VMEM((2, LANES, QL), BF16),
                        pltpu.VMEM((SEL_STREAMS * SEL_SUB * QT, QL), F32),
                        pltpu.VMEM((SEL_STREAMS * SEL_SUB * QT, QL), F32),
                        pltpu.VMEM((NC, QT), F32), pltpu.VMEM((n_sel, QL), F32),
                        pltpu.VMEM((2, DH, QL), F32)],
        compiler_params=_cparams(("arbitrary", "arbitrary", "arbitrary")),
        name="nsa_attention",
    )(q, ks, vst, *([kw] * nw), *([vwt] * nw), kc, vct, gt.reshape(B, G, 3 * HPG, S), tab, ctab, far)

    return pl.pallas_call(
        _out_proj_kernel,
        grid=(B, S // T),
        in_specs=[pl.BlockSpec((1, T, D), lambda b, i: (b, i, 0)),
                  pl.BlockSpec((1, T, D), lambda b, i: (b, i, 0)), _mod_spec(layer),
                  pl.BlockSpec((D, D), lambda b, i: (0, 0)), _vec_spec(layer)],
        out_specs=pl.BlockSpec((1, T, D), lambda b, i: (b, i, 0)),
        out_shape=jax.ShapeDtypeStruct((B, S, D), F32),
        compiler_params=_cparams(("arbitrary", "arbitrary")),
        name="nsa_out_proj",
    )(o, x, mod, p["nsa_w_out"][j].astype(BF16), p["mix_post_g"])


def _as_rows(v):
    return v.reshape(v.shape[0], 1, v.shape[-1])


def kernel(x, c, mix_pre_g, mix_post_g, mix_ada_w, mix_ada_b, ffn_pre_g, ffn_post_g, ffn_ada_w, ffn_ada_b, conf_w_pw1, conf_b_pw1, conf_w_dw, conf_b_dw, conf_ln_g, conf_ln_b, conf_w_pw2, conf_b_pw2, sc_w_in, sc_w_conv, sc_w_out, nsa_w_in, nsa_k_pe, nsa_k_w1, nsa_k_w2, nsa_v_pe, nsa_v_w1, nsa_v_w2, nsa_w_out, rel_table, dense_w_gate, dense_w_up, dense_w_down, moe_w_router, moe_b_router, moe_w_gate, moe_w_up, moe_w_down):
    p = dict(
        mix_pre_g=_as_rows(mix_pre_g), mix_post_g=_as_rows(mix_post_g),
        ffn_pre_g=_as_rows(ffn_pre_g), ffn_post_g=_as_rows(ffn_post_g),
        conf_w_pw1=conf_w_pw1, conf_b_pw1=_as_rows(conf_b_pw1), conf_w_dw=conf_w_dw,
        conf_b_dw=_as_rows(conf_b_dw), conf_ln_g=_as_rows(conf_ln_g), conf_ln_b=_as_rows(conf_ln_b),
        conf_w_pw2=conf_w_pw2, conf_b_pw2=_as_rows(conf_b_pw2),
        sc_w_in=sc_w_in, sc_w_conv=sc_w_conv, sc_w_out=sc_w_out,
        nsa_w_in=nsa_w_in, nsa_k_pe=nsa_k_pe, nsa_k_w1=nsa_k_w1, nsa_k_w2=nsa_k_w2,
        nsa_v_pe=nsa_v_pe, nsa_v_w1=nsa_v_w1, nsa_v_w2=nsa_v_w2, nsa_w_out=nsa_w_out,
        rel_table=rel_table,
        dense_w_gate=dense_w_gate, dense_w_up=dense_w_up, dense_w_down=dense_w_down,
        moe_w_router=moe_w_router, moe_b_router=moe_b_router,
        moe_w_gate=moe_w_gate, moe_w_up=moe_w_up, moe_w_down=moe_w_down,
    )
    mix_mod, ffn_mod = _adaln(c, mix_ada_w, mix_ada_b, ffn_ada_w, ffn_ada_b)
    mixers = (_conformer_layer, _short_conv_layer, _nsa_layer)
    for i in range(DEPTH):
        x = mixers[i % 3](x, mix_mod, i, i // 3, p)
        ffn = _dense_ffn_layer if i % 2 == 0 else _moe_ffn_layer
        x = ffn(x, ffn_mod, i, i // 2, p)
    return x
```

```python
import functools
import math

import numpy as np
import jax
import jax.numpy as jnp
from jax import lax
from jax.experimental import pallas as pl
from jax.experimental.pallas import tpu as pltpu

F32 = jnp.float32
BF16 = jnp.bfloat16
I32 = jnp.int32

D_MODEL = 1024
DEPTH = 4
RMS_EPS = 1e-6
LN_EPS = 1e-5
NEG = -1e30

CONF_KERNEL = 31
SHORT_KERNEL = 3
N_HEADS = 16
HEAD_DIM = 64
N_GROUPS = 4
HPG = 4
CMP_LEN = 32
CMP_STRIDE = 16
CMP_HIDDEN = 128
SEL_BLOCK = 64
SEL_TOPK = 16
N_LOCAL_BLOCKS = 2
WINDOW = 512
REL_BUCKETS = 32
REL_MAX_DIST = 2048
D_FF_DENSE = 2816
N_EXPERTS = 8
D_FF_EXPERT = 3584

LANES = 128
SUBLANES = 8
VMEM_LIMIT = 56 * 1024 * 1024

ADA_TN = 1024
TOK_TILE = 512
CONV_TILE = 256
CONV_HALO = 32
FFN_TILE = 1024
FFN_TF = 256
MOE_TF = 512
MOE_ROWS = 1024
ROUTE_TILE = 512
ROW_DMA_TILE = 256
ROW_DMA_UNROLL = 8
DEST_TILE = 2048
QT = 128
QL = HPG * QT
NEAR_SLOTS = 13
SLOT_FAR = 13
SLOT_WIN = 14
SLOT_NONE = 15
N_SLOTS = 16
CMP_NEAR = 112
SEL_SUB = 4
SEL_STREAMS = 2
SEL_PEN_ROWS = SEL_STREAMS * SEL_SUB * QT // SEL_BLOCK
K_ONE = HEAD_DIM
K_PEN = 80
V_ROWS = 80
LOG2E = 1.4426950408889634


def _bucket_thresholds():
    n = np.arange(0, 4 * REL_MAX_DIST)
    max_exact = REL_BUCKETS // 2
    nf = np.maximum(n, 1).astype(np.float32)
    large = max_exact + (np.log(nf / np.float32(max_exact))
                         / np.float32(math.log(REL_MAX_DIST / max_exact))
                         * np.float32(REL_BUCKETS - max_exact)).astype(np.int32)
    large = np.minimum(large, REL_BUCKETS - 1)
    b = np.where(n < max_exact, n, large)
    return [int(np.argmax(b >= k)) for k in range(REL_BUCKETS)]


BUCKET_THR = _bucket_thresholds()
assert BUCKET_THR[REL_BUCKETS - 1] <= QT * NEAR_SLOTS - (QT - 1)
assert BUCKET_THR[REL_BUCKETS - 1] <= CMP_STRIDE * (CMP_NEAR - 8 + 1) - CMP_LEN + 1


def _cparams(sem):
    return pltpu.CompilerParams(dimension_semantics=sem, vmem_limit_bytes=VMEM_LIMIT)


def _bdot(a, b):
    return jnp.dot(a.astype(BF16), b.astype(BF16), preferred_element_type=F32)


def _sigmoid(x):
    return 1.0 / (1.0 + jnp.exp(-x))


def _rms(x, g):
    return x * lax.rsqrt(jnp.mean(x * x, axis=-1, keepdims=True) + RMS_EPS) * g


def _norm_mod(x, g, mod):
    return _rms(x, g) * (1.0 + mod[1:2]) + mod[0:1]


def _adaln_kernel(c_ref, wm_ref, bm_ref, wf_ref, bf_ref, om_ref, of_ref):
    c = c_ref[...]
    s = c * _sigmoid(c)
    om_ref[0] = jnp.dot(s, wm_ref[0], precision=lax.Precision.HIGHEST,
                        preferred_element_type=F32) + bm_ref[0]
    of_ref[0] = jnp.dot(s, wf_ref[0], precision=lax.Precision.HIGHEST,
                        preferred_element_type=F32) + bf_ref[0]


def _adaln(c, mix_w, mix_b, ffn_w, ffn_b):
    B, D = c.shape
    depth = mix_w.shape[0]
    cp = jnp.zeros((SUBLANES, D), F32).at[:B].set(c)
    w_spec = pl.BlockSpec((1, D, ADA_TN), lambda l, j: (l, 0, j))
    b_spec = pl.BlockSpec((1, 1, ADA_TN), lambda l, j: (l, 0, j))
    o_spec = pl.BlockSpec((1, SUBLANES, ADA_TN), lambda l, j: (l, 0, j))
    om, of = pl.pallas_call(
        _adaln_kernel,
        grid=(depth, 3 * D // ADA_TN),
        in_specs=[pl.BlockSpec((SUBLANES, D), lambda l, j: (0, 0)), w_spec, b_spec, w_spec, b_spec],
        out_specs=[o_spec, o_spec],
        out_shape=[jax.ShapeDtypeStruct((depth, SUBLANES, 3 * D), F32)] * 2,
        compiler_params=_cparams(("arbitrary", "arbitrary")),
        name="adaln",
    )(cp, mix_w, mix_b.reshape(depth, 1, 3 * D), ffn_w, ffn_b.reshape(depth, 1, 3 * D))
    return (om[:, :B].reshape(depth, B, 3, D), of[:, :B].reshape(depth, B, 3, D))


def _mod_spec(layer):
    return pl.BlockSpec((1, 1, 3, D_MODEL), lambda b, *_: (layer, b, 0, 0))


def _vec_spec(layer, width=D_MODEL):
    return pl.BlockSpec((1, 1, width), lambda *_: (layer, 0, 0))


def _post_residual(x, y, post_g, mod):
    return x + mod[2:3] * _rms(y, post_g)


def _conf_a_kernel(x_ref, mod_ref, g_ref, w_ref, b_ref, u_ref):
    D = D_MODEL
    h = _norm_mod(x_ref[0], g_ref[0], mod_ref[0, 0])
    z = _bdot(h, w_ref[...]) + b_ref[0]
    u_ref[0] = z[:, :D] * _sigmoid(z[:, D:])


def _conf_b_kernel(u_ref, up_ref, x_ref, mod_ref, wdw_ref, bdw_ref, lng_ref, lnb_ref,
                   w2_ref, b2_ref, pg_ref, o_ref, ext_ref, conv_ref):
    T, D, K = CONV_TILE, D_MODEL, CONF_KERNEL
    i = pl.program_id(1)
    keep = (i > 0).astype(F32)
    ext_ref[0:CONV_HALO, :] = up_ref[0] * keep
    ext_ref[CONV_HALO:, :] = u_ref[0]
    rb, cb = 64, 256
    off = CONV_HALO - (K - 1)

    def col_body(c, carry):
        c0 = pl.multiple_of(c * cb, cb)
        bias = bdw_ref[0, :, pl.ds(c0, cb)]
        for r in range(T // rb):
            acc = jnp.zeros((rb // SUBLANES, SUBLANES, cb), F32) + bias[None]
            for res in range(SUBLANES):
                offs = [o for o in range(off, off + K) if o % SUBLANES == res]
                lo, hi = offs[0], offs[-1]
                win = ext_ref[pl.ds(r * rb + lo, hi - lo + rb), pl.ds(c0, cb)]
                win = win.reshape((hi - lo + rb) // SUBLANES, SUBLANES, cb)
                for o in offs:
                    i0 = (o - lo) // SUBLANES
                    w = wdw_ref[0, o - off, :, pl.ds(c0, cb)]
                    acc = acc + w[None] * win[i0:i0 + rb // SUBLANES]
            conv_ref[pl.ds(r * rb, rb), pl.ds(c0, cb)] = acc.reshape(rb, cb)
        return carry

    lax.fori_loop(0, D // cb, col_body, 0)
    v = conv_ref[...]
    mu = jnp.mean(v, axis=-1, keepdims=True)
    var = jnp.mean(jnp.square(v - mu), axis=-1, keepdims=True)
    ln = (v - mu) * lax.rsqrt(var + LN_EPS) * lng_ref[0] + lnb_ref[0]
    act = ln * _sigmoid(ln)
    y = _bdot(act, w2_ref[...]) + b2_ref[0]
    o_ref[0] = _post_residual(x_ref[0], y, pg_ref[0], mod_ref[0, 0])


def _conformer_layer(x, mod, layer, j, p):
    B, S, D = x.shape
    T = TOK_TILE
    u = pl.pallas_call(
        _conf_a_kernel,
        grid=(B, S // T),
        in_specs=[pl.BlockSpec((1, T, D), lambda b, i: (b, i, 0)), _mod_spec(layer),
                  _vec_spec(layer), pl.BlockSpec((D, 2 * D), lambda b, i: (0, 0)),
                  _vec_spec(j, 2 * D)],
        out_specs=pl.BlockSpec((1, T, D), lambda b, i: (b, i, 0)),
        out_shape=jax.ShapeDtypeStruct((B, S, D), F32),
        compiler_params=_cparams(("arbitrary", "arbitrary")),
        name="conf_pw1_glu",
    )(x, mod, p["mix_pre_g"], p["conf_w_pw1"][j].astype(BF16), p["conf_b_pw1"])
    T = CONV_TILE
    hb = T // CONV_HALO
    return pl.pallas_call(
        _conf_b_kernel,
        grid=(B, S // T),
        in_specs=[pl.BlockSpec((1, T, D), lambda b, i: (b, i, 0)),
                  pl.BlockSpec((1, CONV_HALO, D), lambda b, i: (b, jnp.maximum(i * hb - 1, 0), 0)),
                  pl.BlockSpec((1, T, D), lambda b, i: (b, i, 0)), _mod_spec(layer),
                  pl.BlockSpec((1, CONF_KERNEL, SUBLANES, D), lambda b, i: (j, 0, 0, 0)),
                  _vec_spec(j), _vec_spec(j), _vec_spec(j),
                  pl.BlockSpec((D, D), lambda b, i: (0, 0)), _vec_spec(j), _vec_spec(layer)],
        out_specs=pl.BlockSpec((1, T, D), lambda b, i: (b, i, 0)),
        out_shape=jax.ShapeDtypeStruct((B, S, D), F32),
        scratch_shapes=[pltpu.VMEM((T + CONV_HALO, D), F32), pltpu.VMEM((T, D), F32)],
        compiler_params=_cparams(("arbitrary", "arbitrary")),
        name="conf_conv_pw2",
    )(u, u, x, mod, jnp.broadcast_to(p["conf_w_dw"][:, :, None, :], (*p["conf_w_dw"].shape[:2], SUBLANES, D)),
      p["conf_b_dw"], p["conf_ln_g"], p["conf_ln_b"],
      p["conf_w_pw2"][j].astype(BF16), p["conf_b_pw2"], p["mix_post_g"])


def _short_conv_kernel(x_ref, mod_ref, g_ref, win_ref, wc_ref, wout_ref, pg_ref, o_ref, ext_ref):
    T, D = TOK_TILE, D_MODEL
    i = pl.program_id(1)

    @pl.when(i == 0)
    def _():
        ext_ref[0:SUBLANES, :] = jnp.zeros((SUBLANES, D), F32)

    x = x_ref[0]
    mod = mod_ref[0, 0]
    h = _norm_mod(x, g_ref[0], mod)
    z = _bdot(h, win_ref[...])
    v, gate_b, gate_c = z[:, :D], z[:, D:2 * D], z[:, 2 * D:]
    ext_ref[SUBLANES:, :] = gate_c * v
    wc = wc_ref[0]
    u = (wc[0:1] * ext_ref[pl.ds(SUBLANES - 2, T), :] + wc[1:2] * ext_ref[pl.ds(SUBLANES - 1, T), :]
         + wc[2:3] * ext_ref[pl.ds(SUBLANES, T), :])
    ext_ref[0:SUBLANES, :] = ext_ref[pl.ds(T, SUBLANES), :]
    y = _bdot(gate_b * u, wout_ref[...])
    o_ref[0] = _post_residual(x, y, pg_ref[0], mod)


def _short_conv_layer(x, mod, layer, j, p):
    B, S, D = x.shape
    T = TOK_TILE
    return pl.pallas_call(
        _short_conv_kernel,
        grid=(B, S // T),
        in_specs=[pl.BlockSpec((1, T, D), lambda b, i: (b, i, 0)), _mod_spec(layer), _vec_spec(layer),
                  pl.BlockSpec((D, 3 * D), lambda b, i: (0, 0)),
                  pl.BlockSpec((1, SHORT_KERNEL, D), lambda b, i: (j, 0, 0)),
                  pl.BlockSpec((D, D), lambda b, i: (0, 0)), _vec_spec(layer)],
        out_specs=pl.BlockSpec((1, T, D), lambda b, i: (b, i, 0)),
        out_shape=jax.ShapeDtypeStruct((B, S, D), F32),
        scratch_shapes=[pltpu.VMEM((T + SUBLANES, D), F32)],
        compiler_params=_cparams(("arbitrary", "arbitrary")),
        name="short_conv",
    )(x, mod, p["mix_pre_g"], p["sc_w_in"][j].astype(BF16), p["sc_w_conv"],
      p["sc_w_out"][j].astype(BF16), p["mix_post_g"])


def _swiglu_step(h, wg_ref, wu_ref, wd_ref, acc_ref):
    a = _bdot(h, wg_ref[...])
    u = _bdot(h, wu_ref[...])
    acc_ref[...] += _bdot((a * _sigmoid(a)) * u, wd_ref[...])


def _dense_ffn_kernel(x_ref, mod_ref, g_ref, wg_ref, wu_ref, wd_ref, pg_ref, o_ref, h_ref, acc_ref):
    f = pl.program_id(2)

    @pl.when(f == 0)
    def _():
        h_ref[...] = _norm_mod(x_ref[0], g_ref[0], mod_ref[0, 0]).astype(BF16)
        acc_ref[...] = jnp.zeros_like(acc_ref)

    _swiglu_step(h_ref[...], wg_ref.at[0], wu_ref.at[0], wd_ref.at[0], acc_ref)

    @pl.when(f == pl.num_programs(2) - 1)
    def _():
        o_ref[0] = _post_residual(x_ref[0], acc_ref[...], pg_ref[0], mod_ref[0, 0])


def _dense_ffn_layer(x, mod, layer, j, p):
    B, S, D = x.shape
    T, TF, F = min(FFN_TILE, S), FFN_TF, D_FF_DENSE
    return pl.pallas_call(
        _dense_ffn_kernel,
        grid=(B, S // T, F // TF),
        in_specs=[pl.BlockSpec((1, T, D), lambda b, i, f: (b, i, 0)), _mod_spec(layer), _vec_spec(layer),
                  pl.BlockSpec((1, D, TF), lambda b, i, f: (j, 0, f)),
                  pl.BlockSpec((1, D, TF), lambda b, i, f: (j, 0, f)),
                  pl.BlockSpec((1, TF, D), lambda b, i, f: (j, f, 0)), _vec_spec(layer)],
        out_specs=pl.BlockSpec((1, T, D), lambda b, i, f: (b, i, 0)),
        out_shape=jax.ShapeDtypeStruct((B, S, D), F32),
        scratch_shapes=[pltpu.VMEM((T, D), BF16), pltpu.VMEM((T, D), F32)],
        compiler_params=_cparams(("arbitrary", "arbitrary", "arbitrary")),
        name="dense_swiglu",
    )(x, mod, p["ffn_pre_g"], p["dense_w_gate"], p["dense_w_up"], p["dense_w_down"], p["ffn_post_g"])


def _router_kernel(x_ref, mod_ref, g_ref, wr_ref, br_ref, h_ref, ri_ref, rw_ref, cnt_ref, base_ref):
    T = ROUTE_TILE
    i = pl.program_id(0)

    @pl.when(i == 0)
    def _():
        base_ref[...] = jnp.zeros_like(base_ref)

    h = _norm_mod(x_ref[...], g_ref[0], mod_ref[0, 0])
    h_ref[...] = h
    logits = jnp.dot(h, wr_ref[...], precision=lax.Precision.HIGHEST,
                     preferred_element_type=F32) + br_ref[0]
    lane = lax.broadcasted_iota(I32, (T, LANES), 1).astype(F32)
    logits = jnp.where(lane < N_EXPERTS, logits, -jnp.inf)
    l0 = jnp.max(logits, axis=1, keepdims=True)
    e0 = jnp.min(jnp.where(logits == l0, lane, float(LANES)), axis=1, keepdims=True)
    rest = jnp.where(lane == e0, -jnp.inf, logits)
    l1 = jnp.max(rest, axis=1, keepdims=True)
    e1 = jnp.min(jnp.where(rest == l1, lane, float(LANES)), axis=1, keepdims=True)
    ex = jnp.exp(l1 - l0)
    w0 = 1.0 / (1.0 + ex)
    w1 = ex / (1.0 + ex)
    onehot = ((lane == e0) | (lane == e1)).astype(F32)
    row = lax.broadcasted_iota(I32, (T, T), 0)
    col = lax.broadcasted_iota(I32, (T, T), 1)
    tri = jnp.where(row > col, 1.0, 0.0).astype(BF16)
    before = jnp.dot(tri, onehot.astype(BF16), preferred_element_type=F32) + base_ref[...]
    r0 = jnp.sum(jnp.where(lane == e0, before, 0.0), axis=1, keepdims=True)
    r1 = jnp.sum(jnp.where(lane == e1, before, 0.0), axis=1, keepdims=True)
    base_ref[...] += jnp.sum(onehot, axis=0, keepdims=True)
    cnt_ref[...] = base_ref[...].astype(I32)
    ri = jnp.where(lane == 0, e0, jnp.where(lane == 1, e1, jnp.where(lane == 2, r0, r1)))
    ri_ref[...] = ri[:, :SUBLANES].astype(I32)
    rw_ref[...] = jnp.where(lane == 0, w0, w1)[:, :SUBLANES]


def _dest_kernel(pstart_ref, ri_ref, dst_ref):
    ri = ri_ref[...]
    lane = lax.broadcasted_iota(I32, ri.shape, 1)
    expert = jnp.where(lane == 0, ri[:, 0:1], ri[:, 1:2])
    rank = jnp.where(lane == 0, ri[:, 2:3], ri[:, 3:4])
    start = jnp.zeros(ri.shape, I32)
    for e in range(N_EXPERTS):
        start = jnp.where(expert == e, pstart_ref[e], start)
    dst_ref[...] = start + rank


def _dispatch_kernel(dst_ref, h_ref, zero_ref, buf_ref, stage_ref, sem):
    del zero_ref
    T = ROW_DMA_TILE
    i = pl.program_id(0)
    slot = i % 2
    stage_ref[slot] = h_ref[...]

    def row_copy(t, dest, s):
        return pltpu.make_async_copy(stage_ref.at[s, pl.ds(t, 1)], buf_ref.at[pl.ds(dest, 1)], sem.at[s])

    def issue(t, carry):
        for k in range(2):
            row_copy(t, dst_ref[t, k], slot).start()
        return carry

    lax.fori_loop(0, T, issue, 0, unroll=ROW_DMA_UNROLL)

    def drain(s):
        def body(t, carry):
            for k in range(2):
                row_copy(0, 0, s).wait()
            return carry

        lax.fori_loop(0, T, body, 0, unroll=ROW_DMA_UNROLL)

    @pl.when(i > 0)
    def _():
        drain(1 - slot)

    @pl.when(i == pl.num_programs(0) - 1)
    def _():
        drain(slot)


def _expert_kernel(blk_e_ref, nblk_ref, xb_ref, wg_ref, wu_ref, wd_ref, o_ref, h_ref, acc_ref):
    del blk_e_ref
    i = pl.program_id(0)
    f = pl.program_id(1)
    live = i < nblk_ref[0]

    @pl.when(f == 0)
    def _():
        h_ref[...] = xb_ref[...].astype(BF16)
        acc_ref[...] = jnp.zeros_like(acc_ref)

    @pl.when(live)
    def _():
        _swiglu_step(h_ref[...], wg_ref.at[0, 0], wu_ref.at[0, 0], wd_ref.at[0, 0], acc_ref)

    @pl.when(f == pl.num_programs(1) - 1)
    def _():
        o_ref[...] = acc_ref[...]


def _combine_kernel(x_ref, mod_ref, pg_ref, dst_ref, dst_next_ref, rw_ref, yb_ref, o_ref, y_ref, sem):
    T = ROW_DMA_TILE
    i = pl.program_id(0)
    slot = i % 2

    def row_copy(src, t, k, s):
        return pltpu.make_async_copy(yb_ref.at[pl.ds(src, 1)], y_ref.at[s, k, pl.ds(t, 1)], sem.at[s])

    def gather(idx_ref, s):
        def body(t, carry):
            for k in range(2):
                row_copy(idx_ref[t, k], t, k, s).start()
            return carry

        lax.fori_loop(0, T, body, 0, unroll=ROW_DMA_UNROLL)

    @pl.when(i == 0)
    def _():
        gather(dst_ref, slot)

    @pl.when(i + 1 < pl.num_programs(0))
    def _():
        gather(dst_next_ref, 1 - slot)

    def drain(t, carry):
        for k in range(2):
            row_copy(0, 0, k, slot).wait()
        return carry

    lax.fori_loop(0, T, drain, 0, unroll=ROW_DMA_UNROLL)
    rw = rw_ref[...]
    y = rw[:, 0:1] * y_ref[slot, 0] + rw[:, 1:2] * y_ref[slot, 1]
    o_ref[...] = _post_residual(x_ref[...], y, pg_ref[0], mod_ref[0, 0])


def _moe_ffn_layer(x, mod, layer, j, p):
    B, S, D = x.shape
    N = B * S
    T = ROUTE_TILE
    tiles_per_seq = S // T
    xt = x.reshape(N, D)
    wr = jnp.zeros((D, LANES), F32).at[:, :N_EXPERTS].set(p["moe_w_router"][j])
    br = jnp.zeros((1, 1, LANES), F32).at[0, 0, :N_EXPERTS].set(p["moe_b_router"][j])
    h, ri, rw, cnt = pl.pallas_call(
        _router_kernel,
        grid=(N // T,),
        in_specs=[pl.BlockSpec((T, D), lambda i: (i, 0)),
                  pl.BlockSpec((1, 1, 3, D), lambda i: (layer, i // tiles_per_seq, 0, 0)),
                  _vec_spec(layer), pl.BlockSpec((D, LANES), lambda i: (0, 0)), _vec_spec(0, LANES)],
        out_specs=[pl.BlockSpec((T, D), lambda i: (i, 0)), pl.BlockSpec((T, SUBLANES), lambda i: (i, 0)),
                   pl.BlockSpec((T, SUBLANES), lambda i: (i, 0)), pl.BlockSpec((1, LANES), lambda i: (0, 0))],
        out_shape=[jax.ShapeDtypeStruct((N, D), F32), jax.ShapeDtypeStruct((N, SUBLANES), I32),
                   jax.ShapeDtypeStruct((N, SUBLANES), F32), jax.ShapeDtypeStruct((1, LANES), I32)],
        scratch_shapes=[pltpu.VMEM((1, LANES), F32)],
        compiler_params=_cparams(("arbitrary",)),
        name="moe_router",
    )(xt, mod, p["ffn_pre_g"], wr, br)

    R = MOE_ROWS
    counts = cnt[0, :N_EXPERTS]
    padded = (counts + R - 1) // R * R
    pend = jnp.cumsum(padded)
    pstart = (pend - padded).astype(I32)
    n_blocks = -(-(2 * N + N_EXPERTS * (R - 1)) // R)
    blk_e = jnp.minimum(jnp.sum((jnp.arange(n_blocks) * R)[:, None] >= pend[None, :], axis=1),
                        N_EXPERTS - 1).astype(I32)
    n_live = (pend[-1] // R).astype(I32).reshape(1)
    rows = n_blocks * R

    TR = DEST_TILE
    dst = pl.pallas_call(
        _dest_kernel,
        grid_spec=pltpu.PrefetchScalarGridSpec(
            num_scalar_prefetch=1,
            grid=(N // TR,),
            in_specs=[pl.BlockSpec((TR, SUBLANES), lambda i, ps: (i, 0))],
            out_specs=pl.BlockSpec((TR, SUBLANES), lambda i, ps: (i, 0)),
        ),
        out_shape=jax.ShapeDtypeStruct((N, SUBLANES), I32),
        compiler_params=_cparams(("arbitrary",)),
        name="moe_dest",
    )(pstart, ri)

    TD = ROW_DMA_TILE
    n_dma_tiles = N // TD
    dst_spec = pl.BlockSpec((TD, SUBLANES), lambda i: (i, 0), memory_space=pltpu.SMEM)
    buf = pl.pallas_call(
        _dispatch_kernel,
        grid=(n_dma_tiles,),
        in_specs=[dst_spec, pl.BlockSpec((TD, D), lambda i: (i, 0)), pl.BlockSpec(memory_space=pl.ANY)],
        out_specs=pl.BlockSpec(memory_space=pl.ANY),
        out_shape=jax.ShapeDtypeStruct((rows, D), F32),
        scratch_shapes=[pltpu.VMEM((2, TD, D), F32), pltpu.SemaphoreType.DMA((2,))],
        input_output_aliases={2: 0},
        compiler_params=_cparams(("arbitrary",)),
        name="moe_dispatch",
    )(dst, h, jnp.zeros((rows, D), F32))

    TF, F = MOE_TF, D_FF_EXPERT
    yb = pl.pallas_call(
        _expert_kernel,
        grid_spec=pltpu.PrefetchScalarGridSpec(
            num_scalar_prefetch=2,
            grid=(n_blocks, F // TF),
            in_specs=[pl.BlockSpec((R, D), lambda i, f, be, nb: (i, 0)),
                      pl.BlockSpec((1, 1, D, TF), lambda i, f, be, nb: (j, be[i], 0, jnp.where(i < nb[0], f, 0))),
                      pl.BlockSpec((1, 1, D, TF), lambda i, f, be, nb: (j, be[i], 0, jnp.where(i < nb[0], f, 0))),
                      pl.BlockSpec((1, 1, TF, D), lambda i, f, be, nb: (j, be[i], jnp.where(i < nb[0], f, 0), 0))],
            out_specs=pl.BlockSpec((R, D), lambda i, f, be, nb: (i, 0)),
            scratch_shapes=[pltpu.VMEM((R, D), BF16), pltpu.VMEM((R, D), F32)],
        ),
        out_shape=jax.ShapeDtypeStruct((rows, D), F32),
        compiler_params=_cparams(("arbitrary", "arbitrary")),
        name="moe_experts",
    )(blk_e, n_live, buf, p["moe_w_gate"], p["moe_w_up"], p["moe_w_down"])

    tiles_per_seq_d = S // TD
    out = pl.pallas_call(
        _combine_kernel,
        grid=(n_dma_tiles,),
        in_specs=[pl.BlockSpec((TD, D), lambda i: (i, 0)),
                  pl.BlockSpec((1, 1, 3, D), lambda i: (layer, i // tiles_per_seq_d, 0, 0)),
                  pl.BlockSpec((1, 1, D), lambda i: (layer, 0, 0)),
                  dst_spec,
                  pl.BlockSpec((TD, SUBLANES), lambda i: (jnp.minimum(i + 1, n_dma_tiles - 1), 0),
                               memory_space=pltpu.SMEM),
                  pl.BlockSpec((TD, SUBLANES), lambda i: (i, 0)),
                  pl.BlockSpec(memory_space=pl.ANY)],
        out_specs=pl.BlockSpec((TD, D), lambda i: (i, 0)),
        out_shape=jax.ShapeDtypeStruct((N, D), F32),
        scratch_shapes=[pltpu.VMEM((2, 2, TD, D), F32), pltpu.SemaphoreType.DMA((2,))],
        compiler_params=_cparams(("arbitrary",)),
        name="moe_combine",
    )(xt, mod, p["ffn_post_g"], dst, dst, rw, yb)
    return out.reshape(B, S, D)


NSA_W = D_MODEL + 2 * N_GROUPS * LANES + 4 * N_GROUPS * HEAD_DIM + LANES


def _ones_rows(n):
    r = lax.broadcasted_iota(I32, (V_ROWS - HEAD_DIM, n), 0)
    return jnp.where(r == 0, 1.0, 0.0)


def _nsa_in_kernel(x_ref, mod_ref, g_ref, w_ref, q_ref, ks_ref, vst_ref, kw_ref, vwt_ref,
                   kc_ref, vc_ref, gt_ref):
    D, G, DH = D_MODEL, N_GROUPS, HEAD_DIM
    KV = G * DH
    T = x_ref.shape[1]
    h = _norm_mod(x_ref[0], g_ref[0], mod_ref[0, 0])
    z = _bdot(h, w_ref[...])
    zq_t = z[:, :D].T
    for g in range(G):
        for i in range(T // QT):
            q_ref[0, g, i] = jnp.concatenate(
                [zq_t[(g * HPG + hh) * DH:(g * HPG + hh + 1) * DH, i * QT:(i + 1) * QT] for hh in range(HPG)],
                axis=1).astype(BF16)
    lane = lax.broadcasted_iota(I32, (T, LANES), 1)
    pos = pl.program_id(1) * T + lax.broadcasted_iota(I32, (T, LANES), 0)
    one_col = jnp.where(lane == K_ONE, 1.0, 0.0)
    blk_col = jnp.where(lane == K_PEN + (pos // SEL_BLOCK) % SEL_PEN_ROWS, 1.0, 0.0)
    o = D
    for g in range(G):
        ks_ref[0, g] = (z[:, o + g * LANES:o + (g + 1) * LANES] + (one_col + blk_col)).astype(BF16)
    o += G * LANES
    for g in range(G):
        kw_ref[0, g] = (z[:, o + g * LANES:o + (g + 1) * LANES] + one_col).astype(BF16)
    o += G * LANES
    v_s, v_w = z[:, o:o + KV], z[:, o + KV:o + 2 * KV]
    k_c, v_c = z[:, o + 2 * KV:o + 3 * KV], z[:, o + 3 * KV:o + 4 * KV]
    gates = _sigmoid(z[:, o + 4 * KV:])
    vst = v_s.T
    vwt = v_w.T
    ones = _ones_rows(T)
    for g in range(G):
        sl = slice(g * DH, (g + 1) * DH)
        kc_ref[0, g] = k_c[:, sl]
        vc_ref[0, g] = v_c[:, sl]
        vst_ref[0, g] = jnp.concatenate([vst[sl, :], ones], axis=0).astype(BF16)
        vwt_ref[0, g] = jnp.concatenate([vwt[sl, :], ones], axis=0).astype(BF16)
    gt_ref[0] = gates.T[:3 * N_HEADS, :]


def _gelu_tanh(x):
    return x * (0.5 * (1.0 + jnp.tanh(math.sqrt(2.0 / math.pi) * (x + 0.044715 * (x * x * x)))))


def _compress_kernel(kc_ref, vc_ref, kpe_ref, kw1_ref, kw2_ref, vpe_ref, vw1_ref, vw2_ref,
                     kco_ref, vcto_ref):
    nc = kc_ref.shape[2]

    def comp(c, pe_ref, w1_ref, w2_ref):
        a = _bdot(c + pe_ref[0:1], w1_ref[0])
        b = _bdot(c + pe_ref[1:2], w1_ref[1])
        b_next = pltpu.roll(b, nc - 1, axis=0)
        out = _bdot(_gelu_tanh(a + b_next), w2_ref[...])
        row = lax.broadcasted_iota(I32, out.shape, 0)
        return jnp.where(row < nc - 1, out, 0.0)

    kc = comp(kc_ref[0, 0], kpe_ref, kw1_ref, kw2_ref)
    lane = lax.broadcasted_iota(I32, kc.shape, 1)
    kco_ref[0, 0] = (kc + jnp.where(lane == K_ONE, 1.0, 0.0)).astype(BF16)
    vct = comp(vc_ref[0, 0], vpe_ref, vw1_ref, vw2_ref).T
    vcto_ref[0, 0] = jnp.concatenate([vct, _ones_rows(nc)], axis=0).astype(BF16)


def _bias_table_kernel(tbl_ref, tab_ref, ctab_ref, far_ref):
    g = pl.program_id(0)

    def bias_of(dist, h, far):
        v = jnp.full(dist.shape, tbl_ref[0, h], F32)
        for b in range(1, REL_BUCKETS):
            v = jnp.where(dist >= BUCKET_THR[b], tbl_ref[b, h], v)
        return jnp.where(dist >= 0, v * LOG2E - far, NEG)

    nc = (ctab_ref.shape[1] - LANES) // 2
    key = lax.broadcasted_iota(I32, (QT, QT), 0)
    qry = lax.broadcasted_iota(I32, (QT, QT), 1)
    crow = lax.broadcasted_iota(I32, (CMP_NEAR, QT), 0)
    cqry = lax.broadcasted_iota(I32, (CMP_NEAR, QT), 1)
    for hh in range(HPG):
        h = g * HPG + hh
        lanes = slice(hh * QT, (hh + 1) * QT)
        far = jnp.full((1, QT), tbl_ref[REL_BUCKETS - 1, h] * LOG2E, F32).astype(BF16).astype(F32)
        far_ref[0, :, lanes] = far

        def near_slot(slot, carry, h=h, lanes=lanes, far=far):
            tab_ref[0, slot, :, lanes] = bias_of(slot * QT + qry - key, h, far)
            return carry

        lax.fori_loop(0, NEAR_SLOTS, near_slot, 0)
        tab_ref[0, SLOT_FAR, :, lanes] = jnp.zeros((QT, QT), F32)
        tab_ref[0, SLOT_NONE, :, lanes] = jnp.full((QT, QT), NEG, F32)
        dw = WINDOW + qry - key
        tab_ref[0, SLOT_WIN, :, lanes] = jnp.where(dw < WINDOW, bias_of(dw, h, far), NEG)
        ctab_ref[0, 0:nc, lanes] = jnp.zeros((nc, QT), F32)
        dc = cqry - CMP_STRIDE * (crow - (CMP_NEAR - SUBLANES)) - (CMP_LEN - 1)
        ctab_ref[0, nc:nc + CMP_NEAR, lanes] = bias_of(dc, h, far)
        ctab_ref[0, nc + CMP_NEAR:, lanes] = jnp.full((nc + LANES - CMP_NEAR, QT), NEG, F32)


def _nsa_attn_kernel(q_ref, ks_ref, vst_ref, kw_refs, vwt_refs, kc_ref, vct_ref, gt_ref,
                     tab_ref, ctab_ref, far_ref, o_ref, qt_ref, s_even_ref, s_odd_ref, psum_ref, pen_ref, ocw_ref,
                     smax_ref):
    DH = HEAD_DIM
    a = pl.program_id(2)
    nc = kc_ref.shape[2]
    n_sel = pen_ref.shape[0]

    aug_row = lax.broadcasted_iota(I32, (LANES - DH, QL), 0)
    aug = jnp.where(aug_row == 0, far_ref[0], 0.0).astype(BF16)
    for i in range(2):
        qt_ref[i, 0:DH, :] = q_ref[0, 0, 0]
        qt_ref[i, DH:, :] = aug
    qt = qt_ref[0]

    def window():
        nw = WINDOW // QT + 1
        s_w = []
        for d in range(nw):
            slot = SLOT_WIN if d == nw - 1 else d
            if d > 0:
                slot = jnp.where(a >= d, slot, SLOT_NONE)
            s_w.append(jnp.dot(kw_refs[d][0, 0], qt, preferred_element_type=F32) + tab_ref[0, slot])
        m_w = functools.reduce(jnp.maximum, [jnp.max(s_d, axis=0, keepdims=True) for s_d in s_w])
        acc_w = functools.reduce(lambda x, y: x + y, [
            jnp.dot(vwt_refs[d][0, 0], jnp.exp2(s_w[d] - m_w).astype(BF16), preferred_element_type=F32)
            for d in range(nw)])
        ocw_ref[1] = acc_w[0:DH] * (1.0 / acc_w[DH:DH + 1])

    n_hi = (QT // CMP_STRIDE) * (a + 1)
    c0 = pl.multiple_of(nc - (n_hi - CMP_NEAR), SUBLANES)
    ratio = SEL_BLOCK // CMP_STRIDE

    def compressed_and_select(rows_c):
        rows_s = rows_c // ratio
        s = (jnp.dot(kc_ref[0, 0, 0:rows_c, :], qt, preferred_element_type=F32)
             + ctab_ref[0, pl.ds(c0, rows_c), :])
        m = jnp.max(s, axis=0, keepdims=True)
        e = jnp.exp2(s - m)
        l = jnp.sum(e, axis=0, keepdims=True)
        pn = e * jnp.where(m > 0.1 * NEG, 1.0 / l, 0.0)
        ocw_ref[0] = jnp.dot(vct_ref[0, 0, :, 0:rows_c], pn.astype(BF16), preferred_element_type=F32)[0:DH]
        psum_ref[0:rows_c, :] = pn[:, 0:QT] + pn[:, QT:2 * QT] + pn[:, 2 * QT:3 * QT] + pn[:, 3 * QT:4 * QT]

        imp = psum_ref[pl.ds(0, rows_s, stride=ratio), :]
        for jj in range(1, ratio):
            imp = imp + psum_ref[pl.ds(jj, rows_s, stride=ratio), :]
        last = psum_ref[pl.ds(ratio - 1, rows_s, stride=ratio), :]
        srow = lax.broadcasted_iota(I32, (rows_s, QT), 0)
        imp = imp + jnp.where(srow > 0, pltpu.roll(last, 1, axis=0), 0.0)
        lane_q = lax.broadcasted_iota(I32, (rows_s, QT), 1)
        cur = (QT // SEL_BLOCK) * a + lane_q // SEL_BLOCK
        back = cur - srow
        valid = back >= 0
        forced = (srow == 0) | (valid & (back < N_LOCAL_BLOCKS))
        n_forced = 1 + N_LOCAL_BLOCKS
        score = jnp.where(forced, -jnp.inf, jnp.where(valid, imp, -1.0))
        srow_f = srow.astype(F32)
        for _ in range(min(SEL_TOPK, n_sel) - n_forced):
            mx = jnp.max(score, axis=0, keepdims=True)
            first = jnp.min(jnp.where(score == mx, srow_f, float(rows_s)), axis=0, keepdims=True)
            score = jnp.where(srow_f == first, -jnp.inf, score)
        pen = jnp.where(score == -jnp.inf, 0.0, NEG)
        for hh in range(HPG):
            pen_ref[0:rows_s, hh * QT:(hh + 1) * QT] = pen
        if rows_s < n_sel:
            pen_ref[rows_s:, :] = jnp.full((n_sel - rows_s, QL), NEG, F32)
        window()

    sizes = [r for r in (nc // 8, nc // 4, nc // 2, nc) if r >= CMP_NEAR + 2 * SUBLANES]
    tiles_per = QT // CMP_STRIDE
    for idx, rows_c in enumerate(sizes):
        lo = 0 if idx == 0 else sizes[idx - 1] // tiles_per
        hi = rows_c // tiles_per
        pl.when((a >= lo) & (a < hi))(functools.partial(compressed_and_select, rows_c))
    o_c = ocw_ref[0]
    o_w = ocw_ref[1]

    big = SEL_SUB * QT
    span = SEL_STREAMS * big
    sub_per_step = SEL_STREAMS * SEL_SUB

    n_span = ks_ref.shape[2] // span

    s_refs = (s_even_ref, s_odd_ref)

    def scores(step, buf, tables):
        sc = jnp.minimum(step, n_span - 1)
        pen_rows = pen_ref[pl.ds(pl.multiple_of(sc * SEL_PEN_ROWS, SEL_PEN_ROWS), SEL_PEN_ROWS), :]
        qt_ref[buf, K_PEN:K_PEN + SEL_PEN_ROWS, :] = pen_rows.astype(BF16)
        k0 = pl.multiple_of(sc * span, span)
        s_all = jnp.dot(ks_ref[0, 0, pl.ds(k0, span), :], qt_ref[buf], preferred_element_type=F32)
        for t in range(SEL_STREAMS):
            s = s_all[t * big:(t + 1) * big]
            if tables:
                parts = []
                for u in range(SEL_SUB):
                    d = a - (step * sub_per_step + t * SEL_SUB + u)
                    slot = jnp.where(d >= 0, jnp.minimum(d, SLOT_FAR), SLOT_NONE)
                    parts.append(s[u * QT:(u + 1) * QT] + tab_ref[0, slot])
                s = jnp.concatenate(parts, axis=0)
            s_refs[buf][t * big:(t + 1) * big, :] = s
            smax_ref[buf, t] = jnp.max(s, axis=0, keepdims=True)

    def update(step, buf, state):
        k0 = pl.multiple_of(jnp.minimum(step, n_span - 1) * span, span)
        out = []
        for t in range(SEL_STREAMS):
            m, acc = state[t]
            m_new = jnp.maximum(m, smax_ref[buf, t])
            p = jnp.exp2(s_refs[buf][t * big:(t + 1) * big, :] - m_new).astype(BF16)
            acc = jnp.exp2(m - m_new) * acc + jnp.dot(vst_ref[0, 0, :, pl.ds(k0 + t * big, big)], p,
                                                      preferred_element_type=F32)
            out.append((m_new, acc))
        return tuple(out)

    def two_steps(jj, state, near):
        scores(2 * jj + 1, 1, near)
        state = update(2 * jj, 0, state)
        scores(2 * jj + 2, 0, True)
        return update(2 * jj + 1, 1, state)

    n_steps = a // sub_per_step + 1
    n_far2 = jnp.maximum(a - (NEAR_SLOTS - 1), 0) // (2 * sub_per_step)
    n_pairs = n_steps // 2
    scores(0, 0, True)
    state = tuple((jnp.full((1, QL), NEG, F32), jnp.zeros((V_ROWS, QL), F32)) for _ in range(SEL_STREAMS))
    state = lax.fori_loop(0, n_far2, functools.partial(two_steps, near=False), state)
    state = lax.fori_loop(n_far2, n_pairs, functools.partial(two_steps, near=True), state)
    state = lax.cond(n_steps % 2 == 1, lambda st: update(n_steps - 1, 0, st), lambda st: st, state)
    m_s = functools.reduce(jnp.maximum, [m for m, _ in state])
    acc_s = functools.reduce(lambda x, y: x + y, [jnp.exp2(m - m_s) * acc for m, acc in state])
    o_s = acc_s[0:DH] * (1.0 / acc_s[DH:DH + 1])

    def gate(branch):
        return jnp.concatenate([gt_ref[0, 0, 3 * hh + branch:3 * hh + branch + 1, :] for hh in range(HPG)],
                               axis=1)

    o_ref[0, 0, 0] = (gate(0) * o_c + gate(1) * o_s + gate(2) * o_w).astype(BF16)


def _nsa_attn_wrapper(*refs):
    nw = WINDOW // QT + 1
    q_ref, ks_ref, vst_ref = refs[0:3]
    kw_refs = refs[3:3 + nw]
    vwt_refs = refs[3 + nw:3 + 2 * nw]
    rest = refs[3 + 2 * nw:]
    _nsa_attn_kernel(q_ref, ks_ref, vst_ref, kw_refs, vwt_refs, *rest)


def _out_proj_kernel(o_ref, x_ref, mod_ref, w_ref, pg_ref, out_ref):
    n_tiles = o_ref.shape[2]
    o_t = jnp.concatenate(
        [jnp.concatenate([o_ref[0, g, i, :, hh * QT:(hh + 1) * QT] for i in range(n_tiles)], axis=1)
         for g in range(N_GROUPS) for hh in range(HPG)], axis=0)
    y = jnp.dot(o_t.astype(F32).T.astype(BF16), w_ref[...], preferred_element_type=F32)
    out_ref[0] = _post_residual(x_ref[0], y, pg_ref[0], mod_ref[0, 0])


def _nsa_layer(x, mod, layer, j, p):
    B, S, D = x.shape
    G, DH, KV = N_GROUPS, HEAD_DIM, N_GROUPS * HEAD_DIM
    T = TOK_TILE
    NC = S // CMP_STRIDE
    n_sel = S // SEL_BLOCK

    w_in = p["nsa_w_in"][j]
    cuts = [D + i * KV for i in range(7)]
    wq, wkc, wvc, wks, wvs, wkw, wvw, wg = jnp.split(w_in, cuts, axis=-1)
    wg = jnp.zeros((D, LANES), F32).at[:, :3 * N_HEADS].set(wg)

    def slabs(w):
        return jnp.zeros((D, G, LANES), F32).at[:, :, :DH].set(w.reshape(D, G, DH)).reshape(D, G * LANES)

    w_all = jnp.concatenate([wq * (HEAD_DIM ** -0.5 * LOG2E), slabs(wks), slabs(wkw), wvs, wvw, wkc, wvc, wg],
                            axis=1).astype(BF16)

    qo_spec = pl.BlockSpec((1, G, T // QT, DH, QL), lambda b, i: (b, 0, i, 0, 0))
    k_spec = pl.BlockSpec((1, G, T, LANES), lambda b, i: (b, 0, i, 0))
    kc_spec = pl.BlockSpec((1, G, T, DH), lambda b, i: (b, 0, i, 0))
    vt_spec = pl.BlockSpec((1, G, V_ROWS, T), lambda b, i: (b, 0, 0, i))
    q, ks, vst, kw, vwt, kc_raw, vc_raw, gt = pl.pallas_call(
        _nsa_in_kernel,
        grid=(B, S // T),
        in_specs=[pl.BlockSpec((1, T, D), lambda b, i: (b, i, 0)), _mod_spec(layer), _vec_spec(layer),
                  pl.BlockSpec((D, NSA_W), lambda b, i: (0, 0))],
        out_specs=[qo_spec, k_spec, vt_spec, k_spec, vt_spec,
                   kc_spec, kc_spec, pl.BlockSpec((1, 3 * N_HEADS, T), lambda b, i: (b, 0, i))],
        out_shape=[jax.ShapeDtypeStruct((B, G, S // QT, DH, QL), BF16),
                   jax.ShapeDtypeStruct((B, G, S, LANES), BF16), jax.ShapeDtypeStruct((B, G, V_ROWS, S), BF16),
                   jax.ShapeDtypeStruct((B, G, S, LANES), BF16), jax.ShapeDtypeStruct((B, G, V_ROWS, S), BF16),
                   jax.ShapeDtypeStruct((B, G, S, DH), F32), jax.ShapeDtypeStruct((B, G, S, DH), F32),
                   jax.ShapeDtypeStruct((B, 3 * N_HEADS, S), F32)],
        compiler_params=_cparams(("arbitrary", "arbitrary")),
        name="nsa_in_proj",
    )(x, mod, p["mix_pre_g"], w_all)

    half = CMP_LEN // 2
    chunk_w = half * DH

    def split_w1(w1):
        return w1.reshape(2, chunk_w, CMP_HIDDEN).astype(BF16)

    def split_pe(pe):
        return pe.reshape(2, chunk_w)

    c_spec = pl.BlockSpec((1, 1, NC, chunk_w), lambda b, g: (b, g, 0, 0))
    full = lambda shape: pl.BlockSpec(shape, lambda b, g: (0,) * len(shape))
    kc, vct = pl.pallas_call(
        _compress_kernel,
        grid=(B, G),
        in_specs=[c_spec, c_spec,
                  full((2, chunk_w)), full((2, chunk_w, CMP_HIDDEN)), full((CMP_HIDDEN, LANES)),
                  full((2, chunk_w)), full((2, chunk_w, CMP_HIDDEN)), full((CMP_HIDDEN, DH))],
        out_specs=[pl.BlockSpec((1, 1, NC, LANES), lambda b, g: (b, g, 0, 0)),
                   pl.BlockSpec((1, 1, V_ROWS, NC), lambda b, g: (b, g, 0, 0))],
        out_shape=[jax.ShapeDtypeStruct((B, G, NC, LANES), BF16),
                   jax.ShapeDtypeStruct((B, G, V_ROWS, NC), BF16)],
        compiler_params=_cparams(("arbitrary", "arbitrary")),
        name="nsa_compress",
    )(kc_raw.reshape(B, G, NC, chunk_w), vc_raw.reshape(B, G, NC, chunk_w),
      split_pe(p["nsa_k_pe"][j]), split_w1(p["nsa_k_w1"][j]),
      jnp.zeros((CMP_HIDDEN, LANES), BF16).at[:, :DH].set(p["nsa_k_w2"][j].astype(BF16)),
      split_pe(p["nsa_v_pe"][j]), split_w1(p["nsa_v_w1"][j]), p["nsa_v_w2"][j].astype(BF16))

    ctab_rows = 2 * NC + LANES
    tab, ctab, far = pl.pallas_call(
        _bias_table_kernel,
        grid=(G,),
        in_specs=[pl.BlockSpec(memory_space=pltpu.SMEM)],
        out_specs=[pl.BlockSpec((1, N_SLOTS, QT, QL), lambda g: (g, 0, 0, 0)),
                   pl.BlockSpec((1, ctab_rows, QL), lambda g: (g, 0, 0)),
                   pl.BlockSpec((1, 1, QL), lambda g: (g, 0, 0))],
        out_shape=[jax.ShapeDtypeStruct((G, N_SLOTS, QT, QL), F32),
                   jax.ShapeDtypeStruct((G, ctab_rows, QL), F32),
                   jax.ShapeDtypeStruct((G, 1, QL), F32)],
        compiler_params=_cparams(("arbitrary",)),
        name="nsa_bias_tables",
    )(p["rel_table"])

    nw = WINDOW // QT + 1
    kw_specs = [pl.BlockSpec((1, 1, QT, LANES), functools.partial(
        lambda b, g, a, d: (b, g, jnp.maximum(a - d, 0), 0), d=d)) for d in range(nw)]
    vwt_specs = [pl.BlockSpec((1, 1, V_ROWS, QT), functools.partial(
        lambda b, g, a, d: (b, g, 0, jnp.maximum(a - d, 0)), d=d)) for d in range(nw)]
    o = pl.pallas_call(
        _nsa_attn_wrapper,
        grid=(B, G, S // QT),
        in_specs=[pl.BlockSpec((1, 1, 1, DH, QL), lambda b, g, a: (b, g, a, 0, 0)),
                  pl.BlockSpec((1, 1, S, LANES), lambda b, g, a: (b, g, 0, 0)),
                  pl.BlockSpec((1, 1, V_ROWS, S), lambda b, g, a: (b, g, 0, 0)),
                  *kw_specs, *vwt_specs,
                  pl.BlockSpec((1, 1, NC, LANES), lambda b, g, a: (b, g, 0, 0)),
                  pl.BlockSpec((1, 1, V_ROWS, NC), lambda b, g, a: (b, g, 0, 0)),
                  pl.BlockSpec((1, 1, 3 * HPG, QT), lambda b, g, a: (b, g, 0, a)),
                  pl.BlockSpec((1, N_SLOTS, QT, QL), lambda b, g, a: (g, 0, 0, 0)),
                  pl.BlockSpec((1, ctab_rows, QL), lambda b, g, a: (g, 0, 0)),
                  pl.BlockSpec((1, 1, QL), lambda b, g, a: (g, 0, 0))],
        out_specs=pl.BlockSpec((1, 1, 1, DH, QL), lambda b, g, a: (b, g, a, 0, 0)),
        out_shape=jax.ShapeDtypeStruct((B, G, S // QT, DH, QL), BF16),
        scratch_shapes=[pltpu.VMEM((2, LANES, QL), BF16),
                        pltpu.VMEM((SEL_STREAMS * SEL_SUB * QT, QL), F32),
                        pltpu.VMEM((SEL_STREAMS * SEL_SUB * QT, QL), F32),
                        pltpu.VMEM((NC, QT), F32), pltpu.VMEM((n_sel, QL), F32),
                        pltpu.VMEM((2, DH, QL), F32), pltpu.VMEM((2, SEL_STREAMS, 1, QL), F32)],
        compiler_params=_cparams(("arbitrary", "arbitrary", "arbitrary")),
        name="nsa_attention",
    )(q, ks, vst, *([kw] * nw), *([vwt] * nw), kc, vct, gt.reshape(B, G, 3 * HPG, S), tab, ctab, far)

    return pl.pallas_call(
        _out_proj_kernel,
        grid=(B, S // T),
        in_specs=[qo_spec,
                  pl.BlockSpec((1, T, D), lambda b, i: (b, i, 0)), _mod_spec(layer),
                  pl.BlockSpec((D, D), lambda b, i: (0, 0)), _vec_spec(layer)],
        out_specs=pl.BlockSpec((1, T, D), lambda b, i: (b, i, 0)),
        out_shape=jax.ShapeDtypeStruct((B, S, D), F32),
        compiler_params=_cparams(("arbitrary", "arbitrary")),
        name="nsa_out_proj",
    )(o, x, mod, p["nsa_w_out"][j].astype(BF16), p["mix_post_g"])


def _as_rows(v):
    return v.reshape(v.shape[0], 1, v.shape[-1])


def kernel(x, c, mix_pre_g, mix_post_g, mix_ada_w, mix_ada_b, ffn_pre_g, ffn_post_g, ffn_ada_w, ffn_ada_b, conf_w_pw1, conf_b_pw1, conf_w_dw, conf_b_dw, conf_ln_g, conf_ln_b, conf_w_pw2, conf_b_pw2, sc_w_in, sc_w_conv, sc_w_out, nsa_w_in, nsa_k_pe, nsa_k_w1, nsa_k_w2, nsa_v_pe, nsa_v_w1, nsa_v_w2, nsa_w_out, rel_table, dense_w_gate, dense_w_up, dense_w_down, moe_w_router, moe_b_router, moe_w_gate, moe_w_up, moe_w_down):
    p = dict(
        mix_pre_g=_as_rows(mix_pre_g), mix_post_g=_as_rows(mix_post_g),
        ffn_pre_g=_as_rows(ffn_pre_g), ffn_post_g=_as_rows(ffn_post_g),
        conf_w_pw1=conf_w_pw1, conf_b_pw1=_as_rows(conf_b_pw1), conf_w_dw=conf_w_dw,
        conf_b_dw=_as_rows(conf_b_dw), conf_ln_g=_as_rows(conf_ln_g), conf_ln_b=_as_rows(conf_ln_b),
        conf_w_pw2=conf_w_pw2, conf_b_pw2=_as_rows(conf_b_pw2),
        sc_w_in=sc_w_in, sc_w_conv=sc_w_conv, sc_w_out=sc_w_out,
        nsa_w_in=nsa_w_in, nsa_k_pe=nsa_k_pe, nsa_k_w1=nsa_k_w1, nsa_k_w2=nsa_k_w2,
        nsa_v_pe=nsa_v_pe, nsa_v_w1=nsa_v_w1, nsa_v_w2=nsa_v_w2, nsa_w_out=nsa_w_out,
        rel_table=rel_table,
        dense_w_gate=dense_w_gate, dense_w_up=dense_w_up, dense_w_down=dense_w_down,
        moe_w_router=moe_w_router, moe_b_router=moe_b_router,
        moe_w_gate=moe_w_gate, moe_w_up=moe_w_up, moe_w_down=moe_w_down,
    )
    mix_mod, ffn_mod = _adaln(c, mix_ada_w, mix_ada_b, ffn_ada_w, ffn_ada_b)
    mixers = (_conformer_layer, _short_conv_layer, _nsa_layer)
    for i in range(DEPTH):
        x = mixers[i % 3](x, mix_mod, i, i // 3, p)
        ffn = _dense_ffn_layer if i % 2 == 0 else _moe_ffn_layer
        x = ffn(x, ffn_mod, i, i // 2, p)
    return x
```

```python
import functools
import math

import numpy as np
import jax
import jax.numpy as jnp
from jax import lax
from jax.experimental import pallas as pl
from jax.experimental.pallas import tpu as pltpu

F32 = jnp.float32
BF16 = jnp.bfloat16
I32 = jnp.int32

D_MODEL = 1024
DEPTH = 4
RMS_EPS = 1e-6
LN_EPS = 1e-5
NEG = -1e30

CONF_KERNEL = 31
SHORT_KERNEL = 3
N_HEADS = 16
HEAD_DIM = 64
N_GROUPS = 4
HPG = 4
CMP_LEN = 32
CMP_STRIDE = 16
CMP_HIDDEN = 128
SEL_BLOCK = 64
SEL_TOPK = 16
N_LOCAL_BLOCKS = 2
WINDOW = 512
REL_BUCKETS = 32
REL_MAX_DIST = 2048
D_FF_DENSE = 2816
N_EXPERTS = 8
D_FF_EXPERT = 3584

LANES = 128
SUBLANES = 8
VMEM_LIMIT = 56 * 1024 * 1024

ADA_TN = 1024
TOK_TILE = 512
CONV_TILE = 256
CONV_HALO = 32
FFN_TILE = 1024
FFN_TF = 256
MOE_TF = 512
MOE_ROWS = 1024
ROUTE_TILE = 512
ROW_DMA_TILE = 256
ROW_DMA_UNROLL = 8
DEST_TILE = 2048
QT = 128
QL = HPG * QT
NEAR_SLOTS = 13
SLOT_FAR = 13
SLOT_WIN = 14
SLOT_NONE = 15
N_SLOTS = 16
CMP_NEAR = 112
SEL_SUB = 4
SEL_STREAMS = 2
SEL_PEN_ROWS = SEL_STREAMS * SEL_SUB * QT // SEL_BLOCK
K_ONE = HEAD_DIM
K_PEN = 80
V_ROWS = 80
LOG2E = 1.4426950408889634


def _bucket_thresholds():
    n = np.arange(0, 4 * REL_MAX_DIST)
    max_exact = REL_BUCKETS // 2
    nf = np.maximum(n, 1).astype(np.float32)
    large = max_exact + (np.log(nf / np.float32(max_exact))
                         / np.float32(math.log(REL_MAX_DIST / max_exact))
                         * np.float32(REL_BUCKETS - max_exact)).astype(np.int32)
    large = np.minimum(large, REL_BUCKETS - 1)
    b = np.where(n < max_exact, n, large)
    return [int(np.argmax(b >= k)) for k in range(REL_BUCKETS)]


BUCKET_THR = _bucket_thresholds()
assert BUCKET_THR[REL_BUCKETS - 1] <= QT * NEAR_SLOTS - (QT - 1)
assert BUCKET_THR[REL_BUCKETS - 1] <= CMP_STRIDE * (CMP_NEAR - 8 + 1) - CMP_LEN + 1


def _cparams(sem):
    return pltpu.CompilerParams(dimension_semantics=sem, vmem_limit_bytes=VMEM_LIMIT)


def _bdot(a, b):
    return jnp.dot(a.astype(BF16), b.astype(BF16), preferred_element_type=F32)


def _sigmoid(x):
    return 1.0 / (1.0 + jnp.exp(-x))


def _rms(x, g):
    return x * lax.rsqrt(jnp.mean(x * x, axis=-1, keepdims=True) + RMS_EPS) * g


def _norm_mod(x, g, mod):
    return _rms(x, g) * (1.0 + mod[1:2]) + mod[0:1]


def _adaln_kernel(c_ref, wm_ref, bm_ref, wf_ref, bf_ref, om_ref, of_ref):
    c = c_ref[...]
    s = c * _sigmoid(c)
    om_ref[0] = jnp.dot(s, wm_ref[0], precision=lax.Precision.HIGHEST,
                        preferred_element_type=F32) + bm_ref[0]
    of_ref[0] = jnp.dot(s, wf_ref[0], precision=lax.Precision.HIGHEST,
                        preferred_element_type=F32) + bf_ref[0]


def _adaln(c, mix_w, mix_b, ffn_w, ffn_b):
    B, D = c.shape
    depth = mix_w.shape[0]
    cp = jnp.zeros((SUBLANES, D), F32).at[:B].set(c)
    w_spec = pl.BlockSpec((1, D, ADA_TN), lambda l, j: (l, 0, j))
    b_spec = pl.BlockSpec((1, 1, ADA_TN), lambda l, j: (l, 0, j))
    o_spec = pl.BlockSpec((1, SUBLANES, ADA_TN), lambda l, j: (l, 0, j))
    om, of = pl.pallas_call(
        _adaln_kernel,
        grid=(depth, 3 * D // ADA_TN),
        in_specs=[pl.BlockSpec((SUBLANES, D), lambda l, j: (0, 0)), w_spec, b_spec, w_spec, b_spec],
        out_specs=[o_spec, o_spec],
        out_shape=[jax.ShapeDtypeStruct((depth, SUBLANES, 3 * D), F32)] * 2,
        compiler_params=_cparams(("arbitrary", "arbitrary")),
        name="adaln",
    )(cp, mix_w, mix_b.reshape(depth, 1, 3 * D), ffn_w, ffn_b.reshape(depth, 1, 3 * D))
    return (om[:, :B].reshape(depth, B, 3, D), of[:, :B].reshape(depth, B, 3, D))


def _mod_spec(layer):
    return pl.BlockSpec((1, 1, 3, D_MODEL), lambda b, *_: (layer, b, 0, 0))


def _vec_spec(layer, width=D_MODEL):
    return pl.BlockSpec((1, 1, width), lambda *_: (layer, 0, 0))


def _post_residual(x, y, post_g, mod):
    return x + mod[2:3] * _rms(y, post_g)


def _conf_a_kernel(x_ref, mod_ref, g_ref, w_ref, b_ref, u_ref):
    D = D_MODEL
    h = _norm_mod(x_ref[0], g_ref[0], mod_ref[0, 0])
    z = _bdot(h, w_ref[...]) + b_ref[0]
    u_ref[0] = z[:, :D] * _sigmoid(z[:, D:])


def _conf_b_kernel(u_ref, up_ref, x_ref, mod_ref, wdw_ref, bdw_ref, lng_ref, lnb_ref,
                   w2_ref, b2_ref, pg_ref, o_ref, ext_ref, conv_ref):
    T, D, K = CONV_TILE, D_MODEL, CONF_KERNEL
    i = pl.program_id(1)
    keep = (i > 0).astype(F32)
    ext_ref[0:CONV_HALO, :] = up_ref[0] * keep
    ext_ref[CONV_HALO:, :] = u_ref[0]
    rb, cb = 64, 256
    off = CONV_HALO - (K - 1)

    def col_body(c, carry):
        c0 = pl.multiple_of(c * cb, cb)
        bias = bdw_ref[0, :, pl.ds(c0, cb)]
        for r in range(T // rb):
            acc = jnp.zeros((rb // SUBLANES, SUBLANES, cb), F32) + bias[None]
            for res in range(SUBLANES):
                offs = [o for o in range(off, off + K) if o % SUBLANES == res]
                lo, hi = offs[0], offs[-1]
                win = ext_ref[pl.ds(r * rb + lo, hi - lo + rb), pl.ds(c0, cb)]
                win = win.reshape((hi - lo + rb) // SUBLANES, SUBLANES, cb)
                for o in offs:
                    i0 = (o - lo) // SUBLANES
                    w = wdw_ref[0, o - off, :, pl.ds(c0, cb)]
                    acc = acc + w[None] * win[i0:i0 + rb // SUBLANES]
            conv_ref[pl.ds(r * rb, rb), pl.ds(c0, cb)] = acc.reshape(rb, cb)
        return carry

    lax.fori_loop(0, D // cb, col_body, 0)
    v = conv_ref[...]
    mu = jnp.mean(v, axis=-1, keepdims=True)
    var = jnp.mean(jnp.square(v - mu), axis=-1, keepdims=True)
    ln = (v - mu) * lax.rsqrt(var + LN_EPS) * lng_ref[0] + lnb_ref[0]
    act = ln * _sigmoid(ln)
    y = _bdot(act, w2_ref[...]) + b2_ref[0]
    o_ref[0] = _post_residual(x_ref[0], y, pg_ref[0], mod_ref[0, 0])


def _conformer_layer(x, mod, layer, j, p):
    B, S, D = x.shape
    T = TOK_TILE
    u = pl.pallas_call(
        _conf_a_kernel,
        grid=(B, S // T),
        in_specs=[pl.BlockSpec((1, T, D), lambda b, i: (b, i, 0)), _mod_spec(layer),
                  _vec_spec(layer), pl.BlockSpec((D, 2 * D), lambda b, i: (0, 0)),
                  _vec_spec(j, 2 * D)],
        out_specs=pl.BlockSpec((1, T, D), lambda b, i: (b, i, 0)),
        out_shape=jax.ShapeDtypeStruct((B, S, D), F32),
        compiler_params=_cparams(("arbitrary", "arbitrary")),
        name="conf_pw1_glu",
    )(x, mod, p["mix_pre_g"], p["conf_w_pw1"][j].astype(BF16), p["conf_b_pw1"])
    T = CONV_TILE
    hb = T // CONV_HALO
    return pl.pallas_call(
        _conf_b_kernel,
        grid=(B, S // T),
        in_specs=[pl.BlockSpec((1, T, D), lambda b, i: (b, i, 0)),
                  pl.BlockSpec((1, CONV_HALO, D), lambda b, i: (b, jnp.maximum(i * hb - 1, 0), 0)),
                  pl.BlockSpec((1, T, D), lambda b, i: (b, i, 0)), _mod_spec(layer),
                  pl.BlockSpec((1, CONF_KERNEL, SUBLANES, D), lambda b, i: (j, 0, 0, 0)),
                  _vec_spec(j), _vec_spec(j), _vec_spec(j),
                  pl.BlockSpec((D, D), lambda b, i: (0, 0)), _vec_spec(j), _vec_spec(layer)],
        out_specs=pl.BlockSpec((1, T, D), lambda b, i: (b, i, 0)),
        out_shape=jax.ShapeDtypeStruct((B, S, D), F32),
        scratch_shapes=[pltpu.VMEM((T + CONV_HALO, D), F32), pltpu.VMEM((T, D), F32)],
        compiler_params=_cparams(("arbitrary", "arbitrary")),
        name="conf_conv_pw2",
    )(u, u, x, mod, jnp.broadcast_to(p["conf_w_dw"][:, :, None, :], (*p["conf_w_dw"].shape[:2], SUBLANES, D)),
      p["conf_b_dw"], p["conf_ln_g"], p["conf_ln_b"],
      p["conf_w_pw2"][j].astype(BF16), p["conf_b_pw2"], p["mix_post_g"])


def _short_conv_kernel(x_ref, mod_ref, g_ref, win_ref, wc_ref, wout_ref, pg_ref, o_ref, ext_ref):
    T, D = TOK_TILE, D_MODEL
    i = pl.program_id(1)

    @pl.when(i == 0)
    def _():
        ext_ref[0:SUBLANES, :] = jnp.zeros((SUBLANES, D), F32)

    x = x_ref[0]
    mod = mod_ref[0, 0]
    h = _norm_mod(x, g_ref[0], mod)
    z = _bdot(h, win_ref[...])
    v, gate_b, gate_c = z[:, :D], z[:, D:2 * D], z[:, 2 * D:]
    ext_ref[SUBLANES:, :] = gate_c * v
    wc = wc_ref[0]
    u = (wc[0:1] * ext_ref[pl.ds(SUBLANES - 2, T), :] + wc[1:2] * ext_ref[pl.ds(SUBLANES - 1, T), :]
         + wc[2:3] * ext_ref[pl.ds(SUBLANES, T), :])
    ext_ref[0:SUBLANES, :] = ext_ref[pl.ds(T, SUBLANES), :]
    y = _bdot(gate_b * u, wout_ref[...])
    o_ref[0] = _post_residual(x, y, pg_ref[0], mod)


def _short_conv_layer(x, mod, layer, j, p):
    B, S, D = x.shape
    T = TOK_TILE
    return pl.pallas_call(
        _short_conv_kernel,
        grid=(B, S // T),
        in_specs=[pl.BlockSpec((1, T, D), lambda b, i: (b, i, 0)), _mod_spec(layer), _vec_spec(layer),
                  pl.BlockSpec((D, 3 * D), lambda b, i: (0, 0)),
                  pl.BlockSpec((1, SHORT_KERNEL, D), lambda b, i: (j, 0, 0)),
                  pl.BlockSpec((D, D), lambda b, i: (0, 0)), _vec_spec(layer)],
        out_specs=pl.BlockSpec((1, T, D), lambda b, i: (b, i, 0)),
        out_shape=jax.ShapeDtypeStruct((B, S, D), F32),
        scratch_shapes=[pltpu.VMEM((T + SUBLANES, D), F32)],
        compiler_params=_cparams(("arbitrary", "arbitrary")),
        name="short_conv",
    )(x, mod, p["mix_pre_g"], p["sc_w_in"][j].astype(BF16), p["sc_w_conv"],
      p["sc_w_out"][j].astype(BF16), p["mix_post_g"])


def _swiglu_step(h, wg_ref, wu_ref, wd_ref, acc_ref):
    a = _bdot(h, wg_ref[...])
    u = _bdot(h, wu_ref[...])
    acc_ref[...] += _bdot((a * _sigmoid(a)) * u, wd_ref[...])


def _dense_ffn_kernel(x_ref, mod_ref, g_ref, wg_ref, wu_ref, wd_ref, pg_ref, o_ref, h_ref, acc_ref):
    f = pl.program_id(2)

    @pl.when(f == 0)
    def _():
        h_ref[...] = _norm_mod(x_ref[0], g_ref[0], mod_ref[0, 0]).astype(BF16)
        acc_ref[...] = jnp.zeros_like(acc_ref)

    _swiglu_step(h_ref[...], wg_ref.at[0], wu_ref.at[0], wd_ref.at[0], acc_ref)

    @pl.when(f == pl.num_programs(2) - 1)
    def _():
        o_ref[0] = _post_residual(x_ref[0], acc_ref[...], pg_ref[0], mod_ref[0, 0])


def _dense_ffn_layer(x, mod, layer, j, p):
    B, S, D = x.shape
    T, TF, F = min(FFN_TILE, S), FFN_TF, D_FF_DENSE
    return pl.pallas_call(
        _dense_ffn_kernel,
        grid=(B, S // T, F // TF),
        in_specs=[pl.BlockSpec((1, T, D), lambda b, i, f: (b, i, 0)), _mod_spec(layer), _vec_spec(layer),
                  pl.BlockSpec((1, D, TF), lambda b, i, f: (j, 0, f)),
                  pl.BlockSpec((1, D, TF), lambda b, i, f: (j, 0, f)),
                  pl.BlockSpec((1, TF, D), lambda b, i, f: (j, f, 0)), _vec_spec(layer)],
        out_specs=pl.BlockSpec((1, T, D), lambda b, i, f: (b, i, 0)),
        out_shape=jax.ShapeDtypeStruct((B, S, D), F32),
        scratch_shapes=[pltpu.VMEM((T, D), BF16), pltpu.VMEM((T, D), F32)],
        compiler_params=_cparams(("arbitrary", "arbitrary", "arbitrary")),
        name="dense_swiglu",
    )(x, mod, p["ffn_pre_g"], p["dense_w_gate"], p["dense_w_up"], p["dense_w_down"], p["ffn_post_g"])


def _router_kernel(x_ref, mod_ref, g_ref, wr_ref, br_ref, h_ref, ri_ref, rw_ref, cnt_ref, base_ref):
    T = ROUTE_TILE
    i = pl.program_id(0)

    @pl.when(i == 0)
    def _():
        base_ref[...] = jnp.zeros_like(base_ref)

    h = _norm_mod(x_ref[...], g_ref[0], mod_ref[0, 0])
    h_ref[...] = h
    logits = jnp.dot(h, wr_ref[...], precision=lax.Precision.HIGHEST,
                     preferred_element_type=F32) + br_ref[0]
    lane = lax.broadcasted_iota(I32, (T, LANES), 1).astype(F32)
    logits = jnp.where(lane < N_EXPERTS, logits, -jnp.inf)
    l0 = jnp.max(logits, axis=1, keepdims=True)
    e0 = jnp.min(jnp.where(logits == l0, lane, float(LANES)), axis=1, keepdims=True)
    rest = jnp.where(lane == e0, -jnp.inf, logits)
    l1 = jnp.max(rest, axis=1, keepdims=True)
    e1 = jnp.min(jnp.where(rest == l1, lane, float(LANES)), axis=1, keepdims=True)
    ex = jnp.exp(l1 - l0)
    w0 = 1.0 / (1.0 + ex)
    w1 = ex / (1.0 + ex)
    onehot = ((lane == e0) | (lane == e1)).astype(F32)
    row = lax.broadcasted_iota(I32, (T, T), 0)
    col = lax.broadcasted_iota(I32, (T, T), 1)
    tri = jnp.where(row > col, 1.0, 0.0).astype(BF16)
    before = jnp.dot(tri, onehot.astype(BF16), preferred_element_type=F32) + base_ref[...]
    r0 = jnp.sum(jnp.where(lane == e0, before, 0.0), axis=1, keepdims=True)
    r1 = jnp.sum(jnp.where(lane == e1, before, 0.0), axis=1, keepdims=True)
    base_ref[...] += jnp.sum(onehot, axis=0, keepdims=True)
    cnt_ref[...] = base_ref[...].astype(I32)
    ri = jnp.where(lane == 0, e0, jnp.where(lane == 1, e1, jnp.where(lane == 2, r0, r1)))
    ri_ref[...] = ri[:, :SUBLANES].astype(I32)
    rw_ref[...] = jnp.where(lane == 0, w0, w1)[:, :SUBLANES]


def _dest_kernel(pstart_ref, ri_ref, dst_ref):
    ri = ri_ref[...]
    lane = lax.broadcasted_iota(I32, ri.shape, 1)
    expert = jnp.where(lane == 0, ri[:, 0:1], ri[:, 1:2])
    rank = jnp.where(lane == 0, ri[:, 2:3], ri[:, 3:4])
    start = jnp.zeros(ri.shape, I32)
    for e in range(N_EXPERTS):
        start = jnp.where(expert == e, pstart_ref[e], start)
    dst_ref[...] = start + rank


def _dispatch_kernel(dst_ref, h_ref, zero_ref, buf_ref, stage_ref, sem):
    del zero_ref
    T = ROW_DMA_TILE
    i = pl.program_id(0)
    slot = i % 2
    stage_ref[slot] = h_ref[...]

    def row_copy(t, dest, s):
        return pltpu.make_async_copy(stage_ref.at[s, pl.ds(t, 1)], buf_ref.at[pl.ds(dest, 1)], sem.at[s])

    def issue(t, carry):
        for k in range(2):
            row_copy(t, dst_ref[t, k], slot).start()
        return carry

    lax.fori_loop(0, T, issue, 0, unroll=ROW_DMA_UNROLL)

    def drain(s):
        def body(t, carry):
            for k in range(2):
                row_copy(0, 0, s).wait()
            return carry

        lax.fori_loop(0, T, body, 0, unroll=ROW_DMA_UNROLL)

    @pl.when(i > 0)
    def _():
        drain(1 - slot)

    @pl.when(i == pl.num_programs(0) - 1)
    def _():
        drain(slot)


def _expert_kernel(blk_e_ref, nblk_ref, xb_ref, wg_ref, wu_ref, wd_ref, o_ref, h_ref):
    del blk_e_ref
    i = pl.program_id(0)
    f = pl.program_id(1)
    live = i < nblk_ref[0]

    @pl.when(f == 0)
    def _():
        h_ref[...] = xb_ref[...].astype(BF16)
        o_ref[...] = jnp.zeros_like(o_ref)

    @pl.when(live)
    def _():
        _swiglu_step(h_ref[...], wg_ref.at[0, 0], wu_ref.at[0, 0], wd_ref.at[0, 0], o_ref)


def _combine_kernel(x_ref, mod_ref, pg_ref, dst_ref, dst_next_ref, rw_ref, yb_ref, o_ref, y_ref, sem):
    T = ROW_DMA_TILE
    i = pl.program_id(0)
    slot = i % 2

    def row_copy(src, t, k, s):
        return pltpu.make_async_copy(yb_ref.at[pl.ds(src, 1)], y_ref.at[s, k, pl.ds(t, 1)], sem.at[s])

    def gather(idx_ref, s):
        def body(t, carry):
            for k in range(2):
                row_copy(idx_ref[t, k], t, k, s).start()
            return carry

        lax.fori_loop(0, T, body, 0, unroll=ROW_DMA_UNROLL)

    @pl.when(i == 0)
    def _():
        gather(dst_ref, slot)

    @pl.when(i + 1 < pl.num_programs(0))
    def _():
        gather(dst_next_ref, 1 - slot)

    def drain(t, carry):
        for k in range(2):
            row_copy(0, 0, k, slot).wait()
        return carry

    lax.fori_loop(0, T, drain, 0, unroll=ROW_DMA_UNROLL)
    rw = rw_ref[...]
    y = rw[:, 0:1] * y_ref[slot, 0] + rw[:, 1:2] * y_ref[slot, 1]
    o_ref[...] = _post_residual(x_ref[...], y, pg_ref[0], mod_ref[0, 0])


def _moe_ffn_layer(x, mod, layer, j, p):
    B, S, D = x.shape
    N = B * S
    T = ROUTE_TILE
    assert D == D_MODEL and S % T == 0 and S % ROW_DMA_TILE == 0 and N % DEST_TILE == 0
    tiles_per_seq = S // T
    xt = x.reshape(N, D)
    wr = jnp.zeros((D, LANES), F32).at[:, :N_EXPERTS].set(p["moe_w_router"][j])
    br = jnp.zeros((1, 1, LANES), F32).at[0, 0, :N_EXPERTS].set(p["moe_b_router"][j])
    h, ri, rw, cnt = pl.pallas_call(
        _router_kernel,
        grid=(N // T,),
        in_specs=[pl.BlockSpec((T, D), lambda i: (i, 0)),
                  pl.BlockSpec((1, 1, 3, D), lambda i: (layer, i // tiles_per_seq, 0, 0)),
                  _vec_spec(layer), pl.BlockSpec((D, LANES), lambda i: (0, 0)), _vec_spec(0, LANES)],
        out_specs=[pl.BlockSpec((T, D), lambda i: (i, 0)), pl.BlockSpec((T, SUBLANES), lambda i: (i, 0)),
                   pl.BlockSpec((T, SUBLANES), lambda i: (i, 0)), pl.BlockSpec((1, LANES), lambda i: (0, 0))],
        out_shape=[jax.ShapeDtypeStruct((N, D), F32), jax.ShapeDtypeStruct((N, SUBLANES), I32),
                   jax.ShapeDtypeStruct((N, SUBLANES), F32), jax.ShapeDtypeStruct((1, LANES), I32)],
        scratch_shapes=[pltpu.VMEM((1, LANES), F32)],
        compiler_params=_cparams(("arbitrary",)),
        name="moe_router",
    )(xt, mod, p["ffn_pre_g"], wr, br)

    R = MOE_ROWS
    counts = cnt[0, :N_EXPERTS]
    padded = (counts + R - 1) // R * R
    pend = jnp.cumsum(padded)
    pstart = (pend - padded).astype(I32)
    n_blocks = -(-(2 * N + N_EXPERTS * (R - 1)) // R)
    blk_e = jnp.minimum(jnp.sum((jnp.arange(n_blocks) * R)[:, None] >= pend[None, :], axis=1),
                        N_EXPERTS - 1).astype(I32)
    n_live = (pend[-1] // R).astype(I32).reshape(1)
    rows = n_blocks * R

    TR = DEST_TILE
    dst = pl.pallas_call(
        _dest_kernel,
        grid_spec=pltpu.PrefetchScalarGridSpec(
            num_scalar_prefetch=1,
            grid=(N // TR,),
            in_specs=[pl.BlockSpec((TR, SUBLANES), lambda i, ps: (i, 0))],
            out_specs=pl.BlockSpec((TR, SUBLANES), lambda i, ps: (i, 0)),
        ),
        out_shape=jax.ShapeDtypeStruct((N, SUBLANES), I32),
        compiler_params=_cparams(("arbitrary",)),
        name="moe_dest",
    )(pstart, ri)

    TD = ROW_DMA_TILE
    n_dma_tiles = N // TD
    dst_spec = pl.BlockSpec((TD, SUBLANES), lambda i: (i, 0), memory_space=pltpu.SMEM)
    buf = pl.pallas_call(
        _dispatch_kernel,
        grid=(n_dma_tiles,),
        in_specs=[dst_spec, pl.BlockSpec((TD, D), lambda i: (i, 0)), pl.BlockSpec(memory_space=pl.ANY)],
        out_specs=pl.BlockSpec(memory_space=pl.ANY),
        out_shape=jax.ShapeDtypeStruct((rows, D), F32),
        scratch_shapes=[pltpu.VMEM((2, TD, D), F32), pltpu.SemaphoreType.DMA((2,))],
        input_output_aliases={2: 0},
        compiler_params=_cparams(("arbitrary",)),
        name="moe_dispatch",
    )(dst, h, jnp.zeros((rows, D), F32))

    TF, F = MOE_TF, D_FF_EXPERT
    yb = pl.pallas_call(
        _expert_kernel,
        grid_spec=pltpu.PrefetchScalarGridSpec(
            num_scalar_prefetch=2,
            grid=(n_blocks, F // TF),
            in_specs=[pl.BlockSpec((R, D), lambda i, f, be, nb: (i, 0)),
                      pl.BlockSpec((1, 1, D, TF), lambda i, f, be, nb: (j, be[i], 0, jnp.where(i < nb[0], f, 0))),
                      pl.BlockSpec((1, 1, D, TF), lambda i, f, be, nb: (j, be[i], 0, jnp.where(i < nb[0], f, 0))),
                      pl.BlockSpec((1, 1, TF, D), lambda i, f, be, nb: (j, be[i], jnp.where(i < nb[0], f, 0), 0))],
            out_specs=pl.BlockSpec((R, D), lambda i, f, be, nb: (i, 0)),
            scratch_shapes=[pltpu.VMEM((R, D), BF16)],
        ),
        out_shape=jax.ShapeDtypeStruct((rows, D), F32),
        compiler_params=_cparams(("arbitrary", "arbitrary")),
        name="moe_experts",
    )(blk_e, n_live, buf, p["moe_w_gate"], p["moe_w_up"], p["moe_w_down"])

    tiles_per_seq_d = S // TD
    out = pl.pallas_call(
        _combine_kernel,
        grid=(n_dma_tiles,),
        in_specs=[pl.BlockSpec((TD, D), lambda i: (i, 0)),
                  pl.BlockSpec((1, 1, 3, D), lambda i: (layer, i // tiles_per_seq_d, 0, 0)),
                  pl.BlockSpec((1, 1, D), lambda i: (layer, 0, 0)),
                  dst_spec,
                  pl.BlockSpec((TD, SUBLANES), lambda i: (jnp.minimum(i + 1, n_dma_tiles - 1), 0),
                               memory_space=pltpu.SMEM),
                  pl.BlockSpec((TD, SUBLANES), lambda i: (i, 0)),
                  pl.BlockSpec(memory_space=pl.ANY)],
        out_specs=pl.BlockSpec((TD, D), lambda i: (i, 0)),
        out_shape=jax.ShapeDtypeStruct((N, D), F32),
        scratch_shapes=[pltpu.VMEM((2, 2, TD, D), F32), pltpu.SemaphoreType.DMA((2,))],
        compiler_params=_cparams(("arbitrary",)),
        name="moe_combine",
    )(xt, mod, p["ffn_post_g"], dst, dst, rw, yb)
    return out.reshape(B, S, D)


NSA_W = D_MODEL + 2 * N_GROUPS * LANES + 4 * N_GROUPS * HEAD_DIM + LANES


def _ones_rows(n):
    r = lax.broadcasted_iota(I32, (V_ROWS - HEAD_DIM, n), 0)
    return jnp.where(r == 0, 1.0, 0.0)


def _nsa_in_kernel(x_ref, mod_ref, g_ref, w_ref, q_ref, ks_ref, vst_ref, kw_ref, vwt_ref,
                   kc_ref, vc_ref, gt_ref):
    D, G, DH = D_MODEL, N_GROUPS, HEAD_DIM
    KV = G * DH
    T = x_ref.shape[1]
    h = _norm_mod(x_ref[0], g_ref[0], mod_ref[0, 0])
    z = _bdot(h, w_ref[...])
    zq_t = z[:, :D].T
    for g in range(G):
        for i in range(T // QT):
            q_ref[0, g, i] = jnp.concatenate(
                [zq_t[(g * HPG + hh) * DH:(g * HPG + hh + 1) * DH, i * QT:(i + 1) * QT] for hh in range(HPG)],
                axis=1).astype(BF16)
    lane = lax.broadcasted_iota(I32, (T, LANES), 1)
    pos = pl.program_id(1) * T + lax.broadcasted_iota(I32, (T, LANES), 0)
    one_col = jnp.where(lane == K_ONE, 1.0, 0.0)
    blk_col = jnp.where(lane == K_PEN + (pos // SEL_BLOCK) % SEL_PEN_ROWS, 1.0, 0.0)
    o = D
    for g in range(G):
        ks_ref[0, g] = (z[:, o + g * LANES:o + (g + 1) * LANES] + (one_col + blk_col)).astype(BF16)
    o += G * LANES
    for g in range(G):
        kw_ref[0, g] = (z[:, o + g * LANES:o + (g + 1) * LANES] + one_col).astype(BF16)
    o += G * LANES
    v_s, v_w = z[:, o:o + KV], z[:, o + KV:o + 2 * KV]
    k_c, v_c = z[:, o + 2 * KV:o + 3 * KV], z[:, o + 3 * KV:o + 4 * KV]
    gates = _sigmoid(z[:, o + 4 * KV:])
    vst = v_s.T
    vwt = v_w.T
    ones = _ones_rows(T)
    for g in range(G):
        sl = slice(g * DH, (g + 1) * DH)
        kc_ref[0, g] = k_c[:, sl]
        vc_ref[0, g] = v_c[:, sl]
        vst_ref[0, g] = jnp.concatenate([vst[sl, :], ones], axis=0).astype(BF16)
        vwt_ref[0, g] = jnp.concatenate([vwt[sl, :], ones], axis=0).astype(BF16)
    gt_ref[0] = gates.T[:3 * N_HEADS, :]


def _gelu_tanh(x):
    return x * (0.5 * (1.0 + jnp.tanh(math.sqrt(2.0 / math.pi) * (x + 0.044715 * (x * x * x)))))


def _compress_kernel(kc_ref, vc_ref, kpe_ref, kw1_ref, kw2_ref, vpe_ref, vw1_ref, vw2_ref,
                     kco_ref, vcto_ref):
    nc = kc_ref.shape[2]

    def comp(c, pe_ref, w1_ref, w2_ref):
        a = _bdot(c + pe_ref[0:1], w1_ref[0])
        b = _bdot(c + pe_ref[1:2], w1_ref[1])
        b_next = pltpu.roll(b, nc - 1, axis=0)
        out = _bdot(_gelu_tanh(a + b_next), w2_ref[...])
        row = lax.broadcasted_iota(I32, out.shape, 0)
        return jnp.where(row < nc - 1, out, 0.0)

    kc = comp(kc_ref[0, 0], kpe_ref, kw1_ref, kw2_ref)
    lane = lax.broadcasted_iota(I32, kc.shape, 1)
    kco_ref[0, 0] = (kc + jnp.where(lane == K_ONE, 1.0, 0.0)).astype(BF16)
    vct = comp(vc_ref[0, 0], vpe_ref, vw1_ref, vw2_ref).T
    vcto_ref[0, 0] = jnp.concatenate([vct, _ones_rows(nc)], axis=0).astype(BF16)


def _bias_table_kernel(tbl_ref, tab_ref, ctab_ref, far_ref):
    g = pl.program_id(0)

    def bias_of(dist, h, far):
        v = jnp.full(dist.shape, tbl_ref[0, h], F32)
        for b in range(1, REL_BUCKETS):
            v = jnp.where(dist >= BUCKET_THR[b], tbl_ref[b, h], v)
        return jnp.where(dist >= 0, v * LOG2E - far, NEG)

    nc = (ctab_ref.shape[1] - LANES) // 2
    key = lax.broadcasted_iota(I32, (QT, QT), 0)
    qry = lax.broadcasted_iota(I32, (QT, QT), 1)
    crow = lax.broadcasted_iota(I32, (CMP_NEAR, QT), 0)
    cqry = lax.broadcasted_iota(I32, (CMP_NEAR, QT), 1)
    for hh in range(HPG):
        h = g * HPG + hh
        lanes = slice(hh * QT, (hh + 1) * QT)
        far = jnp.full((1, QT), tbl_ref[REL_BUCKETS - 1, h] * LOG2E, F32).astype(BF16).astype(F32)
        far_ref[0, :, lanes] = far

        def near_slot(slot, carry, h=h, lanes=lanes, far=far):
            tab_ref[0, slot, :, lanes] = bias_of(slot * QT + qry - key, h, far)
            return carry

        lax.fori_loop(0, NEAR_SLOTS, near_slot, 0)
        tab_ref[0, SLOT_FAR, :, lanes] = jnp.zeros((QT, QT), F32)
        tab_ref[0, SLOT_NONE, :, lanes] = jnp.full((QT, QT), NEG, F32)
        dw = WINDOW + qry - key
        tab_ref[0, SLOT_WIN, :, lanes] = jnp.where(dw < WINDOW, bias_of(dw, h, far), NEG)
        ctab_ref[0, 0:nc, lanes] = jnp.zeros((nc, QT), F32)
        dc = cqry - CMP_STRIDE * (crow - (CMP_NEAR - SUBLANES)) - (CMP_LEN - 1)
        ctab_ref[0, nc:nc + CMP_NEAR, lanes] = bias_of(dc, h, far)
        ctab_ref[0, nc + CMP_NEAR:, lanes] = jnp.full((nc + LANES - CMP_NEAR, QT), NEG, F32)


def _nsa_attn_kernel(q_ref, ks_ref, vst_ref, kw_refs, vwt_refs, kc_ref, vct_ref, gt_ref,
                     tab_ref, ctab_ref, far_ref, o_ref, qt_ref, s_even_ref, s_odd_ref, psum_ref, pen_ref, ocw_ref,
                     smax_ref):
    DH = HEAD_DIM
    a = pl.program_id(2)
    nc = kc_ref.shape[2]
    n_sel = pen_ref.shape[0]

    aug_row = lax.broadcasted_iota(I32, (LANES - DH, QL), 0)
    aug = jnp.where(aug_row == 0, far_ref[0], 0.0).astype(BF16)
    for i in range(2):
        qt_ref[i, 0:DH, :] = q_ref[0, 0, 0]
        qt_ref[i, DH:, :] = aug
    qt = qt_ref[0]

    def window():
        nw = WINDOW // QT + 1
        s_w = []
        for d in range(nw):
            slot = SLOT_WIN if d == nw - 1 else d
            if d > 0:
                slot = jnp.where(a >= d, slot, SLOT_NONE)
            s_w.append(jnp.dot(kw_refs[d][0, 0], qt, preferred_element_type=F32) + tab_ref[0, slot])
        m_w = functools.reduce(jnp.maximum, [jnp.max(s_d, axis=0, keepdims=True) for s_d in s_w])
        acc_w = functools.reduce(lambda x, y: x + y, [
            jnp.dot(vwt_refs[d][0, 0], jnp.exp2(s_w[d] - m_w).astype(BF16), preferred_element_type=F32)
            for d in range(nw)])
        ocw_ref[1] = acc_w[0:DH] * (1.0 / acc_w[DH:DH + 1])

    n_hi = (QT // CMP_STRIDE) * (a + 1)
    c0 = pl.multiple_of(nc - (n_hi - CMP_NEAR), SUBLANES)
    ratio = SEL_BLOCK // CMP_STRIDE

    def compressed_and_select(rows_c):
        rows_s = rows_c // ratio
        s = (jnp.dot(kc_ref[0, 0, 0:rows_c, :], qt, preferred_element_type=F32)
             + ctab_ref[0, pl.ds(c0, rows_c), :])
        m = jnp.max(s, axis=0, keepdims=True)
        e = jnp.exp2(s - m)
        l = jnp.sum(e, axis=0, keepdims=True)
        pn = e * jnp.where(m > 0.1 * NEG, 1.0 / l, 0.0)
        ocw_ref[0] = jnp.dot(vct_ref[0, 0, :, 0:rows_c], pn.astype(BF16), preferred_element_type=F32)[0:DH]
        psum_ref[0:rows_c, :] = pn[:, 0:QT] + pn[:, QT:2 * QT] + pn[:, 2 * QT:3 * QT] + pn[:, 3 * QT:4 * QT]

        imp = psum_ref[pl.ds(0, rows_s, stride=ratio), :]
        for jj in range(1, ratio):
            imp = imp + psum_ref[pl.ds(jj, rows_s, stride=ratio), :]
        last = psum_ref[pl.ds(ratio - 1, rows_s, stride=ratio), :]
        srow = lax.broadcasted_iota(I32, (rows_s, QT), 0)
        imp = imp + jnp.where(srow > 0, pltpu.roll(last, 1, axis=0), 0.0)
        lane_q = lax.broadcasted_iota(I32, (rows_s, QT), 1)
        cur = (QT // SEL_BLOCK) * a + lane_q // SEL_BLOCK
        back = cur - srow
        valid = back >= 0
        forced = (srow == 0) | (valid & (back < N_LOCAL_BLOCKS))
        n_forced = 1 + N_LOCAL_BLOCKS
        score = jnp.where(forced, -jnp.inf, jnp.where(valid, imp, -1.0))
        srow_f = srow.astype(F32)
        for _ in range(min(SEL_TOPK, n_sel) - n_forced):
            mx = jnp.max(score, axis=0, keepdims=True)
            first = jnp.min(jnp.where(score == mx, srow_f, float(rows_s)), axis=0, keepdims=True)
            score = jnp.where(srow_f == first, -jnp.inf, score)
        pen = jnp.where(score == -jnp.inf, 0.0, NEG)
        for hh in range(HPG):
            pen_ref[0:rows_s, hh * QT:(hh + 1) * QT] = pen
        if rows_s < n_sel:
            pen_ref[rows_s:, :] = jnp.full((n_sel - rows_s, QL), NEG, F32)
        window()

    big = SEL_SUB * QT
    span = SEL_STREAMS * big
    sub_per_step = SEL_STREAMS * SEL_SUB

    n_span = ks_ref.shape[2] // span

    s_refs = (s_even_ref, s_odd_ref)

    def scores(step, buf, tables):
        sc = jnp.minimum(step, n_span - 1)
        pen_rows = pen_ref[pl.ds(pl.multiple_of(sc * SEL_PEN_ROWS, SEL_PEN_ROWS), SEL_PEN_ROWS), :]
        qt_ref[buf, K_PEN:K_PEN + SEL_PEN_ROWS, :] = pen_rows.astype(BF16)
        k0 = pl.multiple_of(sc * span, span)
        s_all = jnp.dot(ks_ref[0, 0, pl.ds(k0, span), :], qt_ref[buf], preferred_element_type=F32)
        for t in range(SEL_STREAMS):
            s = s_all[t * big:(t + 1) * big]
            if tables:
                parts = []
                for u in range(SEL_SUB):
                    d = a - (step * sub_per_step + t * SEL_SUB + u)
                    slot = jnp.where(d >= 0, jnp.minimum(d, SLOT_FAR), SLOT_NONE)
                    parts.append(s[u * QT:(u + 1) * QT] + tab_ref[0, slot])
                s = jnp.concatenate(parts, axis=0)
            s_refs[buf][t * big:(t + 1) * big, :] = s
            smax_ref[buf, t] = jnp.max(s, axis=0, keepdims=True)

    def update(step, buf, state):
        k0 = pl.multiple_of(jnp.minimum(step, n_span - 1) * span, span)
        out = []
        for t in range(SEL_STREAMS):
            m, acc = state[t]
            m_new = jnp.maximum(m, smax_ref[buf, t])
            p = jnp.exp2(s_refs[buf][t * big:(t + 1) * big, :] - m_new).astype(BF16)
            acc = jnp.exp2(m - m_new) * acc + jnp.dot(vst_ref[0, 0, :, pl.ds(k0 + t * big, big)], p,
                                                      preferred_element_type=F32)
            out.append((m_new, acc))
        return tuple(out)

    def two_steps(jj, state, near):
        scores(2 * jj + 1, 1, near)
        state = update(2 * jj, 0, state)
        scores(2 * jj + 2, 0, True)
        return update(2 * jj + 1, 1, state)

    sizes = [r for r in (nc // 8, nc // 4, nc // 2, nc) if r >= CMP_NEAR + 2 * SUBLANES]
    tiles_per = QT // CMP_STRIDE
    for idx, rows_c in enumerate(sizes):
        lo = 0 if idx == 0 else sizes[idx - 1] // tiles_per
        hi = rows_c // tiles_per
        pl.when((a >= lo) & (a < hi))(functools.partial(compressed_and_select, rows_c))
    o_c = ocw_ref[0]
    o_w = ocw_ref[1]

    n_steps = a // sub_per_step + 1
    n_far2 = jnp.maximum(a - (NEAR_SLOTS - 1), 0) // (2 * sub_per_step)
    n_pairs = n_steps // 2
    scores(0, 0, True)
    state = tuple((jnp.full((1, QL), NEG, F32), jnp.zeros((V_ROWS, QL), F32)) for _ in range(SEL_STREAMS))
    state = lax.fori_loop(0, n_far2, functools.partial(two_steps, near=False), state)
    state = lax.fori_loop(n_far2, n_pairs, functools.partial(two_steps, near=True), state)
    state = lax.cond(n_steps % 2 == 1, lambda st: update(n_steps - 1, 0, st), lambda st: st, state)
    m_s = functools.reduce(jnp.maximum, [m for m, _ in state])
    acc_s = functools.reduce(lambda x, y: x + y, [jnp.exp2(m - m_s) * acc for m, acc in state])
    o_s = acc_s[0:DH] * (1.0 / acc_s[DH:DH + 1])

    def gate(branch):
        return jnp.concatenate([gt_ref[0, 0, 3 * hh + branch:3 * hh + branch + 1, :] for hh in range(HPG)],
                               axis=1)

    o_ref[0, 0, 0] = (gate(0) * o_c + gate(1) * o_s + gate(2) * o_w).astype(BF16)


def _nsa_attn_wrapper(*refs):
    nw = WINDOW // QT + 1
    q_ref, ks_ref, vst_ref = refs[0:3]
    kw_refs = refs[3:3 + nw]
    vwt_refs = refs[3 + nw:3 + 2 * nw]
    rest = refs[3 + 2 * nw:]
    _nsa_attn_kernel(q_ref, ks_ref, vst_ref, kw_refs, vwt_refs, *rest)


def _out_proj_kernel(o_ref, x_ref, mod_ref, w_ref, pg_ref, out_ref):
    n_tiles = o_ref.shape[2]
    o_t = jnp.concatenate(
        [jnp.concatenate([o_ref[0, g, i, :, hh * QT:(hh + 1) * QT] for i in range(n_tiles)], axis=1)
         for g in range(N_GROUPS) for hh in range(HPG)], axis=0)
    y = jnp.dot(o_t.astype(F32).T.astype(BF16), w_ref[...], preferred_element_type=F32)
    out_ref[0] = _post_residual(x_ref[0], y, pg_ref[0], mod_ref[0, 0])


def _nsa_layer(x, mod, layer, j, p):
    B, S, D = x.shape
    G, DH, KV = N_GROUPS, HEAD_DIM, N_GROUPS * HEAD_DIM
    T = TOK_TILE
    NC = S // CMP_STRIDE
    n_sel = S // SEL_BLOCK
    assert D == D_MODEL and S % (2 * SEL_STREAMS * SEL_SUB * QT) == 0 and S % T == 0

    w_in = p["nsa_w_in"][j]
    cuts = [D + i * KV for i in range(7)]
    wq, wkc, wvc, wks, wvs, wkw, wvw, wg = jnp.split(w_in, cuts, axis=-1)
    wg = jnp.zeros((D, LANES), F32).at[:, :3 * N_HEADS].set(wg)

    def slabs(w):
        return jnp.zeros((D, G, LANES), F32).at[:, :, :DH].set(w.reshape(D, G, DH)).reshape(D, G * LANES)

    w_all = jnp.concatenate([wq * (HEAD_DIM ** -0.5 * LOG2E), slabs(wks), slabs(wkw), wvs, wvw, wkc, wvc, wg],
                            axis=1).astype(BF16)

    qo_spec = pl.BlockSpec((1, G, T // QT, DH, QL), lambda b, i: (b, 0, i, 0, 0))
    k_spec = pl.BlockSpec((1, G, T, LANES), lambda b, i: (b, 0, i, 0))
    kc_spec = pl.BlockSpec((1, G, T, DH), lambda b, i: (b, 0, i, 0))
    vt_spec = pl.BlockSpec((1, G, V_ROWS, T), lambda b, i: (b, 0, 0, i))
    q, ks, vst, kw, vwt, kc_raw, vc_raw, gt = pl.pallas_call(
        _nsa_in_kernel,
        grid=(B, S // T),
        in_specs=[pl.BlockSpec((1, T, D), lambda b, i: (b, i, 0)), _mod_spec(layer), _vec_spec(layer),
                  pl.BlockSpec((D, NSA_W), lambda b, i: (0, 0))],
        out_specs=[qo_spec, k_spec, vt_spec, k_spec, vt_spec,
                   kc_spec, kc_spec, pl.BlockSpec((1, 3 * N_HEADS, T), lambda b, i: (b, 0, i))],
        out_shape=[jax.ShapeDtypeStruct((B, G, S // QT, DH, QL), BF16),
                   jax.ShapeDtypeStruct((B, G, S, LANES), BF16), jax.ShapeDtypeStruct((B, G, V_ROWS, S), BF16),
                   jax.ShapeDtypeStruct((B, G, S, LANES), BF16), jax.ShapeDtypeStruct((B, G, V_ROWS, S), BF16),
                   jax.ShapeDtypeStruct((B, G, S, DH), F32), jax.ShapeDtypeStruct((B, G, S, DH), F32),
                   jax.ShapeDtypeStruct((B, 3 * N_HEADS, S), F32)],
        compiler_params=_cparams(("arbitrary", "arbitrary")),
        name="nsa_in_proj",
    )(x, mod, p["mix_pre_g"], w_all)

    half = CMP_LEN // 2
    chunk_w = half * DH

    def split_w1(w1):
        return w1.reshape(2, chunk_w, CMP_HIDDEN).astype(BF16)

    def split_pe(pe):
        return pe.reshape(2, chunk_w)

    c_spec = pl.BlockSpec((1, 1, NC, chunk_w), lambda b, g: (b, g, 0, 0))
    full = lambda shape: pl.BlockSpec(shape, lambda b, g: (0,) * len(shape))
    kc, vct = pl.pallas_call(
        _compress_kernel,
        grid=(B, G),
        in_specs=[c_spec, c_spec,
                  full((2, chunk_w)), full((2, chunk_w, CMP_HIDDEN)), full((CMP_HIDDEN, LANES)),
                  full((2, chunk_w)), full((2, chunk_w, CMP_HIDDEN)), full((CMP_HIDDEN, DH))],
        out_specs=[pl.BlockSpec((1, 1, NC, LANES), lambda b, g: (b, g, 0, 0)),
                   pl.BlockSpec((1, 1, V_ROWS, NC), lambda b, g: (b, g, 0, 0))],
        out_shape=[jax.ShapeDtypeStruct((B, G, NC, LANES), BF16),
                   jax.ShapeDtypeStruct((B, G, V_ROWS, NC), BF16)],
        compiler_params=_cparams(("arbitrary", "arbitrary")),
        name="nsa_compress",
    )(kc_raw.reshape(B, G, NC, chunk_w), vc_raw.reshape(B, G, NC, chunk_w),
      split_pe(p["nsa_k_pe"][j]), split_w1(p["nsa_k_w1"][j]),
      jnp.zeros((CMP_HIDDEN, LANES), BF16).at[:, :DH].set(p["nsa_k_w2"][j].astype(BF16)),
      split_pe(p["nsa_v_pe"][j]), split_w1(p["nsa_v_w1"][j]), p["nsa_v_w2"][j].astype(BF16))

    ctab_rows = 2 * NC + LANES
    tab, ctab, far = pl.pallas_call(
        _bias_table_kernel,
        grid=(G,),
        in_specs=[pl.BlockSpec(memory_space=pltpu.SMEM)],
        out_specs=[pl.BlockSpec((1, N_SLOTS, QT, QL), lambda g: (g, 0, 0, 0)),
                   pl.BlockSpec((1, ctab_rows, QL), lambda g: (g, 0, 0)),
                   pl.BlockSpec((1, 1, QL), lambda g: (g, 0, 0))],
        out_shape=[jax.ShapeDtypeStruct((G, N_SLOTS, QT, QL), F32),
                   jax.ShapeDtypeStruct((G, ctab_rows, QL), F32),
                   jax.ShapeDtypeStruct((G, 1, QL), F32)],
        compiler_params=_cparams(("arbitrary",)),
        name="nsa_bias_tables",
    )(p["rel_table"])

    nw = WINDOW // QT + 1
    kw_specs = [pl.BlockSpec((1, 1, QT, LANES), functools.partial(
        lambda b, g, a, d: (b, g, jnp.maximum(a - d, 0), 0), d=d)) for d in range(nw)]
    vwt_specs = [pl.BlockSpec((1, 1, V_ROWS, QT), functools.partial(
        lambda b, g, a, d: (b, g, 0, jnp.maximum(a - d, 0)), d=d)) for d in range(nw)]
    o = pl.pallas_call(
        _nsa_attn_wrapper,
        grid=(B, G, S // QT),
        in_specs=[pl.BlockSpec((1, 1, 1, DH, QL), lambda b, g, a: (b, g, a, 0, 0)),
                  pl.BlockSpec((1, 1, S, LANES), lambda b, g, a: (b, g, 0, 0)),
                  pl.BlockSpec((1, 1, V_ROWS, S), lambda b, g, a: (b, g, 0, 0)),
                  *kw_specs, *vwt_specs,
                  pl.BlockSpec((1, 1, NC, LANES), lambda b, g, a: (b, g, 0, 0)),
                  pl.BlockSpec((1, 1, V_ROWS, NC), lambda b, g, a: (b, g, 0, 0)),
                  pl.BlockSpec((1, 1, 3 * HPG, QT), lambda b, g, a: (b, g, 0, a)),
                  pl.BlockSpec((1, N_SLOTS, QT, QL), lambda b, g, a: (g, 0, 0, 0)),
                  pl.BlockSpec((1, ctab_rows, QL), lambda b, g, a: (g, 0, 0)),
                  pl.BlockSpec((1, 1, QL), lambda b, g, a: (g, 0, 0))],
        out_specs=pl.BlockSpec((1, 1, 1, DH, QL), lambda b, g, a: (b, g, a, 0, 0)),
        out_shape=jax.ShapeDtypeStruct((B, G, S // QT, DH, QL), BF16),
        scratch_shapes=[pltpu.VMEM((2, LANES, QL), BF16),
                        pltpu.VMEM((SEL_STREAMS * SEL_SUB * QT, QL), F32),
                        pltpu.VMEM((SEL_STREAMS * SEL_SUB * QT, QL), F32),
                        pltpu.VMEM((NC, QT), F32), pltpu.VMEM((n_sel, QL), F32),
                        pltpu.VMEM((2, DH, QL), F32), pltpu.VMEM((2, SEL_STREAMS, 1, QL), F32)],
        compiler_params=_cparams(("arbitrary", "arbitrary", "arbitrary")),
        name="nsa_attention",
    )(q, ks, vst, *([kw] * nw), *([vwt] * nw), kc, vct, gt.reshape(B, G, 3 * HPG, S), tab, ctab, far)

    return pl.pallas_call(
        _out_proj_kernel,
        grid=(B, S // T),
        in_specs=[qo_spec,
                  pl.BlockSpec((1, T, D), lambda b, i: (b, i, 0)), _mod_spec(layer),
                  pl.BlockSpec((D, D), lambda b, i: (0, 0)), _vec_spec(layer)],
        out_specs=pl.BlockSpec((1, T, D), lambda b, i: (b, i, 0)),
        out_shape=jax.ShapeDtypeStruct((B, S, D), F32),
        compiler_params=_cparams(("arbitrary", "arbitrary")),
        name="nsa_out_proj",
    )(o, x, mod, p["nsa_w_out"][j].astype(BF16), p["mix_post_g"])


def _as_rows(v):
    return v.reshape(v.shape[0], 1, v.shape[-1])


def kernel(x, c, mix_pre_g, mix_post_g, mix_ada_w, mix_ada_b, ffn_pre_g, ffn_post_g, ffn_ada_w, ffn_ada_b, conf_w_pw1, conf_b_pw1, conf_w_dw, conf_b_dw, conf_ln_g, conf_ln_b, conf_w_pw2, conf_b_pw2, sc_w_in, sc_w_conv, sc_w_out, nsa_w_in, nsa_k_pe, nsa_k_w1, nsa_k_w2, nsa_v_pe, nsa_v_w1, nsa_v_w2, nsa_w_out, rel_table, dense_w_gate, dense_w_up, dense_w_down, moe_w_router, moe_b_router, moe_w_gate, moe_w_up, moe_w_down):
    p = dict(
        mix_pre_g=_as_rows(mix_pre_g), mix_post_g=_as_rows(mix_post_g),
        ffn_pre_g=_as_rows(ffn_pre_g), ffn_post_g=_as_rows(ffn_post_g),
        conf_w_pw1=conf_w_pw1, conf_b_pw1=_as_rows(conf_b_pw1), conf_w_dw=conf_w_dw,
        conf_b_dw=_as_rows(conf_b_dw), conf_ln_g=_as_rows(conf_ln_g), conf_ln_b=_as_rows(conf_ln_b),
        conf_w_pw2=conf_w_pw2, conf_b_pw2=_as_rows(conf_b_pw2),
        sc_w_in=sc_w_in, sc_w_conv=sc_w_conv, sc_w_out=sc_w_out,
        nsa_w_in=nsa_w_in, nsa_k_pe=nsa_k_pe, nsa_k_w1=nsa_k_w1, nsa_k_w2=nsa_k_w2,
        nsa_v_pe=nsa_v_pe, nsa_v_w1=nsa_v_w1, nsa_v_w2=nsa_v_w2, nsa_w_out=nsa_w_out,
        rel_table=rel_table,
        dense_w_gate=dense_w_gate, dense_w_up=dense_w_up, dense_w_down=dense_w_down,
        moe_w_router=moe_w_router, moe_b_router=moe_b_router,
        moe_w_gate=moe_w_gate, moe_w_up=moe_w_up, moe_w_down=moe_w_down,
    )
    mix_mod, ffn_mod = _adaln(c, mix_ada_w, mix_ada_b, ffn_ada_w, ffn_ada_b)
    mixers = (_conformer_layer, _short_conv_layer, _nsa_layer)
    for i in range(DEPTH):
        x = mixers[i % 3](x, mix_mod, i, i // 3, p)
        ffn = _dense_ffn_layer if i % 2 == 0 else _moe_ffn_layer
        x = ffn(x, ffn_mod, i, i // 2, p)
    return x
```

```python
import functools
import math

import numpy as np
import jax
import jax.numpy as jnp
from jax import lax
from jax.experimental import pallas as pl
from jax.experimental.pallas import tpu as pltpu

F32 = jnp.float32
BF16 = jnp.bfloat16
I32 = jnp.int32

D_MODEL = 1024
DEPTH = 4
RMS_EPS = 1e-6
LN_EPS = 1e-5
NEG = -1e30

CONF_KERNEL = 31
SHORT_KERNEL = 3
N_HEADS = 16
HEAD_DIM = 64
N_GROUPS = 4
HPG = 4
CMP_LEN = 32
CMP_STRIDE = 16
CMP_HIDDEN = 128
SEL_BLOCK = 64
SEL_TOPK = 16
N_LOCAL_BLOCKS = 2
WINDOW = 512
REL_BUCKETS = 32
REL_MAX_DIST = 2048
D_FF_DENSE = 2816
N_EXPERTS = 8
D_FF_EXPERT = 3584

LANES = 128
SUBLANES = 8
VMEM_LIMIT = 56 * 1024 * 1024

ADA_TN = 1024
TOK_TILE = 512
CONV_TILE = 256
CONV_HALO = 32
FFN_TILE = 1024
FFN_TF = 256
MOE_TF = 512
MOE_ROWS = 1024
ROUTE_TILE = 512
ROW_DMA_TILE = 256
ROW_DMA_UNROLL = 8
DEST_TILE = 2048
QT = 128
QL = HPG * QT
NEAR_SLOTS = 13
SLOT_FAR = 13
SLOT_WIN = 14
SLOT_NONE = 15
N_SLOTS = 16
CMP_NEAR = 112
SEL_SUB = 4
SEL_STREAMS = 2
SEL_PEN_ROWS = SEL_STREAMS * SEL_SUB * QT // SEL_BLOCK
K_ONE = HEAD_DIM
K_PEN = 80
V_ROWS = 80
LOG2E = 1.4426950408889634


def _bucket_thresholds():
    n = np.arange(0, 4 * REL_MAX_DIST)
    max_exact = REL_BUCKETS // 2
    nf = np.maximum(n, 1).astype(np.float32)
    large = max_exact + (np.log(nf / np.float32(max_exact))
                         / np.float32(math.log(REL_MAX_DIST / max_exact))
                         * np.float32(REL_BUCKETS - max_exact)).astype(np.int32)
    large = np.minimum(large, REL_BUCKETS - 1)
    b = np.where(n < max_exact, n, large)
    return [int(np.argmax(b >= k)) for k in range(REL_BUCKETS)]


BUCKET_THR = _bucket_thresholds()
assert BUCKET_THR[REL_BUCKETS - 1] <= QT * NEAR_SLOTS - (QT - 1)
assert BUCKET_THR[REL_BUCKETS - 1] <= CMP_STRIDE * (CMP_NEAR - 8 + 1) - CMP_LEN + 1


def _cparams(sem):
    return pltpu.CompilerParams(dimension_semantics=sem, vmem_limit_bytes=VMEM_LIMIT)


def _bdot(a, b):
    return jnp.dot(a.astype(BF16), b.astype(BF16), preferred_element_type=F32)


def _sigmoid(x):
    return 1.0 / (1.0 + jnp.exp(-x))


def _rms(x, g):
    return x * lax.rsqrt(jnp.mean(x * x, axis=-1, keepdims=True) + RMS_EPS) * g


def _norm_mod(x, g, mod):
    return _rms(x, g) * (1.0 + mod[1:2]) + mod[0:1]


def _adaln_kernel(c_ref, wm_ref, bm_ref, wf_ref, bf_ref, om_ref, of_ref):
    c = c_ref[...]
    s = c * _sigmoid(c)
    om_ref[0] = jnp.dot(s, wm_ref[0], precision=lax.Precision.HIGHEST,
                        preferred_element_type=F32) + bm_ref[0]
    of_ref[0] = jnp.dot(s, wf_ref[0], precision=lax.Precision.HIGHEST,
                        preferred_element_type=F32) + bf_ref[0]


def _adaln(c, mix_w, mix_b, ffn_w, ffn_b):
    B, D = c.shape
    depth = mix_w.shape[0]
    cp = jnp.zeros((SUBLANES, D), F32).at[:B].set(c)
    w_spec = pl.BlockSpec((1, D, ADA_TN), lambda l, j: (l, 0, j))
    b_spec = pl.BlockSpec((1, 1, ADA_TN), lambda l, j: (l, 0, j))
    o_spec = pl.BlockSpec((1, SUBLANES, ADA_TN), lambda l, j: (l, 0, j))
    om, of = pl.pallas_call(
        _adaln_kernel,
        grid=(depth, 3 * D // ADA_TN),
        in_specs=[pl.BlockSpec((SUBLANES, D), lambda l, j: (0, 0)), w_spec, b_spec, w_spec, b_spec],
        out_specs=[o_spec, o_spec],
        out_shape=[jax.ShapeDtypeStruct((depth, SUBLANES, 3 * D), F32)] * 2,
        compiler_params=_cparams(("arbitrary", "arbitrary")),
        name="adaln",
    )(cp, mix_w, mix_b.reshape(depth, 1, 3 * D), ffn_w, ffn_b.reshape(depth, 1, 3 * D))
    return (om[:, :B].reshape(depth, B, 3, D), of[:, :B].reshape(depth, B, 3, D))


def _mod_spec(layer):
    return pl.BlockSpec((1, 1, 3, D_MODEL), lambda b, *_: (layer, b, 0, 0))


def _vec_spec(layer, width=D_MODEL):
    return pl.BlockSpec((1, 1, width), lambda *_: (layer, 0, 0))


def _post_residual(x, y, post_g, mod):
    return x + mod[2:3] * _rms(y, post_g)


def _conf_a_kernel(x_ref, mod_ref, g_ref, w_ref, b_ref, u_ref):
    D = D_MODEL
    h = _norm_mod(x_ref[0], g_ref[0], mod_ref[0, 0])
    z = _bdot(h, w_ref[...]) + b_ref[0]
    u_ref[0] = z[:, :D] * _sigmoid(z[:, D:])


def _conf_b_kernel(u_ref, up_ref, x_ref, mod_ref, wdw_ref, bdw_ref, lng_ref, lnb_ref,
                   w2_ref, b2_ref, pg_ref, o_ref, ext_ref, conv_ref):
    T, D, K = CONV_TILE, D_MODEL, CONF_KERNEL
    i = pl.program_id(1)
    keep = (i > 0).astype(F32)
    ext_ref[0:CONV_HALO, :] = up_ref[0] * keep
    ext_ref[CONV_HALO:, :] = u_ref[0]
    rb, cb = 64, 256
    off = CONV_HALO - (K - 1)

    def col_body(c, carry):
        c0 = pl.multiple_of(c * cb, cb)
        bias = bdw_ref[0, :, pl.ds(c0, cb)]
        for r in range(T // rb):
            acc = jnp.zeros((rb // SUBLANES, SUBLANES, cb), F32) + bias[None]
            for res in range(SUBLANES):
                offs = [o for o in range(off, off + K) if o % SUBLANES == res]
                lo, hi = offs[0], offs[-1]
                win = ext_ref[pl.ds(r * rb + lo, hi - lo + rb), pl.ds(c0, cb)]
                win = win.reshape((hi - lo + rb) // SUBLANES, SUBLANES, cb)
                for o in offs:
                    i0 = (o - lo) // SUBLANES
                    w = wdw_ref[0, o - off, :, pl.ds(c0, cb)]
                    acc = acc + w[None] * win[i0:i0 + rb // SUBLANES]
            conv_ref[pl.ds(r * rb, rb), pl.ds(c0, cb)] = acc.reshape(rb, cb)
        return carry

    lax.fori_loop(0, D // cb, col_body, 0)
    v = conv_ref[...]
    mu = jnp.mean(v, axis=-1, keepdims=True)
    var = jnp.mean(jnp.square(v - mu), axis=-1, keepdims=True)
    ln = (v - mu) * lax.rsqrt(var + LN_EPS) * lng_ref[0] + lnb_ref[0]
    act = ln * _sigmoid(ln)
    y = _bdot(act, w2_ref[...]) + b2_ref[0]
    o_ref[0] = _post_residual(x_ref[0], y, pg_ref[0], mod_ref[0, 0])


def _conformer_layer(x, mod, layer, j, p):
    B, S, D = x.shape
    T = TOK_TILE
    u = pl.pallas_call(
        _conf_a_kernel,
        grid=(B, S // T),
        in_specs=[pl.BlockSpec((1, T, D), lambda b, i: (b, i, 0)), _mod_spec(layer),
                  _vec_spec(layer), pl.BlockSpec((D, 2 * D), lambda b, i: (0, 0)),
                  _vec_spec(j, 2 * D)],
        out_specs=pl.BlockSpec((1, T, D), lambda b, i: (b, i, 0)),
        out_shape=jax.ShapeDtypeStruct((B, S, D), F32),
        compiler_params=_cparams(("arbitrary", "arbitrary")),
        name="conf_pw1_glu",
    )(x, mod, p["mix_pre_g"], p["conf_w_pw1"][j].astype(BF16), p["conf_b_pw1"])
    T = CONV_TILE
    hb = T // CONV_HALO
    return pl.pallas_call(
        _conf_b_kernel,
        grid=(B, S // T),
        in_specs=[pl.BlockSpec((1, T, D), lambda b, i: (b, i, 0)),
                  pl.BlockSpec((1, CONV_HALO, D), lambda b, i: (b, jnp.maximum(i * hb - 1, 0), 0)),
                  pl.BlockSpec((1, T, D), lambda b, i: (b, i, 0)), _mod_spec(layer),
                  pl.BlockSpec((1, CONF_KERNEL, SUBLANES, D), lambda b, i: (j, 0, 0, 0)),
                  _vec_spec(j), _vec_spec(j), _vec_spec(j),
                  pl.BlockSpec((D, D), lambda b, i: (0, 0)), _vec_spec(j), _vec_spec(layer)],
        out_specs=pl.BlockSpec((1, T, D), lambda b, i: (b, i, 0)),
        out_shape=jax.ShapeDtypeStruct((B, S, D), F32),
        scratch_shapes=[pltpu.VMEM((T + CONV_HALO, D), F32), pltpu.VMEM((T, D), F32)],
        compiler_params=_cparams(("arbitrary", "arbitrary")),
        name="conf_conv_pw2",
    )(u, u, x, mod, jnp.broadcast_to(p["conf_w_dw"][:, :, None, :], (*p["conf_w_dw"].shape[:2], SUBLANES, D)),
      p["conf_b_dw"], p["conf_ln_g"], p["conf_ln_b"],
      p["conf_w_pw2"][j].astype(BF16), p["conf_b_pw2"], p["mix_post_g"])


def _short_conv_kernel(x_ref, mod_ref, g_ref, win_ref, wc_ref, wout_ref, pg_ref, o_ref, ext_ref):
    T, D = TOK_TILE, D_MODEL
    i = pl.program_id(1)

    @pl.when(i == 0)
    def _():
        ext_ref[0:SUBLANES, :] = jnp.zeros((SUBLANES, D), F32)

    x = x_ref[0]
    mod = mod_ref[0, 0]
    h = _norm_mod(x, g_ref[0], mod)
    z = _bdot(h, win_ref[...])
    v, gate_b, gate_c = z[:, :D], z[:, D:2 * D], z[:, 2 * D:]
    ext_ref[SUBLANES:, :] = gate_c * v
    wc = wc_ref[0]
    u = (wc[0:1] * ext_ref[pl.ds(SUBLANES - 2, T), :] + wc[1:2] * ext_ref[pl.ds(SUBLANES - 1, T), :]
         + wc[2:3] * ext_ref[pl.ds(SUBLANES, T), :])
    ext_ref[0:SUBLANES, :] = ext_ref[pl.ds(T, SUBLANES), :]
    y = _bdot(gate_b * u, wout_ref[...])
    o_ref[0] = _post_residual(x, y, pg_ref[0], mod)


def _short_conv_layer(x, mod, layer, j, p):
    B, S, D = x.shape
    T = TOK_TILE
    return pl.pallas_call(
        _short_conv_kernel,
        grid=(B, S // T),
        in_specs=[pl.BlockSpec((1, T, D), lambda b, i: (b, i, 0)), _mod_spec(layer), _vec_spec(layer),
                  pl.BlockSpec((D, 3 * D), lambda b, i: (0, 0)),
                  pl.BlockSpec((1, SHORT_KERNEL, D), lambda b, i: (j, 0, 0)),
                  pl.BlockSpec((D, D), lambda b, i: (0, 0)), _vec_spec(layer)],
        out_specs=pl.BlockSpec((1, T, D), lambda b, i: (b, i, 0)),
        out_shape=jax.ShapeDtypeStruct((B, S, D), F32),
        scratch_shapes=[pltpu.VMEM((T + SUBLANES, D), F32)],
        compiler_params=_cparams(("arbitrary", "arbitrary")),
        name="short_conv",
    )(x, mod, p["mix_pre_g"], p["sc_w_in"][j].astype(BF16), p["sc_w_conv"],
      p["sc_w_out"][j].astype(BF16), p["mix_post_g"])


def _swiglu_step(h, wg_ref, wu_ref, wd_ref, acc_ref):
    a = _bdot(h, wg_ref[...])
    u = _bdot(h, wu_ref[...])
    acc_ref[...] += _bdot((a * _sigmoid(a)) * u, wd_ref[...])


def _dense_ffn_kernel(x_ref, mod_ref, g_ref, wg_ref, wu_ref, wd_ref, pg_ref, o_ref, h_ref, acc_ref):
    f = pl.program_id(2)

    @pl.when(f == 0)
    def _():
        h_ref[...] = _norm_mod(x_ref[0], g_ref[0], mod_ref[0, 0]).astype(BF16)
        acc_ref[...] = jnp.zeros_like(acc_ref)

    _swiglu_step(h_ref[...], wg_ref.at[0], wu_ref.at[0], wd_ref.at[0], acc_ref)

    @pl.when(f == pl.num_programs(2) - 1)
    def _():
        o_ref[0] = _post_residual(x_ref[0], acc_ref[...], pg_ref[0], mod_ref[0, 0])


def _dense_ffn_layer(x, mod, layer, j, p):
    B, S, D = x.shape
    T, TF, F = min(FFN_TILE, S), FFN_TF, D_FF_DENSE
    return pl.pallas_call(
        _dense_ffn_kernel,
        grid=(B, S // T, F // TF),
        in_specs=[pl.BlockSpec((1, T, D), lambda b, i, f: (b, i, 0)), _mod_spec(layer), _vec_spec(layer),
                  pl.BlockSpec((1, D, TF), lambda b, i, f: (j, 0, f)),
                  pl.BlockSpec((1, D, TF), lambda b, i, f: (j, 0, f)),
                  pl.BlockSpec((1, TF, D), lambda b, i, f: (j, f, 0)), _vec_spec(layer)],
        out_specs=pl.BlockSpec((1, T, D), lambda b, i, f: (b, i, 0)),
        out_shape=jax.ShapeDtypeStruct((B, S, D), F32),
        scratch_shapes=[pltpu.VMEM((T, D), BF16), pltpu.VMEM((T, D), F32)],
        compiler_params=_cparams(("arbitrary", "arbitrary", "arbitrary")),
        name="dense_swiglu",
    )(x, mod, p["ffn_pre_g"], p["dense_w_gate"], p["dense_w_up"], p["dense_w_down"], p["ffn_post_g"])


def _router_kernel(x_ref, mod_ref, g_ref, wr_ref, br_ref, h_ref, ri_ref, rw_ref, cnt_ref, zero_ref, base_ref):
    T = ROUTE_TILE
    i = pl.program_id(0)

    @pl.when(i == 0)
    def _():
        base_ref[...] = jnp.zeros_like(base_ref)

    zero_ref[...] = jnp.zeros_like(zero_ref)

    h = _norm_mod(x_ref[...], g_ref[0], mod_ref[0, 0])
    h_ref[...] = h
    logits = jnp.dot(h, wr_ref[...], precision=lax.Precision.HIGHEST,
                     preferred_element_type=F32) + br_ref[0]
    lane = lax.broadcasted_iota(I32, (T, LANES), 1).astype(F32)
    logits = jnp.where(lane < N_EXPERTS, logits, -jnp.inf)
    l0 = jnp.max(logits, axis=1, keepdims=True)
    e0 = jnp.min(jnp.where(logits == l0, lane, float(LANES)), axis=1, keepdims=True)
    rest = jnp.where(lane == e0, -jnp.inf, logits)
    l1 = jnp.max(rest, axis=1, keepdims=True)
    e1 = jnp.min(jnp.where(rest == l1, lane, float(LANES)), axis=1, keepdims=True)
    ex = jnp.exp(l1 - l0)
    w0 = 1.0 / (1.0 + ex)
    w1 = ex / (1.0 + ex)
    onehot = ((lane == e0) | (lane == e1)).astype(F32)
    row = lax.broadcasted_iota(I32, (T, T), 0)
    col = lax.broadcasted_iota(I32, (T, T), 1)
    tri = jnp.where(row > col, 1.0, 0.0).astype(BF16)
    before = jnp.dot(tri, onehot.astype(BF16), preferred_element_type=F32) + base_ref[...]
    r0 = jnp.sum(jnp.where(lane == e0, before, 0.0), axis=1, keepdims=True)
    r1 = jnp.sum(jnp.where(lane == e1, before, 0.0), axis=1, keepdims=True)
    base_ref[...] += jnp.sum(onehot, axis=0, keepdims=True)
    cnt_ref[...] = base_ref[...].astype(I32)
    ri = jnp.where(lane == 0, e0, jnp.where(lane == 1, e1, jnp.where(lane == 2, r0, r1)))
    ri_ref[...] = ri[:, :SUBLANES].astype(I32)
    rw_ref[...] = jnp.where(lane == 0, w0, w1)[:, :SUBLANES]


def _dest_kernel(pstart_ref, ri_ref, dst_ref):
    ri = ri_ref[...]
    lane = lax.broadcasted_iota(I32, ri.shape, 1)
    expert = jnp.where(lane == 0, ri[:, 0:1], ri[:, 1:2])
    rank = jnp.where(lane == 0, ri[:, 2:3], ri[:, 3:4])
    start = jnp.zeros(ri.shape, I32)
    for e in range(N_EXPERTS):
        start = jnp.where(expert == e, pstart_ref[e], start)
    dst_ref[...] = start + rank


def _dispatch_kernel(dst_ref, h_ref, zero_ref, buf_ref, stage_ref, sem):
    del zero_ref
    T = ROW_DMA_TILE
    i = pl.program_id(0)
    slot = i % 2
    stage_ref[slot] = h_ref[...]

    def row_copy(t, dest, s):
        return pltpu.make_async_copy(stage_ref.at[s, pl.ds(t, 1)], buf_ref.at[pl.ds(dest, 1)], sem.at[s])

    def issue(t, carry):
        for k in range(2):
            row_copy(t, dst_ref[t, k], slot).start()
        return carry

    lax.fori_loop(0, T, issue, 0, unroll=ROW_DMA_UNROLL)

    def drain(s):
        def body(t, carry):
            for k in range(2):
                row_copy(0, 0, s).wait()
            return carry

        lax.fori_loop(0, T, body, 0, unroll=ROW_DMA_UNROLL)

    @pl.when(i > 0)
    def _():
        drain(1 - slot)

    @pl.when(i == pl.num_programs(0) - 1)
    def _():
        drain(slot)


def _expert_kernel(blk_e_ref, nblk_ref, xb_ref, wg_ref, wu_ref, wd_ref, o_ref, h_ref):
    del blk_e_ref
    i = pl.program_id(0)
    f = pl.program_id(1)
    live = i < nblk_ref[0]

    @pl.when(f == 0)
    def _():
        h_ref[...] = xb_ref[...].astype(BF16)
        o_ref[...] = jnp.zeros_like(o_ref)

    @pl.when(live)
    def _():
        _swiglu_step(h_ref[...], wg_ref.at[0, 0], wu_ref.at[0, 0], wd_ref.at[0, 0], o_ref)


def _combine_kernel(x_ref, mod_ref, pg_ref, dst_ref, dst_next_ref, rw_ref, yb_ref, o_ref, y_ref, sem):
    T = ROW_DMA_TILE
    i = pl.program_id(0)
    slot = i % 2

    def row_copy(src, t, k, s):
        return pltpu.make_async_copy(yb_ref.at[pl.ds(src, 1)], y_ref.at[s, k, pl.ds(t, 1)], sem.at[s])

    def gather(idx_ref, s):
        def body(t, carry):
            for k in range(2):
                row_copy(idx_ref[t, k], t, k, s).start()
            return carry

        lax.fori_loop(0, T, body, 0, unroll=ROW_DMA_UNROLL)

    @pl.when(i == 0)
    def _():
        gather(dst_ref, slot)

    @pl.when(i + 1 < pl.num_programs(0))
    def _():
        gather(dst_next_ref, 1 - slot)

    def drain(t, carry):
        for k in range(2):
            row_copy(0, 0, k, slot).wait()
        return carry

    lax.fori_loop(0, T, drain, 0, unroll=ROW_DMA_UNROLL)
    rw = rw_ref[...]
    y = rw[:, 0:1] * y_ref[slot, 0] + rw[:, 1:2] * y_ref[slot, 1]
    o_ref[...] = _post_residual(x_ref[...], y, pg_ref[0], mod_ref[0, 0])


def _moe_ffn_layer(x, mod, layer, j, p):
    B, S, D = x.shape
    N = B * S
    T = ROUTE_TILE
    assert D == D_MODEL and S % T == 0 and S % ROW_DMA_TILE == 0 and N % DEST_TILE == 0
    tiles_per_seq = S // T
    xt = x.reshape(N, D)
    wr = jnp.zeros((D, LANES), F32).at[:, :N_EXPERTS].set(p["moe_w_router"][j])
    br = jnp.zeros((1, 1, LANES), F32).at[0, 0, :N_EXPERTS].set(p["moe_b_router"][j])
    R = MOE_ROWS
    n_blocks = -(-(2 * N + N_EXPERTS * (R - 1)) // R)
    rows = n_blocks * R
    n_route = N // T
    zero_rows = rows // n_route
    assert zero_rows * n_route == rows and zero_rows % SUBLANES == 0

    h, ri, rw, cnt, zeros = pl.pallas_call(
        _router_kernel,
        grid=(n_route,),
        in_specs=[pl.BlockSpec((T, D), lambda i: (i, 0)),
                  pl.BlockSpec((1, 1, 3, D), lambda i: (layer, i // tiles_per_seq, 0, 0)),
                  _vec_spec(layer), pl.BlockSpec((D, LANES), lambda i: (0, 0)), _vec_spec(0, LANES)],
        out_specs=[pl.BlockSpec((T, D), lambda i: (i, 0)), pl.BlockSpec((T, SUBLANES), lambda i: (i, 0)),
                   pl.BlockSpec((T, SUBLANES), lambda i: (i, 0)), pl.BlockSpec((1, LANES), lambda i: (0, 0)),
                   pl.BlockSpec((zero_rows, D), lambda i: (i, 0))],
        out_shape=[jax.ShapeDtypeStruct((N, D), F32), jax.ShapeDtypeStruct((N, SUBLANES), I32),
                   jax.ShapeDtypeStruct((N, SUBLANES), F32), jax.ShapeDtypeStruct((1, LANES), I32),
                   jax.ShapeDtypeStruct((rows, D), F32)],
        scratch_shapes=[pltpu.VMEM((1, LANES), F32)],
        compiler_params=_cparams(("arbitrary",)),
        name="moe_router",
    )(xt, mod, p["ffn_pre_g"], wr, br)

    counts = cnt[0, :N_EXPERTS]
    padded = (counts + R - 1) // R * R
    pend = jnp.cumsum(padded)
    pstart = (pend - padded).astype(I32)
    blk_e = jnp.minimum(jnp.sum((jnp.arange(n_blocks) * R)[:, None] >= pend[None, :], axis=1),
                        N_EXPERTS - 1).astype(I32)
    n_live = (pend[-1] // R).astype(I32).reshape(1)

    TR = DEST_TILE
    dst = pl.pallas_call(
        _dest_kernel,
        grid_spec=pltpu.PrefetchScalarGridSpec(
            num_scalar_prefetch=1,
            grid=(N // TR,),
            in_specs=[pl.BlockSpec((TR, SUBLANES), lambda i, ps: (i, 0))],
            out_specs=pl.BlockSpec((TR, SUBLANES), lambda i, ps: (i, 0)),
        ),
        out_shape=jax.ShapeDtypeStruct((N, SUBLANES), I32),
        compiler_params=_cparams(("arbitrary",)),
        name="moe_dest",
    )(pstart, ri)

    TD = ROW_DMA_TILE
    n_dma_tiles = N // TD
    dst_spec = pl.BlockSpec((TD, SUBLANES), lambda i: (i, 0), memory_space=pltpu.SMEM)
    buf = pl.pallas_call(
        _dispatch_kernel,
        grid=(n_dma_tiles,),
        in_specs=[dst_spec, pl.BlockSpec((TD, D), lambda i: (i, 0)), pl.BlockSpec(memory_space=pl.ANY)],
        out_specs=pl.BlockSpec(memory_space=pl.ANY),
        out_shape=jax.ShapeDtypeStruct((rows, D), F32),
        scratch_shapes=[pltpu.VMEM((2, TD, D), F32), pltpu.SemaphoreType.DMA((2,))],
        input_output_aliases={2: 0},
        compiler_params=_cparams(("arbitrary",)),
        name="moe_dispatch",
    )(dst, h, zeros)

    TF, F = MOE_TF, D_FF_EXPERT
    yb = pl.pallas_call(
        _expert_kernel,
        grid_spec=pltpu.PrefetchScalarGridSpec(
            num_scalar_prefetch=2,
            grid=(n_blocks, F // TF),
            in_specs=[pl.BlockSpec((R, D), lambda i, f, be, nb: (i, 0)),
                      pl.BlockSpec((1, 1, D, TF), lambda i, f, be, nb: (j, be[i], 0, jnp.where(i < nb[0], f, 0))),
                      pl.BlockSpec((1, 1, D, TF), lambda i, f, be, nb: (j, be[i], 0, jnp.where(i < nb[0], f, 0))),
                      pl.BlockSpec((1, 1, TF, D), lambda i, f, be, nb: (j, be[i], jnp.where(i < nb[0], f, 0), 0))],
            out_specs=pl.BlockSpec((R, D), lambda i, f, be, nb: (i, 0)),
            scratch_shapes=[pltpu.VMEM((R, D), BF16)],
        ),
        out_shape=jax.ShapeDtypeStruct((rows, D), F32),
        compiler_params=_cparams(("arbitrary", "arbitrary")),
        name="moe_experts",
    )(blk_e, n_live, buf, p["moe_w_gate"], p["moe_w_up"], p["moe_w_down"])

    tiles_per_seq_d = S // TD
    out = pl.pallas_call(
        _combine_kernel,
        grid=(n_dma_tiles,),
        in_specs=[pl.BlockSpec((TD, D), lambda i: (i, 0)),
                  pl.BlockSpec((1, 1, 3, D), lambda i: (layer, i // tiles_per_seq_d, 0, 0)),
                  pl.BlockSpec((1, 1, D), lambda i: (layer, 0, 0)),
                  dst_spec,
                  pl.BlockSpec((TD, SUBLANES), lambda i: (jnp.minimum(i + 1, n_dma_tiles - 1), 0),
                               memory_space=pltpu.SMEM),
                  pl.BlockSpec((TD, SUBLANES), lambda i: (i, 0)),
                  pl.BlockSpec(memory_space=pl.ANY)],
        out_specs=pl.BlockSpec((TD, D), lambda i: (i, 0)),
        out_shape=jax.ShapeDtypeStruct((N, D), F32),
        scratch_shapes=[pltpu.VMEM((2, 2, TD, D), F32), pltpu.SemaphoreType.DMA((2,))],
        compiler_params=_cparams(("arbitrary",)),
        name="moe_combine",
    )(xt, mod, p["ffn_post_g"], dst, dst, rw, yb)
    return out.reshape(B, S, D)


NSA_W = D_MODEL + 2 * N_GROUPS * LANES + 4 * N_GROUPS * HEAD_DIM + LANES


def _ones_rows(n):
    r = lax.broadcasted_iota(I32, (V_ROWS - HEAD_DIM, n), 0)
    return jnp.where(r == 0, 1.0, 0.0)


def _nsa_in_kernel(x_ref, mod_ref, g_ref, w_ref, q_ref, ks_ref, vst_ref, kw_ref, vwt_ref,
                   kc_ref, vc_ref, gt_ref, stage_ref):
    D, G, DH = D_MODEL, N_GROUPS, HEAD_DIM
    KV = G * DH
    T = x_ref.shape[1]
    h = _norm_mod(x_ref[0], g_ref[0], mod_ref[0, 0])
    z = _bdot(h, w_ref[...])
    zq_t = z[:, :D].T
    for g in range(G):
        for i in range(T // QT):
            q_ref[0, g, i] = jnp.concatenate(
                [zq_t[(g * HPG + hh) * DH:(g * HPG + hh + 1) * DH, i * QT:(i + 1) * QT] for hh in range(HPG)],
                axis=1).astype(BF16)
    lane = lax.broadcasted_iota(I32, (T, LANES), 1)
    pos = pl.program_id(1) * T + lax.broadcasted_iota(I32, (T, LANES), 0)
    one_col = jnp.where(lane == K_ONE, 1.0, 0.0)
    blk_col = jnp.where(lane == K_PEN + (pos // SEL_BLOCK) % SEL_PEN_ROWS, 1.0, 0.0)
    o = D
    for g in range(G):
        ks_ref[0, g] = (z[:, o + g * LANES:o + (g + 1) * LANES] + (one_col + blk_col)).astype(BF16)
    o += G * LANES
    for g in range(G):
        kw_ref[0, g] = (z[:, o + g * LANES:o + (g + 1) * LANES] + one_col).astype(BF16)
    o += G * LANES
    v_s, v_w = z[:, o:o + KV], z[:, o + KV:o + 2 * KV]
    k_c, v_c = z[:, o + 2 * KV:o + 3 * KV], z[:, o + 3 * KV:o + 4 * KV]
    gates = _sigmoid(z[:, o + 4 * KV:])
    vst = v_s.T
    vwt = v_w.T
    ones = _ones_rows(T)
    half = CMP_LEN // 2
    for g in range(G):
        sl = slice(g * DH, (g + 1) * DH)
        vst_ref[0, g] = jnp.concatenate([vst[sl, :], ones], axis=0).astype(BF16)
        vwt_ref[0, g] = jnp.concatenate([vwt[sl, :], ones], axis=0).astype(BF16)
        for src, out_ref, slot in ((k_c, kc_ref, 2 * g), (v_c, vc_ref, 2 * g + 1)):
            stage_ref[slot] = src[:, sl]
            for pos in range(half):
                out_ref[0, g, :, pos * DH:(pos + 1) * DH] = stage_ref[slot, pl.ds(pos, T // half, stride=half), :]
    gt_ref[0] = gates.T[:3 * N_HEADS, :]


def _gelu_tanh(x):
    return x * (0.5 * (1.0 + jnp.tanh(math.sqrt(2.0 / math.pi) * (x + 0.044715 * (x * x * x)))))


def _compress_kernel(kc_ref, vc_ref, kpe_ref, kw1_ref, kw2_ref, vpe_ref, vw1_ref, vw2_ref,
                     kco_ref, vcto_ref):
    nc = kc_ref.shape[2]

    def comp(c, pe_ref, w1_ref, w2_ref):
        a = _bdot(c + pe_ref[0:1], w1_ref[0])
        b = _bdot(c + pe_ref[1:2], w1_ref[1])
        b_next = pltpu.roll(b, nc - 1, axis=0)
        out = _bdot(_gelu_tanh(a + b_next), w2_ref[...])
        row = lax.broadcasted_iota(I32, out.shape, 0)
        return jnp.where(row < nc - 1, out, 0.0)

    kc = comp(kc_ref[0, 0], kpe_ref, kw1_ref, kw2_ref)
    lane = lax.broadcasted_iota(I32, kc.shape, 1)
    kco_ref[0, 0] = (kc + jnp.where(lane == K_ONE, 1.0, 0.0)).astype(BF16)
    vct = comp(vc_ref[0, 0], vpe_ref, vw1_ref, vw2_ref).T
    vcto_ref[0, 0] = jnp.concatenate([vct, _ones_rows(nc)], axis=0).astype(BF16)


def _bias_table_kernel(tbl_ref, tab_ref, ctab_ref, far_ref):
    g = pl.program_id(0)

    def bias_of(dist, h, far):
        v = jnp.full(dist.shape, tbl_ref[0, h], F32)
        for b in range(1, REL_BUCKETS):
            v = jnp.where(dist >= BUCKET_THR[b], tbl_ref[b, h], v)
        return jnp.where(dist >= 0, v * LOG2E - far, NEG)

    nc = (ctab_ref.shape[1] - LANES) // 2
    key = lax.broadcasted_iota(I32, (QT, QT), 0)
    qry = lax.broadcasted_iota(I32, (QT, QT), 1)
    crow = lax.broadcasted_iota(I32, (CMP_NEAR, QT), 0)
    cqry = lax.broadcasted_iota(I32, (CMP_NEAR, QT), 1)
    for hh in range(HPG):
        h = g * HPG + hh
        lanes = slice(hh * QT, (hh + 1) * QT)
        far = jnp.full((1, QT), tbl_ref[REL_BUCKETS - 1, h] * LOG2E, F32).astype(BF16).astype(F32)
        far_ref[0, :, lanes] = far

        def near_slot(slot, carry, h=h, lanes=lanes, far=far):
            tab_ref[0, slot, :, lanes] = bias_of(slot * QT + qry - key, h, far)
            return carry

        lax.fori_loop(0, NEAR_SLOTS, near_slot, 0)
        tab_ref[0, SLOT_FAR, :, lanes] = jnp.zeros((QT, QT), F32)
        tab_ref[0, SLOT_NONE, :, lanes] = jnp.full((QT, QT), NEG, F32)
        dw = WINDOW + qry - key
        tab_ref[0, SLOT_WIN, :, lanes] = jnp.where(dw < WINDOW, bias_of(dw, h, far), NEG)
        ctab_ref[0, 0:nc, lanes] = jnp.zeros((nc, QT), F32)
        dc = cqry - CMP_STRIDE * (crow - (CMP_NEAR - SUBLANES)) - (CMP_LEN - 1)
        ctab_ref[0, nc:nc + CMP_NEAR, lanes] = bias_of(dc, h, far)
        ctab_ref[0, nc + CMP_NEAR:, lanes] = jnp.full((nc + LANES - CMP_NEAR, QT), NEG, F32)


def _nsa_attn_kernel(q_ref, ks_ref, vst_ref, kw_refs, vwt_refs, kc_ref, vct_ref, gt_ref,
                     tab_ref, ctab_ref, far_ref, o_ref, qt_ref, s_even_ref, s_odd_ref, psum_ref, pen_ref, ocw_ref,
                     smax_ref):
    DH = HEAD_DIM
    a = pl.program_id(2)
    nc = kc_ref.shape[2]
    n_sel = pen_ref.shape[0]

    aug_row = lax.broadcasted_iota(I32, (LANES - DH, QL), 0)
    aug = jnp.where(aug_row == 0, far_ref[0], 0.0).astype(BF16)
    for i in range(2):
        qt_ref[i, 0:DH, :] = q_ref[0, 0, 0]
        qt_ref[i, DH:, :] = aug
    qt = qt_ref[0]

    def window():
        nw = WINDOW // QT + 1
        s_w = []
        for d in range(nw):
            slot = SLOT_WIN if d == nw - 1 else d
            if d > 0:
                slot = jnp.where(a >= d, slot, SLOT_NONE)
            s_w.append(jnp.dot(kw_refs[d][0, 0], qt, preferred_element_type=F32) + tab_ref[0, slot])
        m_w = functools.reduce(jnp.maximum, [jnp.max(s_d, axis=0, keepdims=True) for s_d in s_w])
        acc_w = functools.reduce(lambda x, y: x + y, [
            jnp.dot(vwt_refs[d][0, 0], jnp.exp2(s_w[d] - m_w).astype(BF16), preferred_element_type=F32)
            for d in range(nw)])
        ocw_ref[1] = acc_w[0:DH] * (1.0 / acc_w[DH:DH + 1])

    n_hi = (QT // CMP_STRIDE) * (a + 1)
    c0 = pl.multiple_of(nc - (n_hi - CMP_NEAR), SUBLANES)
    ratio = SEL_BLOCK // CMP_STRIDE

    def compressed_and_select(rows_c):
        rows_s = rows_c // ratio
        s = (jnp.dot(kc_ref[0, 0, 0:rows_c, :], qt, preferred_element_type=F32)
             + ctab_ref[0, pl.ds(c0, rows_c), :])
        m = jnp.max(s, axis=0, keepdims=True)
        e = jnp.exp2(s - m)
        l = jnp.sum(e, axis=0, keepdims=True)
        pn = e * jnp.where(m > 0.1 * NEG, 1.0 / l, 0.0)
        ocw_ref[0] = jnp.dot(vct_ref[0, 0, :, 0:rows_c], pn.astype(BF16), preferred_element_type=F32)[0:DH]
        psum_ref[0:rows_c, :] = pn[:, 0:QT] + pn[:, QT:2 * QT] + pn[:, 2 * QT:3 * QT] + pn[:, 3 * QT:4 * QT]

        imp = psum_ref[pl.ds(0, rows_s, stride=ratio), :]
        for jj in range(1, ratio):
            imp = imp + psum_ref[pl.ds(jj, rows_s, stride=ratio), :]
        last = psum_ref[pl.ds(ratio - 1, rows_s, stride=ratio), :]
        srow = lax.broadcasted_iota(I32, (rows_s, QT), 0)
        imp = imp + jnp.where(srow > 0, pltpu.roll(last, 1, axis=0), 0.0)
        lane_q = lax.broadcasted_iota(I32, (rows_s, QT), 1)
        cur = (QT // SEL_BLOCK) * a + lane_q // SEL_BLOCK
        back = cur - srow
        valid = back >= 0
        forced = (srow == 0) | (valid & (back < N_LOCAL_BLOCKS))
        n_forced = 1 + N_LOCAL_BLOCKS
        score = jnp.where(forced, -jnp.inf, jnp.where(valid, imp, -1.0))
        srow_f = srow.astype(F32)
        for _ in range(min(SEL_TOPK, n_sel) - n_forced):
            mx = jnp.max(score, axis=0, keepdims=True)
            first = jnp.min(jnp.where(score == mx, srow_f, float(rows_s)), axis=0, keepdims=True)
            score = jnp.where(srow_f == first, -jnp.inf, score)
        pen = jnp.where(score == -jnp.inf, 0.0, NEG)
        for hh in range(HPG):
            pen_ref[0:rows_s, hh * QT:(hh + 1) * QT] = pen
        if rows_s < n_sel:
            pen_ref[rows_s:, :] = jnp.full((n_sel - rows_s, QL), NEG, F32)
        window()

    big = SEL_SUB * QT
    span = SEL_STREAMS * big
    sub_per_step = SEL_STREAMS * SEL_SUB

    n_span = ks_ref.shape[2] // span

    s_refs = (s_even_ref, s_odd_ref)

    def scores(step, buf, tables):
        sc = jnp.minimum(step, n_span - 1)
        pen_rows = pen_ref[pl.ds(pl.multiple_of(sc * SEL_PEN_ROWS, SEL_PEN_ROWS), SEL_PEN_ROWS), :]
        qt_ref[buf, K_PEN:K_PEN + SEL_PEN_ROWS, :] = pen_rows.astype(BF16)
        k0 = pl.multiple_of(sc * span, span)
        s_all = jnp.dot(ks_ref[0, 0, pl.ds(k0, span), :], qt_ref[buf], preferred_element_type=F32)
        for t in range(SEL_STREAMS):
            s = s_all[t * big:(t + 1) * big]
            if tables:
                parts = []
                for u in range(SEL_SUB):
                    d = a - (step * sub_per_step + t * SEL_SUB + u)
                    slot = jnp.where(d >= 0, jnp.minimum(d, SLOT_FAR), SLOT_NONE)
                    parts.append(s[u * QT:(u + 1) * QT] + tab_ref[0, slot])
                s = jnp.concatenate(parts, axis=0)
            s_refs[buf][t * big:(t + 1) * big, :] = s
            smax_ref[buf, t] = jnp.max(s, axis=0, keepdims=True)

    def update(step, buf, state):
        k0 = pl.multiple_of(jnp.minimum(step, n_span - 1) * span, span)
        out = []
        for t in range(SEL_STREAMS):
            m, acc = state[t]
            m_new = jnp.maximum(m, smax_ref[buf, t])
            p = jnp.exp2(s_refs[buf][t * big:(t + 1) * big, :] - m_new).astype(BF16)
            acc = jnp.exp2(m - m_new) * acc + jnp.dot(vst_ref[0, 0, :, pl.ds(k0 + t * big, big)], p,
                                                      preferred_element_type=F32)
            out.append((m_new, acc))
        return tuple(out)

    def two_steps(jj, state, near):
        scores(2 * jj + 1, 1, near)
        state = update(2 * jj, 0, state)
        scores(2 * jj + 2, 0, True)
        return update(2 * jj + 1, 1, state)

    sizes = [r for r in (nc // 8, nc // 4, nc // 2, nc) if r >= CMP_NEAR + 2 * SUBLANES]
    tiles_per = QT // CMP_STRIDE
    for idx, rows_c in enumerate(sizes):
        lo = 0 if idx == 0 else sizes[idx - 1] // tiles_per
        hi = rows_c // tiles_per
        pl.when((a >= lo) & (a < hi))(functools.partial(compressed_and_select, rows_c))
    o_c = ocw_ref[0]
    o_w = ocw_ref[1]

    n_steps = a // sub_per_step + 1
    n_far2 = jnp.maximum(a - (NEAR_SLOTS - 1), 0) // (2 * sub_per_step)
    n_pairs = n_steps // 2
    scores(0, 0, True)
    state = tuple((jnp.full((1, QL), NEG, F32), jnp.zeros((V_ROWS, QL), F32)) for _ in range(SEL_STREAMS))
    state = lax.fori_loop(0, n_far2, functools.partial(two_steps, near=False), state)
    state = lax.fori_loop(n_far2, n_pairs, functools.partial(two_steps, near=True), state)
    state = lax.cond(n_steps % 2 == 1, lambda st: update(n_steps - 1, 0, st), lambda st: st, state)
    m_s = functools.reduce(jnp.maximum, [m for m, _ in state])
    acc_s = functools.reduce(lambda x, y: x + y, [jnp.exp2(m - m_s) * acc for m, acc in state])
    o_s = acc_s[0:DH] * (1.0 / acc_s[DH:DH + 1])

    def gate(branch):
        return jnp.concatenate([gt_ref[0, 0, 3 * hh + branch:3 * hh + branch + 1, :] for hh in range(HPG)],
                               axis=1)

    o_ref[0, 0, 0] = (gate(0) * o_c + gate(1) * o_s + gate(2) * o_w).astype(BF16)


def _nsa_attn_wrapper(*refs):
    nw = WINDOW // QT + 1
    q_ref, ks_ref, vst_ref = refs[0:3]
    kw_refs = refs[3:3 + nw]
    vwt_refs = refs[3 + nw:3 + 2 * nw]
    rest = refs[3 + 2 * nw:]
    _nsa_attn_kernel(q_ref, ks_ref, vst_ref, kw_refs, vwt_refs, *rest)


def _out_proj_kernel(o_ref, x_ref, mod_ref, w_ref, pg_ref, out_ref):
    n_tiles = o_ref.shape[2]
    o_t = jnp.concatenate(
        [jnp.concatenate([o_ref[0, g, i, :, hh * QT:(hh + 1) * QT] for i in range(n_tiles)], axis=1)
         for g in range(N_GROUPS) for hh in range(HPG)], axis=0)
    y = jnp.dot(o_t.astype(F32).T.astype(BF16), w_ref[...], preferred_element_type=F32)
    out_ref[0] = _post_residual(x_ref[0], y, pg_ref[0], mod_ref[0, 0])


def _nsa_layer(x, mod, layer, j, p):
    B, S, D = x.shape
    G, DH, KV = N_GROUPS, HEAD_DIM, N_GROUPS * HEAD_DIM
    T = TOK_TILE
    NC = S // CMP_STRIDE
    n_sel = S // SEL_BLOCK
    assert D == D_MODEL and S % (2 * SEL_STREAMS * SEL_SUB * QT) == 0 and S % T == 0

    w_in = p["nsa_w_in"][j]
    cuts = [D + i * KV for i in range(7)]
    wq, wkc, wvc, wks, wvs, wkw, wvw, wg = jnp.split(w_in, cuts, axis=-1)
    wg = jnp.zeros((D, LANES), F32).at[:, :3 * N_HEADS].set(wg)

    def slabs(w):
        return jnp.zeros((D, G, LANES), F32).at[:, :, :DH].set(w.reshape(D, G, DH)).reshape(D, G * LANES)

    w_all = jnp.concatenate([wq * (HEAD_DIM ** -0.5 * LOG2E), slabs(wks), slabs(wkw), wvs, wvw, wkc, wvc, wg],
                            axis=1).astype(BF16)

    qo_spec = pl.BlockSpec((1, G, T // QT, DH, QL), lambda b, i: (b, 0, i, 0, 0))
    k_spec = pl.BlockSpec((1, G, T, LANES), lambda b, i: (b, 0, i, 0))
    half = CMP_LEN // 2
    chunk_w = half * DH
    kc_spec = pl.BlockSpec((1, G, T // half, chunk_w), lambda b, i: (b, 0, i, 0))
    vt_spec = pl.BlockSpec((1, G, V_ROWS, T), lambda b, i: (b, 0, 0, i))
    q, ks, vst, kw, vwt, kc_raw, vc_raw, gt = pl.pallas_call(
        _nsa_in_kernel,
        grid=(B, S // T),
        in_specs=[pl.BlockSpec((1, T, D), lambda b, i: (b, i, 0)), _mod_spec(layer), _vec_spec(layer),
                  pl.BlockSpec((D, NSA_W), lambda b, i: (0, 0))],
        out_specs=[qo_spec, k_spec, vt_spec, k_spec, vt_spec,
                   kc_spec, kc_spec, pl.BlockSpec((1, 3 * N_HEADS, T), lambda b, i: (b, 0, i))],
        out_shape=[jax.ShapeDtypeStruct((B, G, S // QT, DH, QL), BF16),
                   jax.ShapeDtypeStruct((B, G, S, LANES), BF16), jax.ShapeDtypeStruct((B, G, V_ROWS, S), BF16),
                   jax.ShapeDtypeStruct((B, G, S, LANES), BF16), jax.ShapeDtypeStruct((B, G, V_ROWS, S), BF16),
                   jax.ShapeDtypeStruct((B, G, NC, chunk_w), F32), jax.ShapeDtypeStruct((B, G, NC, chunk_w), F32),
                   jax.ShapeDtypeStruct((B, 3 * N_HEADS, S), F32)],
        scratch_shapes=[pltpu.VMEM((2 * G, T, DH), F32)],
        compiler_params=_cparams(("arbitrary", "arbitrary")),
        name="nsa_in_proj",
    )(x, mod, p["mix_pre_g"], w_all)


    def split_w1(w1):
        return w1.reshape(2, chunk_w, CMP_HIDDEN).astype(BF16)

    def split_pe(pe):
        return pe.reshape(2, chunk_w)

    c_spec = pl.BlockSpec((1, 1, NC, chunk_w), lambda b, g: (b, g, 0, 0))
    full = lambda shape: pl.BlockSpec(shape, lambda b, g: (0,) * len(shape))
    kc, vct = pl.pallas_call(
        _compress_kernel,
        grid=(B, G),
        in_specs=[c_spec, c_spec,
                  full((2, chunk_w)), full((2, chunk_w, CMP_HIDDEN)), full((CMP_HIDDEN, LANES)),
                  full((2, chunk_w)), full((2, chunk_w, CMP_HIDDEN)), full((CMP_HIDDEN, DH))],
        out_specs=[pl.BlockSpec((1, 1, NC, LANES), lambda b, g: (b, g, 0, 0)),
                   pl.BlockSpec((1, 1, V_ROWS, NC), lambda b, g: (b, g, 0, 0))],
        out_shape=[jax.ShapeDtypeStruct((B, G, NC, LANES), BF16),
                   jax.ShapeDtypeStruct((B, G, V_ROWS, NC), BF16)],
        compiler_params=_cparams(("arbitrary", "arbitrary")),
        name="nsa_compress",
    )(kc_raw, vc_raw,
      split_pe(p["nsa_k_pe"][j]), split_w1(p["nsa_k_w1"][j]),
      jnp.zeros((CMP_HIDDEN, LANES), BF16).at[:, :DH].set(p["nsa_k_w2"][j].astype(BF16)),
      split_pe(p["nsa_v_pe"][j]), split_w1(p["nsa_v_w1"][j]), p["nsa_v_w2"][j].astype(BF16))

    ctab_rows = 2 * NC + LANES
    tab, ctab, far = pl.pallas_call(
        _bias_table_kernel,
        grid=(G,),
        in_specs=[pl.BlockSpec(memory_space=pltpu.SMEM)],
        out_specs=[pl.BlockSpec((1, N_SLOTS, QT, QL), lambda g: (g, 0, 0, 0)),
                   pl.BlockSpec((1, ctab_rows, QL), lambda g: (g, 0, 0)),
                   pl.BlockSpec((1, 1, QL), lambda g: (g, 0, 0))],
        out_shape=[jax.ShapeDtypeStruct((G, N_SLOTS, QT, QL), F32),
                   jax.ShapeDtypeStruct((G, ctab_rows, QL), F32),
                   jax.ShapeDtypeStruct((G, 1, QL), F32)],
        compiler_params=_cparams(("arbitrary",)),
        name="nsa_bias_tables",
    )(p["rel_table"])

    nw = WINDOW // QT + 1
    kw_specs = [pl.BlockSpec((1, 1, QT, LANES), functools.partial(
        lambda b, g, a, d: (b, g, jnp.maximum(a - d, 0), 0), d=d)) for d in range(nw)]
    vwt_specs = [pl.BlockSpec((1, 1, V_ROWS, QT), functools.partial(
        lambda b, g, a, d: (b, g, 0, jnp.maximum(a - d, 0)), d=d)) for d in range(nw)]
    o = pl.pallas_call(
        _nsa_attn_wrapper,
        grid=(B, G, S // QT),
        in_specs=[pl.BlockSpec((1, 1, 1, DH, QL), lambda b, g, a: (b, g, a, 0, 0)),
                  pl.BlockSpec((1, 1, S, LANES), lambda b, g, a: (b, g, 0, 0)),
                  pl.BlockSpec((1, 1, V_ROWS, S), lambda b, g, a: (b, g, 0, 0)),
                  *kw_specs, *vwt_specs,
                  pl.BlockSpec((1, 1, NC, LANES), lambda b, g, a: (b, g, 0, 0)),
                  pl.BlockSpec((1, 1, V_ROWS, NC), lambda b, g, a: (b, g, 0, 0)),
                  pl.BlockSpec((1, 1, 3 * HPG, QT), lambda b, g, a: (b, g, 0, a)),
                  pl.BlockSpec((1, N_SLOTS, QT, QL), lambda b, g, a: (g, 0, 0, 0)),
                  pl.BlockSpec((1, ctab_rows, QL), lambda b, g, a: (g, 0, 0)),
                  pl.BlockSpec((1, 1, QL), lambda b, g, a: (g, 0, 0))],
        out_specs=pl.BlockSpec((1, 1, 1, DH, QL), lambda b, g, a: (b, g, a, 0, 0)),
        out_shape=jax.ShapeDtypeStruct((B, G, S // QT, DH, QL), BF16),
        scratch_shapes=[pltpu.VMEM((2, LANES, QL), BF16),
                        pltpu.VMEM((SEL_STREAMS * SEL_SUB * QT, QL), F32),
                        pltpu.VMEM((SEL_STREAMS * SEL_SUB * QT, QL), F32),
                        pltpu.VMEM((NC, QT), F32), pltpu.VMEM((n_sel, QL), F32),
                        pltpu.VMEM((2, DH, QL), F32), pltpu.VMEM((2, SEL_STREAMS, 1, QL), F32)],
        compiler_params=_cparams(("arbitrary", "arbitrary", "arbitrary")),
        name="nsa_attention",
    )(q, ks, vst, *([kw] * nw), *([vwt] * nw), kc, vct, gt.reshape(B, G, 3 * HPG, S), tab, ctab, far)

    return pl.pallas_call(
        _out_proj_kernel,
        grid=(B, S // T),
        in_specs=[qo_spec,
                  pl.BlockSpec((1, T, D), lambda b, i: (b, i, 0)), _mod_spec(layer),
                  pl.BlockSpec((D, D), lambda b, i: (0, 0)), _vec_spec(layer)],
        out_specs=pl.BlockSpec((1, T, D), lambda b, i: (b, i, 0)),
        out_shape=jax.ShapeDtypeStruct((B, S, D), F32),
        compiler_params=_cparams(("arbitrary", "arbitrary")),
        name="nsa_out_proj",
    )(o, x, mod, p["nsa_w_out"][j].astype(BF16), p["mix_post_g"])


def _as_rows(v):
    return v.reshape(v.shape[0], 1, v.shape[-1])


def kernel(x, c, mix_pre_g, mix_post_g, mix_ada_w, mix_ada_b, ffn_pre_g, ffn_post_g, ffn_ada_w, ffn_ada_b, conf_w_pw1, conf_b_pw1, conf_w_dw, conf_b_dw, conf_ln_g, conf_ln_b, conf_w_pw2, conf_b_pw2, sc_w_in, sc_w_conv, sc_w_out, nsa_w_in, nsa_k_pe, nsa_k_w1, nsa_k_w2, nsa_v_pe, nsa_v_w1, nsa_v_w2, nsa_w_out, rel_table, dense_w_gate, dense_w_up, dense_w_down, moe_w_router, moe_b_router, moe_w_gate, moe_w_up, moe_w_down):
    p = dict(
        mix_pre_g=_as_rows(mix_pre_g), mix_post_g=_as_rows(mix_post_g),
        ffn_pre_g=_as_rows(ffn_pre_g), ffn_post_g=_as_rows(ffn_post_g),
        conf_w_pw1=conf_w_pw1, conf_b_pw1=_as_rows(conf_b_pw1), conf_w_dw=conf_w_dw,
        conf_b_dw=_as_rows(conf_b_dw), conf_ln_g=_as_rows(conf_ln_g), conf_ln_b=_as_rows(conf_ln_b),
        conf_w_pw2=conf_w_pw2, conf_b_pw2=_as_rows(conf_b_pw2),
        sc_w_in=sc_w_in, sc_w_conv=sc_w_conv, sc_w_out=sc_w_out,
        nsa_w_in=nsa_w_in, nsa_k_pe=nsa_k_pe, nsa_k_w1=nsa_k_w1, nsa_k_w2=nsa_k_w2,
        nsa_v_pe=nsa_v_pe, nsa_v_w1=nsa_v_w1, nsa_v_w2=nsa_v_w2, nsa_w_out=nsa_w_out,
        rel_table=rel_table,
        dense_w_gate=dense_w_gate, dense_w_up=dense_w_up, dense_w_down=dense_w_down,
        moe_w_router=moe_w_router, moe_b_router=moe_b_router,
        moe_w_gate=moe_w_gate, moe_w_up=moe_w_up, moe_w_down=moe_w_down,
    )
    mix_mod, ffn_mod = _adaln(c, mix_ada_w, mix_ada_b, ffn_ada_w, ffn_ada_b)
    mixers = (_conformer_layer, _short_conv_layer, _nsa_layer)
    for i in range(DEPTH):
        x = mixers[i % 3](x, mix_mod, i, i // 3, p)
        ffn = _dense_ffn_layer if i % 2 == 0 else _moe_ffn_layer
        x = ffn(x, ffn_mod, i, i // 2, p)
    return x
```
